```python
import math, functools
import jax, jax.numpy as jnp
from jax import lax
import numpy as np

D_MODEL = 4096
BATCH = 1
SEQ = 8192
DEPTH = 1
DEC_BATCH = 128
DEC_SEQ = 4
PAST_LEN = 2048
PAGE_SIZE = 128

N_HEADS = 16
N_KV_HEADS = 4
HEAD_DIM = 128
ATTN_WIDTH = N_HEADS * HEAD_DIM
KV_WIDTH = N_KV_HEADS * HEAD_DIM
N_IDX_HEADS = 32
IDX_DIM = 128
IDX_W_SCALE = (N_IDX_HEADS * IDX_DIM) ** -0.5
TOP_K_MAX = 256
Q_BLOCK = 128
N_BUCKETS = 32
MAX_DISTANCE = 128
GDN_HEADS = 16
GDN_DK = 128
GDN_DV = 128
GDN_KW = GDN_HEADS * GDN_DK
GDN_VW = GDN_HEADS * GDN_DV
CONV_WIDTH = 4
CONV_CH = 2 * GDN_KW + GDN_VW
GDN_CHUNK = 64
D_FF = -(-8 * D_MODEL // (3 * 256)) * 256
PLE_DIM = 256
EPS = 1e-6
SPLIT_SIZES = (ATTN_WIDTH, KV_WIDTH, KV_WIDTH, N_IDX_HEADS * IDX_DIM, IDX_DIM, N_IDX_HEADS,
               CONV_CH, GDN_HEADS, GDN_HEADS, GDN_VW, D_MODEL, D_MODEL)
IN_COLS = sum(SPLIT_SIZES)

kernel_name = "hybrid_dsa_gdn_decode_step"


def rmsnorm(x, w):
    xf = x.astype(jnp.float32)
    y = xf * lax.rsqrt(jnp.mean(xf * xf, axis=-1, keepdims=True) + EPS)
    return (y * w.astype(jnp.float32)).astype(x.dtype)


def l2norm(x):
    return x * lax.rsqrt(jnp.sum(x * x, axis=-1, keepdims=True) + 1e-6)


def t5_bucket(dist):
    max_exact = N_BUCKETS // 2
    n = jnp.maximum(dist, 0)
    large = max_exact + (jnp.log(jnp.maximum(n, 1).astype(jnp.float32) / max_exact)
                         / math.log(MAX_DISTANCE / max_exact) * (N_BUCKETS - max_exact)).astype(jnp.int32)
    large = jnp.minimum(large, N_BUCKETS - 1)
    return jnp.where(n < max_exact, n, large)


def index_select(qi, wi, ki, q_pos, n_sel):
    s = jax.nn.relu(jnp.einsum('bthd,bld->bthl', qi.astype(jnp.float32), ki.astype(jnp.float32)))
    score = jnp.einsum('bth,bthl->btl', wi.astype(jnp.float32), s)
    key_pos = jnp.arange(ki.shape[1])
    score = jnp.where(key_pos[None, None, :] <= q_pos[None, :, None], score, -jnp.inf)
    return lax.top_k(score, n_sel)[1]


def sparse_attend(q, k_sel, v_sel, idx, q_pos, rel_bias):
    B, T = q.shape[:2]
    K = idx.shape[-1]
    G = N_HEADS // N_KV_HEADS
    qg = q.reshape(B, T, N_KV_HEADS, G, HEAD_DIM)
    logits = jnp.einsum('btkgd,btskd->btkgs', qg, k_sel).astype(jnp.float32) * HEAD_DIM ** -0.5
    dist = q_pos[None, :, None] - idx
    bias = rel_bias[t5_bucket(dist)].reshape(B, T, K, N_KV_HEADS, G).transpose(0, 1, 3, 4, 2)
    logits = jnp.where((dist >= 0)[:, :, None, None, :], logits + bias.astype(jnp.float32), -jnp.inf)
    probs = jax.nn.softmax(logits, axis=-1)
    out = jnp.einsum('btkgs,btskd->btkgd', probs.astype(v_sel.dtype), v_sel)
    return out.reshape(B, T, ATTN_WIDTH)


def gather_rows(rows, ii):
    return jax.vmap(lambda r, i: r[i])(rows, ii)


def prompt_attention(q, k, v, qi, ki, wi, rel_bias):
    B, S = q.shape[:2]
    n_sel = min(TOP_K_MAX, S // 4)

    def block(i):
        start = i * Q_BLOCK
        q_pos = start + jnp.arange(Q_BLOCK)
        sl = lambda a: lax.dynamic_slice_in_dim(a, start, Q_BLOCK, axis=1)
        idx = index_select(sl(qi), sl(wi), ki, q_pos, n_sel)
        return sparse_attend(sl(q), gather_rows(k, idx), gather_rows(v, idx), idx, q_pos, rel_bias)

    out = lax.map(block, jnp.arange(S // Q_BLOCK))
    return out.transpose(1, 0, 2, 3).reshape(B, S, ATTN_WIDTH)


def sample_attention(q, k, v, qi, ki, wi, cache_k, cache_v, cache_idx_k, page_table, rel_bias):
    DB, T = q.shape[:2]
    L = PAST_LEN + T
    n_sel = min(TOP_K_MAX, L // 4)
    q_pos = PAST_LEN + jnp.arange(T)
    ki_past = cache_idx_k[page_table].reshape(DB, PAST_LEN, IDX_DIM)
    ki_all = jnp.concatenate([ki_past.astype(ki.dtype), ki], axis=1)
    idx = index_select(qi, wi, ki_all, q_pos, n_sel)
    in_past = (idx < PAST_LEN)[..., None, None]
    ip = jnp.minimum(idx, PAST_LEN - 1)
    phys = gather_rows(page_table, ip // PAGE_SIZE) * PAGE_SIZE + ip % PAGE_SIZE
    inew = jnp.clip(idx - PAST_LEN, 0, T - 1)
    k_flat = cache_k.reshape(-1, N_KV_HEADS, HEAD_DIM)
    v_flat = cache_v.reshape(-1, N_KV_HEADS, HEAD_DIM)
    k_sel = jnp.where(in_past, k_flat[phys].astype(k.dtype), gather_rows(k, inew))
    v_sel = jnp.where(in_past, v_flat[phys].astype(v.dtype), gather_rows(v, inew))
    return sparse_attend(q, k_sel, v_sel, idx, q_pos, rel_bias)


def causal_conv(x, buf, w):
    T = x.shape[1]
    xp = jnp.concatenate([buf.astype(x.dtype), x], axis=1)
    y = w[0] * xp[:, 0:T]
    for j in range(1, CONV_WIDTH):
        y = y + w[j] * xp[:, j:j + T]
    return y, xp[:, -(CONV_WIDTH - 1):]


def gdn_step(S, inp):
    u, w, qd, a_in, kt, gl = inp
    v_new = u - jnp.einsum('bhcd,bhde->bhce', w, S)
    o = jnp.einsum('bhcd,bhde->bhce', qd, S) + jnp.einsum('bhcs,bhse->bhce', a_in, v_new)
    S = S * gl[..., None, None] + jnp.einsum('bhcd,bhce->bhde', kt, v_new)
    return S, o


def gated_delta_chunked(q, k, v, g, beta, state):
    B, T, H, _ = q.shape
    f32 = jnp.float32
    C = min(GDN_CHUNK, T)
    n = -(-T // C)
    pad = n * C - T
    q = l2norm(q.astype(f32)) * GDN_DK ** -0.5
    k = l2norm(k.astype(f32))

    def chunks(a):
        a = jnp.pad(a.astype(f32), [(0, 0), (0, pad)] + [(0, 0)] * (a.ndim - 2))
        a = a.reshape((B, n, C) + a.shape[2:])
        return jnp.moveaxis(a, 3, 2).swapaxes(0, 1)

    q, k, v, g, beta = chunks(q), chunks(k), chunks(v), chunks(g), chunks(beta)
    gc = jnp.cumsum(g, axis=-1)
    causal = jnp.tril(jnp.ones((C, C), bool))
    strict = jnp.tril(jnp.ones((C, C), bool), -1)
    decay = jnp.exp(jnp.where(causal, gc[..., :, None] - gc[..., None, :], -jnp.inf))
    kb = k * beta[..., None]
    a_ut = jnp.where(strict, jnp.einsum('nbhcd,nbhsd->nbhcs', kb, k) * decay, 0.0)
    eye = jnp.eye(C, dtype=f32)
    t_inv = lax.linalg.triangular_solve(eye + a_ut, jnp.broadcast_to(eye, a_ut.shape),
                                        left_side=True, lower=True)
    u = t_inv @ (v * beta[..., None])
    w = t_inv @ (kb * jnp.exp(gc)[..., None])
    a_in = jnp.where(causal, jnp.einsum('nbhcd,nbhsd->nbhcs', q, k) * decay, 0.0)
    qd = q * jnp.exp(gc)[..., None]
    kt = k * jnp.exp(gc[..., -1:] - gc)[..., None]
    gl = jnp.exp(gc[..., -1])
    S, o = lax.scan(gdn_step, state.astype(f32), (u, w, qd, a_in, kt, gl))
    o = jnp.moveaxis(o.swapaxes(0, 1), 2, 3).reshape(B, n * C, H, GDN_DV)[:, :T]
    return o, S


def decoder_layer(x, p, attn_fn, conv_buf, ssm_state, norm_mix, w_in, conv_w, a_log, dt_bias, gdn_norm,
                  w_attn_up, w_gdn_up, w_out, norm_ffn, w_gate_up, w_down, norm_ple, w_ple_gate, w_ple):
    B, T, _ = x.shape
    f32 = jnp.float32
    h = rmsnorm(x, norm_mix)
    offsets = [int(o) for o in np.cumsum(SPLIT_SIZES)[:-1]]
    q, k, v, qi, ki, wi, qkv, ga, gb, gz, gate_a, gate_b = jnp.split(h @ w_in, offsets, axis=-1)
    q = q.reshape(B, T, N_HEADS, HEAD_DIM)
    k = k.reshape(B, T, N_KV_HEADS, HEAD_DIM)
    v = v.reshape(B, T, N_KV_HEADS, HEAD_DIM)
    qi = qi.reshape(B, T, N_IDX_HEADS, IDX_DIM)
    attn = attn_fn(q, k, v, qi, ki, wi * IDX_W_SCALE)
    qkv, new_conv = causal_conv(qkv, conv_buf, conv_w)
    gq, gk, gv = jnp.split(jax.nn.silu(qkv), [GDN_KW, 2 * GDN_KW], axis=-1)
    g = -jnp.exp(a_log.astype(f32)) * jax.nn.softplus(ga.astype(f32) + dt_bias.astype(f32))
    beta = jax.nn.sigmoid(gb.astype(f32))
    o, new_ssm = gated_delta_chunked(gq.reshape(B, T, GDN_HEADS, GDN_DK), gk.reshape(B, T, GDN_HEADS, GDN_DK),
                                     gv.reshape(B, T, GDN_HEADS, GDN_DV), g, beta, ssm_state)
    o = rmsnorm(o.astype(x.dtype), gdn_norm) * jax.nn.silu(gz.reshape(B, T, GDN_HEADS, GDN_DV))
    merged = (jax.nn.sigmoid(gate_a) * (attn @ w_attn_up)
              + jax.nn.sigmoid(gate_b) * (o.reshape(B, T, GDN_VW) @ w_gdn_up))
    x = x + merged @ w_out
    gt, up = jnp.split(rmsnorm(x, norm_ffn) @ w_gate_up, 2, axis=-1)
    x = x + (jax.nn.silu(gt) * up) @ w_down
    x = x + jax.nn.sigmoid(rmsnorm(x, norm_ple) @ w_ple_gate) * (p.astype(x.dtype) @ w_ple)
    return x, k, v, ki, new_conv, new_ssm.astype(ssm_state.dtype)


def setup_inputs(seed: int = 0) -> dict:
    key = jax.random.key(seed)
    ks = jax.random.split(key, 32)
    f32 = jnp.float32
    n_pages = PAST_LEN // PAGE_SIZE
    n_used = DEC_BATCH * n_pages
    n_pool = (5 * n_used + 3) // 4
    nrm = lambda k, shape, scale: jax.random.normal(k, shape, f32) * scale
    gain = lambda k, shape: 1.0 + 0.02 * jax.random.normal(k, shape, f32)
    page_table = jax.random.permutation(ks[7], n_pool)[:n_used].reshape(DEC_BATCH, n_pages).astype(jnp.int32)
    dt = jnp.exp(jax.random.uniform(ks[12], (DEPTH, GDN_HEADS), f32, math.log(1e-3), math.log(1e-1)))
    return {
        "x_prompt": nrm(ks[0], (BATCH, SEQ, D_MODEL), 1.0),
        "x_sample": nrm(ks[1], (DEC_BATCH, DEC_SEQ, D_MODEL), 1.0),
        "cache_k": nrm(ks[2], (DEPTH, n_pool, PAGE_SIZE, N_KV_HEADS, HEAD_DIM), 1.0),
        "cache_v": nrm(ks[3], (DEPTH, n_pool, PAGE_SIZE, N_KV_HEADS, HEAD_DIM), 1.0),
        "cache_idx_k": nrm(ks[4], (DEPTH, n_pool, PAGE_SIZE, IDX_DIM), 1.0),
        "state_conv": nrm(ks[5], (DEPTH, DEC_BATCH, CONV_WIDTH - 1, CONV_CH), 1.0),
        "state_ssm": nrm(ks[6], (DEPTH, DEC_BATCH, GDN_HEADS, GDN_DK, GDN_DV), 0.5),
        "page_table": page_table,
        "p_prompt": nrm(ks[8], (DEPTH, BATCH, SEQ, PLE_DIM), 1.0),
        "p_sample": nrm(ks[9], (DEPTH, DEC_BATCH, DEC_SEQ, PLE_DIM), 1.0),
        "rel_bias": nrm(ks[10], (N_BUCKETS, N_HEADS), 0.5),
        "norm_mix": gain(ks[11], (DEPTH, D_MODEL)),
        "w_in": nrm(ks[13], (DEPTH, D_MODEL, IN_COLS), D_MODEL ** -0.5),
        "conv_w": nrm(ks[14], (DEPTH, CONV_WIDTH, CONV_CH), CONV_WIDTH ** -0.5),
        "a_log": jnp.log(jax.random.uniform(ks[15], (DEPTH, GDN_HEADS), f32, 1.0, 16.0)),
        "dt_bias": dt + jnp.log(-jnp.expm1(-dt)),
        "gdn_norm": gain(ks[16], (DEPTH, GDN_DV)),
        "w_attn_up": nrm(ks[17], (DEPTH, ATTN_WIDTH, D_MODEL), ATTN_WIDTH ** -0.5),
        "w_gdn_up": nrm(ks[18], (DEPTH, GDN_VW, D_MODEL), GDN_VW ** -0.5),
        "w_out": nrm(ks[19], (DEPTH, D_MODEL, D_MODEL), D_MODEL ** -0.5),
        "norm_ffn": gain(ks[20], (DEPTH, D_MODEL)),
        "w_gate_up": nrm(ks[21], (DEPTH, D_MODEL, 2 * D_FF), D_MODEL ** -0.5),
        "w_down": nrm(ks[22], (DEPTH, D_FF, D_MODEL), D_FF ** -0.5),
        "norm_ple": gain(ks[23], (DEPTH, D_MODEL)),
        "w_ple_gate": nrm(ks[24], (DEPTH, D_MODEL, D_MODEL), D_MODEL ** -0.5),
        "w_ple": nrm(ks[25], (DEPTH, PLE_DIM, D_MODEL), PLE_DIM ** -0.5),
        "norm_final": gain(ks[26], (D_MODEL,)),
    }


def reference(x_prompt, x_sample, cache_k, cache_v, cache_idx_k, state_conv, state_ssm, page_table,
              p_prompt, p_sample, rel_bias, norm_mix, w_in, conv_w, a_log, dt_bias, gdn_norm,
              w_attn_up, w_gdn_up, w_out, norm_ffn, w_gate_up, w_down, norm_ple, w_ple_gate, w_ple,
              norm_final):
    xp, xs = x_prompt, x_sample
    B = x_prompt.shape[0]
    conv0 = jnp.zeros((B, CONV_WIDTH - 1, CONV_CH), x_prompt.dtype)
    ssm0 = jnp.zeros((B, GDN_HEADS, GDN_DK, GDN_DV), state_ssm.dtype)
    kp_l, vp_l, kip_l, cp_l, sp_l = [], [], [], [], []
    ks_l, vs_l, kis_l, cs_l, ss_l = [], [], [], [], []
    prompt_attn = functools.partial(prompt_attention, rel_bias=rel_bias)
    for i in range(DEPTH):
        lw = (norm_mix[i], w_in[i], conv_w[i], a_log[i], dt_bias[i], gdn_norm[i], w_attn_up[i], w_gdn_up[i],
              w_out[i], norm_ffn[i], w_gate_up[i], w_down[i], norm_ple[i], w_ple_gate[i], w_ple[i])
        sample_attn = functools.partial(sample_attention, cache_k=cache_k[i], cache_v=cache_v[i],
                                        cache_idx_k=cache_idx_k[i], page_table=page_table, rel_bias=rel_bias)
        xp, kp, vp, kip, cp, sp = decoder_layer(xp, p_prompt[i], prompt_attn, conv0, ssm0, *lw)
        xs, ksm, vsm, kism, csm, ssm = decoder_layer(xs, p_sample[i], sample_attn, state_conv[i], state_ssm[i], *lw)
        kp_l.append(kp); vp_l.append(vp); kip_l.append(kip); cp_l.append(cp); sp_l.append(sp)
        ks_l.append(ksm); vs_l.append(vsm); kis_l.append(kism); cs_l.append(csm); ss_l.append(ssm)
    y_prompt = rmsnorm(xp, norm_final)
    y_sample = rmsnorm(xs, norm_final)
    return (y_prompt, y_sample,
            jnp.stack(kp_l), jnp.stack(vp_l), jnp.stack(kip_l), jnp.stack(cp_l), jnp.stack(sp_l),
            jnp.stack(ks_l), jnp.stack(vs_l), jnp.stack(kis_l), jnp.stack(cs_l), jnp.stack(ss_l))
```

```python
import functools
import math

import numpy as np
import jax
import jax.numpy as jnp
from jax import lax
from jax.experimental import pallas as pl
from jax.experimental.pallas import tpu as pltpu

F32 = jnp.float32
BF16 = jnp.bfloat16

N_HEADS = 16
N_KV_HEADS = 4
HEAD_DIM = 128
GROUPS = N_HEADS // N_KV_HEADS
N_IDX_HEADS = 32
IDX_DIM = 128
TOP_K_MAX = 256
PAGE_SIZE = 128
N_BUCKETS = 32
MAX_DISTANCE = 128
GDN_HEADS = 16
GDN_DK = 128
GDN_DV = 128
CONV_WIDTH = 4
GDN_CHUNK = 64
EPS = 1e-6
NEG = -1e30

LANES = 128
SUBLANES = 8
VMEM_LIMIT = 56 * 1024 * 1024


def _cparams(*sem):
    return pltpu.CompilerParams(dimension_semantics=sem, vmem_limit_bytes=VMEM_LIMIT)


def _rms_rows(x_ref, nw_ref, h_ref, rows):
    tm = x_ref.shape[0]
    rows = math.gcd(rows, tm)
    nw = nw_ref[...]

    def body(r, _):
        sl = pl.ds(pl.multiple_of(r * rows, rows), rows)
        x = x_ref[sl, :]
        ms = jnp.mean(x * x, axis=-1, keepdims=True)
        h_ref[sl, :] = (x * lax.rsqrt(ms + EPS) * nw).astype(h_ref.dtype)
        return 0

    lax.fori_loop(0, tm // rows, body, 0)


PROJ_TN = 512


def _proj_body(x_ref, nw_ref, w_ref, q_ref, k_ref, v_ref, kvb_ref, qi_ref, qkv_ref, gates_ref, misc_ref, h_ref,
               *, routes):
    j = pl.program_id(1)

    @pl.when(j == 0)
    def _():
        _rms_rows(x_ref, nw_ref, h_ref, 64)

    acc = jnp.dot(h_ref[...], w_ref[...], preferred_element_type=F32)
    hpt = PROJ_TN // HEAD_DIM

    def on(name):
        lo, n = routes[name]
        return pl.when((j >= lo) & (j < lo + n))

    @on("q")
    def _():
        for hh in range(hpt):
            q_ref[hh] = acc[:, hh * HEAD_DIM:(hh + 1) * HEAD_DIM].astype(q_ref.dtype)

    @on("k")
    def _():
        k_ref[...] = acc
        kvb_ref[...] = acc.astype(kvb_ref.dtype)

    @on("v")
    def _():
        v_ref[...] = acc
        kvb_ref[...] = acc.astype(kvb_ref.dtype)

    @on("qi")
    def _():
        for hh in range(hpt):
            qi_ref[hh] = acc[:, hh * IDX_DIM:(hh + 1) * IDX_DIM].astype(qi_ref.dtype)

    @on("qkv")
    def _():
        qkv_ref[...] = acc

    @on("gates")
    def _():
        gates_ref[...] = acc

    @on("misc")
    def _():
        misc_ref[...] = acc


def _proj(x, norm_w, w_packed, tm):
    M, D = x.shape
    tn = PROJ_TN
    widths = dict(q=2048, k=512, v=512, qi=4096, qkv=6144, gates=10240, misc=512)
    routes, lo = {}, 0
    for name, wd in widths.items():
        routes[name] = (lo, wd // tn)
        lo += wd // tn
    nj = lo
    assert w_packed.shape == (D, nj * tn)

    def col(name):
        a, n = routes[name]
        return lambda i, j: (i, jnp.clip(j - a, 0, n - 1))

    def heads(name):
        a, n = routes[name]
        return lambda i, j: (jnp.clip(j - a, 0, n - 1), i, 0)

    def kvb_map(i, j):
        return (i, jnp.clip(j - routes["k"][0], 0, 1))

    hpt = tn // HEAD_DIM
    out_shape = (
        jax.ShapeDtypeStruct((N_HEADS, M, HEAD_DIM), BF16),
        jax.ShapeDtypeStruct((M, 512), F32),
        jax.ShapeDtypeStruct((M, 512), F32),
        jax.ShapeDtypeStruct((M, 1024), BF16),
        jax.ShapeDtypeStruct((N_IDX_HEADS, M, IDX_DIM), BF16),
        jax.ShapeDtypeStruct((M, 6144), F32),
        jax.ShapeDtypeStruct((M, 10240), F32),
        jax.ShapeDtypeStruct((M, 512), F32),
    )
    out_specs = (
        pl.BlockSpec((hpt, tm, HEAD_DIM), heads("q")),
        pl.BlockSpec((tm, tn), col("k")),
        pl.BlockSpec((tm, tn), col("v")),
        pl.BlockSpec((tm, tn), kvb_map),
        pl.BlockSpec((hpt, tm, IDX_DIM), heads("qi")),
        pl.BlockSpec((tm, tn), col("qkv")),
        pl.BlockSpec((tm, tn), col("gates")),
        pl.BlockSpec((tm, tn), col("misc")),
    )
    return pl.pallas_call(
        functools.partial(_proj_body, routes=routes),
        grid=(M // tm, nj),
        in_specs=[
            pl.BlockSpec((tm, D), lambda i, j: (i, 0), pipeline_mode=pl.Buffered(1)),
            pl.BlockSpec((1, D), lambda i, j: (0, 0)),
            pl.BlockSpec((D, tn), lambda i, j: (0, j)),
        ],
        out_specs=out_specs,
        out_shape=out_shape,
        scratch_shapes=[pltpu.VMEM((tm, D), BF16)],
        compiler_params=_cparams("parallel", "arbitrary"),
        name="in_proj",
    )(x, norm_w.reshape(1, D), w_packed)


def _merge_body(attn_ref, o_ref, ga_ref, gb_ref, wa_ref, wg_ref, out_ref):
    a = jnp.dot(attn_ref[...], wa_ref[...], preferred_element_type=F32)
    b = jnp.dot(o_ref[...], wg_ref[...], preferred_element_type=F32)
    out_ref[...] = (jax.nn.sigmoid(ga_ref[...]) * a + jax.nn.sigmoid(gb_ref[...]) * b).astype(out_ref.dtype)


def _merge(attn, o, gates, wa, wg, tm, tn=512):
    M, KA = attn.shape
    D = wa.shape[1]
    a_off = 2048 // tn
    b_off = (2048 + D) // tn
    return pl.pallas_call(
        _merge_body,
        grid=(M // tm, D // tn),
        in_specs=[
            pl.BlockSpec((tm, KA), lambda i, j: (i, 0)),
            pl.BlockSpec((tm, KA), lambda i, j: (i, 0)),
            pl.BlockSpec((tm, tn), lambda i, j: (i, a_off + j)),
            pl.BlockSpec((tm, tn), lambda i, j: (i, b_off + j)),
            pl.BlockSpec((KA, tn), lambda i, j: (0, j)),
            pl.BlockSpec((KA, tn), lambda i, j: (0, j)),
        ],
        out_specs=pl.BlockSpec((tm, tn), lambda i, j: (i, j)),
        out_shape=jax.ShapeDtypeStruct((M, D), BF16),
        compiler_params=_cparams("parallel", "arbitrary"),
        name="merge",
    )(attn, o, gates, gates, wa, wg)


def _resid_mm_body(x_ref, a_ref, w_ref, out_ref):
    out_ref[...] = x_ref[...] + jnp.dot(a_ref[...], w_ref[...], preferred_element_type=F32)


def _resid_mm(x, a, w, tm, tn):
    M, N = x.shape
    Kd = a.shape[1]
    return pl.pallas_call(
        _resid_mm_body,
        grid=(M // tm, N // tn),
        in_specs=[
            pl.BlockSpec((tm, tn), lambda i, j: (i, j)),
            pl.BlockSpec((tm, Kd), lambda i, j: (i, 0), pipeline_mode=pl.Buffered(1)),
            pl.BlockSpec((Kd, tn), lambda i, j: (0, j)),
        ],
        out_specs=pl.BlockSpec((tm, tn), lambda i, j: (i, j)),
        out_shape=jax.ShapeDtypeStruct((M, N), F32),
        compiler_params=_cparams("parallel", "arbitrary"),
        name="resid_mm",
    )(x, a, w)


def _ffn_up_body(x_ref, nw_ref, wg_ref, wu_ref, out_ref, h_ref):
    @pl.when(pl.program_id(1) == 0)
    def _():
        _rms_rows(x_ref, nw_ref, h_ref, 64)

    h = h_ref[...]
    g = jnp.dot(h, wg_ref[...], preferred_element_type=F32)
    u = jnp.dot(h, wu_ref[...], preferred_element_type=F32)
    out_ref[...] = (jax.nn.silu(g) * u).astype(out_ref.dtype)


def _ffn_up(x, norm_w, w_gu, tm, tn=512):
    M, D = x.shape
    ffp = w_gu.shape[1] // 2
    nj = ffp // tn
    return pl.pallas_call(
        _ffn_up_body,
        grid=(M // tm, nj),
        in_specs=[
            pl.BlockSpec((tm, D), lambda i, j: (i, 0), pipeline_mode=pl.Buffered(1)),
            pl.BlockSpec((1, D), lambda i, j: (0, 0)),
            pl.BlockSpec((D, tn), lambda i, j: (0, j)),
            pl.BlockSpec((D, tn), lambda i, j: (0, nj + j)),
        ],
        out_specs=pl.BlockSpec((tm, tn), lambda i, j: (i, j)),
        out_shape=jax.ShapeDtypeStruct((M, ffp), BF16),
        scratch_shapes=[pltpu.VMEM((tm, D), BF16)],
        compiler_params=_cparams("parallel", "arbitrary"),
        name="ffn_up",
    )(x, norm_w.reshape(1, D), w_gu, w_gu)


def _ple_body(x_ref, nw_ref, wg_ref, p_ref, wp_ref, nf_ref, y_ref, h_ref, x3_ref, *, tn):
    j = pl.program_id(1)

    @pl.when(j == 0)
    def _():
        _rms_rows(x_ref, nw_ref, h_ref, 64)

    g = jnp.dot(h_ref[...], wg_ref[...], preferred_element_type=F32)
    e = jnp.dot(p_ref[...], wp_ref[...], preferred_element_type=F32)
    cols = pl.ds(pl.multiple_of(j * tn, tn), tn)
    x3_ref[:, cols] = x_ref[:, cols] + jax.nn.sigmoid(g) * e

    @pl.when(j == pl.num_programs(1) - 1)
    def _():
        _rms_rows(x3_ref, nf_ref, y_ref, 64)


def _ple_final(x, norm_w, w_gate, p, w_ple, norm_final, tm, tn=512):
    M, D = x.shape
    P = p.shape[1]
    return pl.pallas_call(
        functools.partial(_ple_body, tn=tn),
        grid=(M // tm, D // tn),
        in_specs=[
            pl.BlockSpec((tm, D), lambda i, j: (i, 0), pipeline_mode=pl.Buffered(1)),
            pl.BlockSpec((1, D), lambda i, j: (0, 0)),
            pl.BlockSpec((D, tn), lambda i, j: (0, j)),
            pl.BlockSpec((tm, P), lambda i, j: (i, 0)),
            pl.BlockSpec((P, tn), lambda i, j: (0, j)),
            pl.BlockSpec((1, D), lambda i, j: (0, 0)),
        ],
        out_specs=pl.BlockSpec((tm, D), lambda i, j: (i, 0)),
        out_shape=jax.ShapeDtypeStruct((M, D), F32),
        scratch_shapes=[pltpu.VMEM((tm, D), BF16), pltpu.VMEM((tm, D), F32)],
        compiler_params=_cparams("parallel", "arbitrary"),
        name="ple_final",
    )(x, norm_w.reshape(1, D), w_gate, p, w_ple, norm_final.reshape(1, D))


def _bucket_thresholds():
    d = np.arange(0, 4 * MAX_DISTANCE)
    max_exact = N_BUCKETS // 2
    large = max_exact + (np.log(np.maximum(d, 1) / max_exact) / math.log(MAX_DISTANCE / max_exact)
                         * (N_BUCKETS - max_exact)).astype(np.int32)
    b = np.where(d < max_exact, d, np.minimum(large, N_BUCKETS - 1))
    return [int(np.argmax(b >= k)) for k in range(N_BUCKETS)]


_BUCKET_THR = _bucket_thresholds()


def _bias_of_dist(rb_ref, h, d):
    v = jnp.full(d.shape, rb_ref[0, h], F32)
    for b in range(1, N_BUCKETS):
        v = jnp.where(d >= _BUCKET_THR[b], rb_ref[b, h], v)
    return v


def _bias_tables_body(rb_ref, tprev_ref, tdiag_ref, tsamp_ref, *, qb, past):
    h = pl.program_id(0)
    r = lax.broadcasted_iota(jnp.int32, (qb, qb), 0)
    c = lax.broadcasted_iota(jnp.int32, (qb, qb), 1)
    tprev_ref[0] = _bias_of_dist(rb_ref, h, r + qb - c)
    tdiag_ref[0] = _bias_of_dist(rb_ref, h, r - c)
    ls = tsamp_ref.shape[2]
    t = lax.broadcasted_iota(jnp.int32, (SUBLANES, ls), 0) % 4
    lane = lax.broadcasted_iota(jnp.int32, (SUBLANES, ls), 1)
    d = jnp.where(lane < past, past + t - lane, t - (lane - past) % 4)
    tsamp_ref[0] = _bias_of_dist(rb_ref, h, d)


def _bias_tables(rel_bias, qb, past, ls):
    return pl.pallas_call(
        functools.partial(_bias_tables_body, qb=qb, past=past),
        grid=(N_HEADS,),
        in_specs=[pl.BlockSpec(memory_space=pltpu.SMEM)],
        out_specs=(
            pl.BlockSpec((1, qb, qb), lambda h: (h, 0, 0)),
            pl.BlockSpec((1, qb, qb), lambda h: (h, 0, 0)),
            pl.BlockSpec((1, SUBLANES, ls), lambda h: (h, 0, 0)),
        ),
        out_shape=(
            jax.ShapeDtypeStruct((N_HEADS, qb, qb), F32),
            jax.ShapeDtypeStruct((N_HEADS, qb, qb), F32),
            jax.ShapeDtypeStruct((N_HEADS, SUBLANES, ls), F32),
        ),
        compiler_params=_cparams("arbitrary"),
        name="bias_tables",
    )(rel_bias)


INT_MIN = -2 ** 31


def _sort_key(score, valid):
    bits = lax.bitcast_convert_type(score, jnp.int32)
    key = jnp.where(bits < 0, bits ^ jnp.int32(0x7FFFFFFF), bits)
    return jnp.where(valid, key, jnp.int32(INT_MIN))


def _kth_largest(count_ge, rows, k):
    def step(b, t):
        cand = t + lax.shift_left(jnp.int32(1), jnp.asarray(31 - b, jnp.int32))
        n = count_ge(cand)
        return jnp.where(n >= k, cand, t)

    return lax.fori_loop(0, 32, step, jnp.full((rows, LANES), INT_MIN, jnp.int32))


PQB = 256
PKC = 256


def _pattn_body(rb_ref, qi_ref, wi_ref, ki_ref, q_ref, k_ref, v_ref, tprev_ref, tdiag_ref, out_ref,
                key_ref, wb_ref, m_ref, l_ref, acc_ref, *, idx_scale, n_sel):
    i = pl.program_id(0)
    qb, kc = PQB, PKC
    hg = 8

    wi = wi_ref[...] * idx_scale
    for h in range(N_IDX_HEADS):
        wb_ref[h] = jnp.broadcast_to(wi[:, h:h + 1], (qb, LANES))

    def chunk(c):
        return pl.ds(pl.multiple_of(c * kc, kc), kc)

    row = lax.broadcasted_iota(jnp.int32, (qb, kc), 0)
    col = lax.broadcasted_iota(jnp.int32, (qb, kc), 1)

    def score_chunk(c, _):
        kic = ki_ref[chunk(c), :]
        acc = [jnp.zeros((qb, LANES), F32) for _ in range(kc // LANES)]
        for g in range(N_IDX_HEADS // hg):
            qg = qi_ref[g * hg:(g + 1) * hg].reshape(hg * qb, IDX_DIM)
            s = lax.dot_general(qg, kic, (((1,), (1,)), ((), ())), preferred_element_type=F32)
            for hh in range(hg):
                w = wb_ref[g * hg + hh]
                for half in range(kc // LANES):
                    sh = s[hh * qb:(hh + 1) * qb, half * LANES:(half + 1) * LANES]
                    acc[half] = acc[half] + jnp.maximum(sh, 0.0) * w
        score = jnp.concatenate(acc, axis=1)
        valid = (c * kc + col) <= (i * qb + row)
        key_ref[:, chunk(c)] = _sort_key(score, valid)
        return 0

    lax.fori_loop(0, i + 1, score_chunk, 0)

    def count_ge(cand):
        def body(c, cnt):
            kk = key_ref[:, chunk(c)]
            for half in range(kc // LANES):
                cnt = cnt + jnp.where(kk[:, half * LANES:(half + 1) * LANES] >= cand, 1.0, 0.0)
            return cnt

        cnt = lax.fori_loop(0, i + 1, body, jnp.zeros((qb, LANES), F32))
        return jnp.sum(cnt, axis=1, keepdims=True)

    thr = _kth_largest(count_ge, qb, float(n_sel))

    def mask_chunk(c, _):
        kk = key_ref[:, chunk(c)]
        t2 = jnp.concatenate([thr] * (kc // LANES), axis=1)
        sel = (kk >= t2) & (kk > jnp.int32(INT_MIN))
        key_ref[:, chunk(c)] = lax.bitcast_convert_type(jnp.where(sel, 0.0, NEG).astype(F32), jnp.int32)
        return 0

    lax.fori_loop(0, i + 1, mask_chunk, 0)

    scale = HEAD_DIM ** -0.5

    def kv_head(kh, _):
        qs = q_ref[pl.ds(kh * GROUPS, GROUPS)].reshape(GROUPS * qb, HEAD_DIM)
        lanes = pl.ds(pl.multiple_of(kh * HEAD_DIM, HEAD_DIM), HEAD_DIM)
        m_ref[...] = jnp.full(m_ref.shape, NEG, F32)
        l_ref[...] = jnp.zeros(l_ref.shape, F32)
        acc_ref[...] = jnp.zeros(acc_ref.shape, F32)

        def attend(c, bias_of_group):
            kt = k_ref[chunk(c), lanes]
            vt = v_ref[chunk(c), lanes]
            s = lax.dot_general(qs, kt, (((1,), (1,)), ((), ())), preferred_element_type=F32)
            mb = lax.bitcast_convert_type(key_ref[:, chunk(c)], F32)
            ps = []
            for g in range(GROUPS):
                sg = s[g * qb:(g + 1) * qb] * scale + bias_of_group(g) + mb
                m_old = m_ref[g]
                m_new = jnp.maximum(m_old, jnp.max(sg, axis=1, keepdims=True))
                alpha = jnp.exp(m_old - m_new)
                p = jnp.exp(sg - m_new)
                l_ref[g] = alpha * l_ref[g] + jnp.sum(p, axis=1, keepdims=True)
                acc_ref[g] = alpha * acc_ref[g]
                m_ref[g] = m_new
                ps.append(p.astype(BF16))
            pv = jnp.dot(jnp.concatenate(ps, axis=0), vt, preferred_element_type=F32)
            for g in range(GROUPS):
                acc_ref[g] = acc_ref[g] + pv[g * qb:(g + 1) * qb]

        def far(c, _):
            attend(c, lambda g: rb_ref[N_BUCKETS - 1, kh * GROUPS + g])
            return 0

        lax.fori_loop(0, jnp.maximum(i - 1, 0), far, 0)

        @pl.when(i >= 1)
        def _():
            attend(i - 1, lambda g: tprev_ref[kh * GROUPS + g])

        attend(i, lambda g: tdiag_ref[kh * GROUPS + g])

        for g in range(GROUPS):
            o = acc_ref[g] / l_ref[g]
            out_ref[:, pl.ds(pl.multiple_of((kh * GROUPS + g) * HEAD_DIM, HEAD_DIM), HEAD_DIM)] = o.astype(out_ref.dtype)
        return 0

    lax.fori_loop(0, N_KV_HEADS, kv_head, 0)


def _prompt_attention(rel_bias, qi_hm, wi, ki_bf, q_hm, kv_bf, tprev, tdiag, seq, n_sel):
    qb = PQB
    resident = dict(pipeline_mode=pl.Buffered(1))
    return pl.pallas_call(
        functools.partial(_pattn_body, idx_scale=(N_IDX_HEADS * IDX_DIM) ** -0.5, n_sel=n_sel),
        grid=(seq // qb,),
        in_specs=[
            pl.BlockSpec(memory_space=pltpu.SMEM),
            pl.BlockSpec((N_IDX_HEADS, qb, IDX_DIM), lambda i: (0, i, 0)),
            pl.BlockSpec((qb, N_IDX_HEADS), lambda i: (i, 0)),
            pl.BlockSpec((seq, IDX_DIM), lambda i: (0, 0), **resident),
            pl.BlockSpec((N_HEADS, qb, HEAD_DIM), lambda i: (0, i, 0)),
            pl.BlockSpec((seq, N_KV_HEADS * HEAD_DIM), lambda i: (0, 0), **resident),
            pl.BlockSpec((seq, N_KV_HEADS * HEAD_DIM), lambda i: (0, 1), **resident),
            pl.BlockSpec((N_HEADS, qb, qb), lambda i: (0, 0, 0), **resident),
            pl.BlockSpec((N_HEADS, qb, qb), lambda i: (0, 0, 0), **resident),
        ],
        out_specs=pl.BlockSpec((qb, N_HEADS * HEAD_DIM), lambda i: (i, 0)),
        out_shape=jax.ShapeDtypeStruct((seq, N_HEADS * HEAD_DIM), BF16),
        scratch_shapes=[
            pltpu.VMEM((qb, seq), jnp.int32),
            pltpu.VMEM((N_IDX_HEADS, qb, LANES), F32),
            pltpu.VMEM((GROUPS, qb, 1), F32),
            pltpu.VMEM((GROUPS, qb, 1), F32),
            pltpu.VMEM((GROUPS, qb, HEAD_DIM), F32),
        ],
        compiler_params=_cparams("arbitrary"),
        name="prompt_attention",
    )(rel_bias, qi_hm, wi, ki_bf, q_hm, kv_bf, kv_bf, tprev, tdiag)


GA_LANE = 32
GB_LANE = 48
HALO = SUBLANES
HI = lax.Precision.HIGHEST


def _mm(a, b):
    return jnp.dot(a, b, precision=HI, preferred_element_type=F32)


def _mm_nt(a, b):
    return lax.dot_general(a, b, (((1,), (1,)), ((), ())), precision=HI, preferred_element_type=F32)


def _mm_tn(a, b):
    return lax.dot_general(a, b, (((0,), (0,)), ((), ())), precision=HI, preferred_element_type=F32)


def _gdn_body(qkv_ref, slab_ref, gz_ref, cw_ref, alog_ref, dtb_ref, gn_ref, halo0_ref, s0_ref,
              o_ref, s_ref, xp_ref, *, C, valid_rows):
    c = pl.program_id(1)
    kw, vw = GDN_HEADS * GDN_DK, GDN_HEADS * GDN_DV

    @pl.when(c == 0)
    def _():
        xp_ref[0:HALO] = halo0_ref[0]
        s_ref[...] = s0_ref[...]

    xp_ref[HALO:HALO + C] = qkv_ref[...]

    ri = lax.broadcasted_iota(jnp.int32, (C, C), 0)
    ci = lax.broadcasted_iota(jnp.int32, (C, C), 1)
    causal = ri >= ci
    strict = ri > ci
    eye = jnp.where(ri == ci, 1.0, 0.0).astype(F32)
    ltri = jnp.where(causal, 1.0, 0.0).astype(F32)

    slab = slab_ref[...]
    live = lax.broadcasted_iota(jnp.int32, slab.shape, 0) < valid_rows
    g_all = jnp.where(live, -jnp.exp(alog_ref[...]) * jax.nn.softplus(slab + dtb_ref[...]), 0.0)
    beta_all = jnp.where(live, jax.nn.sigmoid(slab), 0.0)
    gc_all = _mm(ltri, g_all)
    pad = jnp.zeros((LANES - C, LANES), F32)
    gc_t = jnp.concatenate([gc_all, pad], axis=0).T

    def conv(cols):
        y = cw_ref[0:1, cols] * xp_ref[HALO - 3:HALO - 3 + C, cols]
        for j in range(1, CONV_WIDTH):
            y = y + cw_ref[j:j + 1, cols] * xp_ref[HALO - 3 + j:HALO - 3 + j + C, cols]
        return jax.nn.silu(y)

    def l2n(x):
        return x * lax.rsqrt(jnp.sum(x * x, axis=-1, keepdims=True) + 1e-6)

    n_sq = max(1, (C - 1).bit_length() - 1)

    for h in range(GDN_HEADS):
        q = l2n(conv(slice(h * GDN_DK, (h + 1) * GDN_DK))) * GDN_DK ** -0.5
        k = l2n(conv(slice(kw + h * GDN_DK, kw + (h + 1) * GDN_DK)))
        v = conv(slice(2 * kw + h * GDN_DV, 2 * kw + (h + 1) * GDN_DV))
        beta = beta_all[:, GB_LANE + h:GB_LANE + h + 1]
        gc = gc_all[:, GA_LANE + h:GA_LANE + h + 1]
        gc_row = gc_t[GA_LANE + h:GA_LANE + h + 1, 0:C]
        gc_last = gc_all[C - 1:C, GA_LANE + h:GA_LANE + h + 1]
        decay = jnp.exp(jnp.where(causal, gc - gc_row, NEG))
        kb = k * beta
        a_ut = jnp.where(strict, _mm_nt(kb, k) * decay, 0.0)
        p = -a_ut
        t_inv = eye + p
        for _ in range(n_sq):
            p = _mm(p, p)
            t_inv = t_inv + _mm(t_inv, p)
        egc = jnp.exp(gc)
        u = _mm(t_inv, v * beta)
        w = _mm(t_inv, kb * egc)
        a_in = jnp.where(causal, _mm_nt(q, k) * decay, 0.0)
        qd = q * egc
        kt = k * jnp.exp(gc_last - gc)
        s_old = s_ref[0, h]
        v_new = u - _mm(w, s_old)
        o = _mm(qd, s_old) + _mm(a_in, v_new)
        s_ref[0, h] = s_old * jnp.exp(gc_last) + _mm_tn(kt, v_new)
        on = o * lax.rsqrt(jnp.mean(o * o, axis=-1, keepdims=True) + EPS) * gn_ref[...]
        gz = gz_ref[:, h * GDN_DV:(h + 1) * GDN_DV]
        o_ref[:, h * GDN_DV:(h + 1) * GDN_DV] = (on * jax.nn.silu(gz)).astype(o_ref.dtype)

    xp_ref[0:HALO] = xp_ref[C:C + HALO]


def _gdn(qkv, misc, gates, conv_w, alog_vec, dtb_vec, gdn_norm, halo0, s0, *, n_seq, n_chunks, C, valid_rows,
         out_dtype):
    ch = qkv.shape[1]
    vw = GDN_HEADS * GDN_DV
    rows = n_seq * n_chunks * C

    def rmap(s, c):
        return (s * n_chunks + c, 0)

    return pl.pallas_call(
        functools.partial(_gdn_body, C=C, valid_rows=valid_rows),
        grid=(n_seq, n_chunks),
        in_specs=[
            pl.BlockSpec((C, ch), rmap),
            pl.BlockSpec((C, LANES), lambda s, c: (s * n_chunks + c, 1)),
            pl.BlockSpec((C, vw), rmap),
            pl.BlockSpec((CONV_WIDTH, ch), lambda s, c: (0, 0)),
            pl.BlockSpec((1, LANES), lambda s, c: (0, 0)),
            pl.BlockSpec((1, LANES), lambda s, c: (0, 0)),
            pl.BlockSpec((1, GDN_DV), lambda s, c: (0, 0)),
            pl.BlockSpec((1, HALO, ch), lambda s, c: (s, 0, 0)),
            pl.BlockSpec((1, GDN_HEADS, GDN_DK, GDN_DV), lambda s, c: (s, 0, 0, 0)),
        ],
        out_specs=(
            pl.BlockSpec((C, vw), rmap),
            pl.BlockSpec((1, GDN_HEADS, GDN_DK, GDN_DV), lambda s, c: (s, 0, 0, 0)),
        ),
        out_shape=(
            jax.ShapeDtypeStruct((rows, vw), out_dtype),
            jax.ShapeDtypeStruct((n_seq, GDN_HEADS, GDN_DK, GDN_DV), F32),
        ),
        scratch_shapes=[pltpu.VMEM((C + HALO, ch), F32)],
        compiler_params=_cparams("arbitrary", "arbitrary"),
        name=f"gdn_c{C}",
    )(qkv, misc, gates, conv_w, alog_vec, dtb_vec, gdn_norm.reshape(1, GDN_DV), halo0, s0)


SB = 2
ST = SUBLANES // SB


def _sattn_body(pt_ref, qi_ref, wi_ref, q_ref, kin_ref, kn_ref, vn_ref, tsamp_ref, cidx_hbm, ck_hbm, cv_hbm, out_ref,
                kibuf, kbuf, vbuf, sem, *, past, n_pages, n_sel, idx_scale):
    p = pl.program_id(0)
    n_steps = pl.num_programs(0)
    ls = kibuf.shape[1]
    rows = SB * ST

    def copies(step, bl):
        b = step * SB + bl
        out = []
        for j in range(n_pages):
            pg = pt_ref[b, j]
            dst = pl.ds(j * PAGE_SIZE, PAGE_SIZE)
            out.append(pltpu.make_async_copy(cidx_hbm.at[pg], kibuf.at[bl, dst], sem.at[bl]))
            out.append(pltpu.make_async_copy(ck_hbm.at[pg], kbuf.at[bl, dst], sem.at[bl]))
            out.append(pltpu.make_async_copy(cv_hbm.at[pg], vbuf.at[bl, dst], sem.at[bl]))
        return out

    @pl.when(p == 0)
    def _():
        for bl in range(SB):
            for cp in copies(0, bl):
                cp.start()
            kibuf[bl, past:ls] = jnp.zeros((ls - past, IDX_DIM), F32)
            kbuf[bl, past:ls] = jnp.zeros((ls - past, kbuf.shape[2]), F32)
            vbuf[bl, past:ls] = jnp.zeros((ls - past, vbuf.shape[2]), F32)

    bt = lax.broadcasted_iota(jnp.int32, (rows, ls), 0)
    lane = lax.broadcasted_iota(jnp.int32, (rows, ls), 1)
    r_new = lane - past
    valid = (lane < past) | ((r_new < rows) & (r_new // ST == bt // ST) & (r_new % ST <= bt % ST))

    for bl in range(SB):
        kibuf[bl, past:past + rows] = kin_ref[...]
        kbuf[bl, past:past + rows] = kn_ref[...]
        vbuf[bl, past:past + rows] = vn_ref[...]

    wi = wi_ref[...] * idx_scale
    qi = qi_ref[...].reshape(N_IDX_HEADS * rows, IDX_DIM).astype(BF16)
    key_chunks = [(s, min(512, ls - s)) for s in range(0, ls, 512)]

    def scores(bl):
        parts = []
        for s0, n in key_chunks:
            kic = kibuf[bl, s0:s0 + n].astype(BF16)
            s = lax.dot_general(qi, kic, (((1,), (1,)), ((), ())), preferred_element_type=F32)
            acc = jnp.zeros((rows, n), F32)
            for h in range(N_IDX_HEADS):
                acc = acc + jnp.maximum(s[h * rows:(h + 1) * rows], 0.0) * wi[:, h:h + 1]
            parts.append(acc)
        return jnp.concatenate(parts, axis=1)

    sc = []
    for bl in range(SB):
        for cp in copies(p, bl):
            cp.wait()
        sc.append(scores(bl))
    score = sc[0]
    for bl in range(1, SB):
        score = jnp.where(bt // ST == bl, sc[bl], score)
    key = _sort_key(score, valid)

    def count_ge(cand):
        cnt = jnp.zeros((rows, LANES), F32)
        for t in range(ls // LANES):
            cnt = cnt + jnp.where(key[:, t * LANES:(t + 1) * LANES] >= cand, 1.0, 0.0)
        return jnp.sum(cnt, axis=1, keepdims=True)

    thr = _kth_largest(count_ge, rows, float(n_sel))
    thr_full = jnp.concatenate([thr] * (ls // LANES), axis=1)
    mb = jnp.where((key >= thr_full) & valid, 0.0, NEG).astype(F32)

    scale = HEAD_DIM ** -0.5
    row_b = lax.broadcasted_iota(jnp.int32, (GROUPS * rows, HEAD_DIM), 0) % rows // ST
    outs = [None] * N_KV_HEADS
    for bl in range(SB):
        for kh in range(N_KV_HEADS):
            lanes = slice(kh * HEAD_DIM, (kh + 1) * HEAD_DIM)
            qs = q_ref[kh * GROUPS:(kh + 1) * GROUPS].reshape(GROUPS * rows, HEAD_DIM).astype(BF16)
            kt = kbuf[bl, :, lanes].astype(BF16)
            vt = vbuf[bl, :, lanes].astype(BF16)
            s = lax.dot_general(qs, kt, (((1,), (1,)), ((), ())), preferred_element_type=F32) * scale
            s = (s.reshape(GROUPS, rows, ls) + tsamp_ref[kh * GROUPS:(kh + 1) * GROUPS] + mb[None]).reshape(GROUPS * rows, ls)
            m = jnp.max(s, axis=1, keepdims=True)
            e = jnp.exp(s - m)
            o = jnp.dot(e.astype(BF16), vt, preferred_element_type=F32) / jnp.sum(e, axis=1, keepdims=True)
            outs[kh] = o if bl == 0 else jnp.where(row_b == bl, o, outs[kh])

        @pl.when(p + 1 < n_steps)
        def _():
            for cp in copies(p + 1, bl):
                cp.start()

    for kh in range(N_KV_HEADS):
        for g in range(GROUPS):
            hd = kh * GROUPS + g
            out_ref[:, hd * HEAD_DIM:(hd + 1) * HEAD_DIM] = outs[kh][g * rows:(g + 1) * rows]


def _sample_attention(page_table, qi_s, wi_s, q_s, ki_new, k_new, v_new, tsamp, cache_idx, cache_k, cache_v, n_sel):
    R = wi_s.shape[0]
    n_b, n_pages = page_table.shape
    past = n_pages * PAGE_SIZE
    ls = past + PAGE_SIZE
    rows = SB * ST
    assert R == n_b * ST and n_b % SB == 0
    kvw = N_KV_HEADS * HEAD_DIM
    grid_spec = pltpu.PrefetchScalarGridSpec(
        num_scalar_prefetch=1,
        grid=(n_b // SB,),
        in_specs=[
            pl.BlockSpec((N_IDX_HEADS, rows, IDX_DIM), lambda p, pt: (0, p, 0)),
            pl.BlockSpec((rows, N_IDX_HEADS), lambda p, pt: (p, 0)),
            pl.BlockSpec((N_HEADS, rows, HEAD_DIM), lambda p, pt: (0, p, 0)),
            pl.BlockSpec((rows, IDX_DIM), lambda p, pt: (p, 0)),
            pl.BlockSpec((rows, kvw), lambda p, pt: (p, 0)),
            pl.BlockSpec((rows, kvw), lambda p, pt: (p, 0)),
            pl.BlockSpec((N_HEADS, SUBLANES, ls), lambda p, pt: (0, 0, 0)),
            pl.BlockSpec(memory_space=pl.ANY),
            pl.BlockSpec(memory_space=pl.ANY),
            pl.BlockSpec(memory_space=pl.ANY),
        ],
        out_specs=pl.BlockSpec((rows, N_HEADS * HEAD_DIM), lambda p, pt: (p, 0)),
        scratch_shapes=[
            pltpu.VMEM((SB, ls, IDX_DIM), F32),
            pltpu.VMEM((SB, ls, kvw), F32),
            pltpu.VMEM((SB, ls, kvw), F32),
            pltpu.SemaphoreType.DMA((SB,)),
        ],
    )
    return pl.pallas_call(
        functools.partial(_sattn_body, past=past, n_pages=n_pages, n_sel=n_sel,
                          idx_scale=(N_IDX_HEADS * IDX_DIM) ** -0.5),
        grid_spec=grid_spec,
        out_shape=jax.ShapeDtypeStruct((R, N_HEADS * HEAD_DIM), F32),
        compiler_params=_cparams("arbitrary"),
        name="sample_attention",
    )(page_table, qi_s, wi_s, q_s, ki_new, k_new, v_new, tsamp, cache_idx, cache_k, cache_v)


ROW_TILE_CAP = 512
FF_TILE = 512


def _row_tile(m, cap=ROW_TILE_CAP):
    return max(t for t in range(16, cap + 1, 16) if m % t == 0)


def _pad_tokens(a, db, dt):
    a = a.reshape(db, dt, a.shape[-1])
    return jnp.pad(a, ((0, 0), (0, SUBLANES - dt), (0, 0))).reshape(db * SUBLANES, a.shape[-1])


def kernel(x_prompt, x_sample, cache_k, cache_v, cache_idx_k, state_conv, state_ssm, page_table, p_prompt, p_sample,
           rel_bias, norm_mix, w_in, conv_w, a_log, dt_bias, gdn_norm, w_attn_up, w_gdn_up, w_out, norm_ffn, w_gate_up,
           w_down, norm_ple, w_ple_gate, w_ple, norm_final):
    assert x_prompt.shape[0] == 1 and w_in.shape[0] == 1, "one prompt sequence, one layer"
    _, T, D = x_prompt.shape
    DB, DT, _ = x_sample.shape
    assert DT == ST and T % PQB == 0 and T % GDN_CHUNK == 0 and DT >= CONV_WIDTH - 1
    n_pool = cache_k.shape[1]
    past = page_table.shape[1] * PAGE_SIZE
    RS = DB * DT
    M = T + RS
    tm = _row_tile(M)
    kw, vw = GDN_HEADS * GDN_DK, GDN_HEADS * GDN_DV
    ch = 2 * kw + vw
    aw, kvw, iw = N_HEADS * HEAD_DIM, N_KV_HEADS * HEAD_DIM, N_IDX_HEADS * IDX_DIM

    w = w_in[0]
    o_ki = aw + 2 * kvw + iw
    o_wi = o_ki + IDX_DIM
    o_qkv = o_wi + N_IDX_HEADS
    o_ga = o_qkv + ch
    o_gz = o_ga + 2 * GDN_HEADS
    n_main = o_ki + ch + (vw + 2 * D)
    n_misc = IDX_DIM + N_IDX_HEADS + 2 * GDN_HEADS
    w_packed = jnp.concatenate([
        w[:, :o_ki], w[:, o_qkv:o_ga], w[:, o_gz:], w[:, o_ki:o_qkv], w[:, o_ga:o_gz],
        jnp.zeros((D, PROJ_TN - n_misc), w.dtype)], axis=1).astype(BF16)
    assert w_packed.shape[1] == n_main + PROJ_TN
    wa = w_attn_up[0].astype(BF16)
    wg = w_gdn_up[0].astype(BF16)
    wo = w_out[0].astype(BF16)
    dff = w_down.shape[1]
    ffp = -(-dff // FF_TILE) * FF_TILE
    zpad = jnp.zeros((D, ffp - dff), w_gate_up.dtype)
    w_gu = jnp.concatenate([w_gate_up[0][:, :dff], zpad, w_gate_up[0][:, dff:], zpad], axis=1).astype(BF16)
    w_dn = jnp.pad(w_down[0], ((0, ffp - dff), (0, 0))).astype(BF16)
    w_pg = w_ple_gate[0].astype(BF16)
    w_pe = w_ple[0].astype(BF16)
    alog_vec = jnp.zeros((1, LANES), F32).at[0, GA_LANE:GA_LANE + GDN_HEADS].set(a_log[0])
    dtb_vec = jnp.zeros((1, LANES), F32).at[0, GA_LANE:GA_LANE + GDN_HEADS].set(dt_bias[0])

    x = jnp.concatenate([x_prompt[0], x_sample.reshape(RS, D)], axis=0)
    p_all = jnp.concatenate([p_prompt[0, 0], p_sample[0].reshape(RS, -1)], axis=0).astype(BF16)

    q_hm, k_f, v_f, kv_bf, qi_hm, qkv, gates, misc = _proj(x, norm_mix[0], w_packed, tm)
    ki_f = misc[:, :IDX_DIM]
    wi_f = misc[:, IDX_DIM:IDX_DIM + N_IDX_HEADS]

    tprev, tdiag, tsamp = _bias_tables(rel_bias, PQB, past, past + PAGE_SIZE)
    attn_p = _prompt_attention(rel_bias, qi_hm, wi_f, ki_f.astype(BF16), q_hm, kv_bf, tprev, tdiag, T,
                               min(TOP_K_MAX, T // 4))
    attn_s = _sample_attention(
        page_table, qi_hm[:, T:].astype(F32), wi_f[T:], q_hm[:, T:].astype(F32), ki_f[T:], k_f[T:], v_f[T:], tsamp,
        cache_idx_k[0], cache_k[0].reshape(n_pool, PAGE_SIZE, kvw), cache_v[0].reshape(n_pool, PAGE_SIZE, kvw),
        min(TOP_K_MAX, (past + DT) // 4))
    attn = jnp.concatenate([attn_p, attn_s.astype(BF16)], axis=0)

    o_p, ssm_p = _gdn(qkv, misc, gates, conv_w[0], alog_vec, dtb_vec, gdn_norm[0],
                      jnp.zeros((1, HALO, ch), F32), jnp.zeros((1, GDN_HEADS, GDN_DK, GDN_DV), F32),
                      n_seq=1, n_chunks=T // GDN_CHUNK, C=GDN_CHUNK, valid_rows=GDN_CHUNK, out_dtype=BF16)
    halo_s = jnp.pad(state_conv[0], ((0, 0), (HALO - (CONV_WIDTH - 1), 0), (0, 0)))
    o_s, ssm_s = _gdn(_pad_tokens(qkv[T:], DB, DT), _pad_tokens(misc[T:], DB, DT), _pad_tokens(gates[T:, :vw], DB, DT),
                      conv_w[0], alog_vec, dtb_vec, gdn_norm[0], halo_s, state_ssm[0],
                      n_seq=DB, n_chunks=1, C=SUBLANES, valid_rows=DT, out_dtype=F32)
    o_s = o_s.reshape(DB, SUBLANES, vw)[:, :DT].reshape(RS, vw)
    o_all = jnp.concatenate([o_p, o_s.astype(BF16)], axis=0)

    merged = _merge(attn, o_all, gates, wa, wg, tm)
    x1 = _resid_mm(x, merged, wo, tm, 512)
    act = _ffn_up(x1, norm_ffn[0], w_gu, tm, FF_TILE)
    x2 = _resid_mm(x1, act, w_dn, tm, 256)
    y = _ple_final(x2, norm_ple[0], w_pg, p_all, w_pe, norm_final, tm)

    def heads(a, n):
        return a.reshape(a.shape[0], n, a.shape[1] // n)

    nc = CONV_WIDTH - 1
    return (
        y[:T].reshape(1, T, D),
        y[T:].reshape(DB, DT, D),
        heads(k_f[:T], N_KV_HEADS)[None, None],
        heads(v_f[:T], N_KV_HEADS)[None, None],
        ki_f[:T][None, None],
        qkv[T - nc:T][None, None],
        ssm_p.astype(state_ssm.dtype)[None],
        heads(k_f[T:], N_KV_HEADS).reshape(1, DB, DT, N_KV_HEADS, HEAD_DIM),
        heads(v_f[T:], N_KV_HEADS).reshape(1, DB, DT, N_KV_HEADS, HEAD_DIM),
        ki_f[T:].reshape(1, DB, DT, IDX_DIM),
        qkv[T:].reshape(DB, DT, ch)[:, DT - nc:][None],
        ssm_s.astype(state_ssm.dtype)[None],
    )
```

```python
import functools
import math

import numpy as np
import jax
import jax.numpy as jnp
from jax import lax
from jax.experimental import pallas as pl
from jax.experimental.pallas import tpu as pltpu

F32 = jnp.float32
BF16 = jnp.bfloat16

N_HEADS = 16
N_KV_HEADS = 4
HEAD_DIM = 128
GROUPS = N_HEADS // N_KV_HEADS
N_IDX_HEADS = 32
IDX_DIM = 128
TOP_K_MAX = 256
PAGE_SIZE = 128
N_BUCKETS = 32
MAX_DISTANCE = 128
GDN_HEADS = 16
GDN_DK = 128
GDN_DV = 128
CONV_WIDTH = 4
GDN_CHUNK = 64
EPS = 1e-6
NEG = -1e30
LOG2E = math.log2(math.e)

LANES = 128
SUBLANES = 8
VMEM_LIMIT = 56 * 1024 * 1024


def _cparams(*sem):
    return pltpu.CompilerParams(dimension_semantics=sem, vmem_limit_bytes=VMEM_LIMIT)


def _rms_rows(x_ref, nw_ref, h_ref, rows):
    tm = x_ref.shape[0]
    rows = math.gcd(rows, tm)
    nw = nw_ref[...]

    def body(r, _):
        sl = pl.ds(pl.multiple_of(r * rows, rows), rows)
        x = x_ref[sl, :]
        ms = jnp.mean(x * x, axis=-1, keepdims=True)
        h_ref[sl, :] = (x * lax.rsqrt(ms + EPS) * nw).astype(h_ref.dtype)
        return 0

    lax.fori_loop(0, tm // rows, body, 0)


PROJ_TN = 512


def _proj_body(x_ref, nw_ref, w_ref, q_ref, k_ref, v_ref, kvb_ref, qi_ref, qkv_ref, gates_ref, misc_ref, h_ref,
               *, routes):
    j = pl.program_id(1)

    @pl.when(j == 0)
    def _():
        _rms_rows(x_ref, nw_ref, h_ref, 64)

    acc = jnp.dot(h_ref[...], w_ref[...], preferred_element_type=F32)
    hpt = PROJ_TN // HEAD_DIM

    def on(name):
        lo, n = routes[name]
        return pl.when((j >= lo) & (j < lo + n))

    @on("q")
    def _():
        for hh in range(hpt):
            q_ref[hh] = acc[:, hh * HEAD_DIM:(hh + 1) * HEAD_DIM].astype(q_ref.dtype)

    @on("k")
    def _():
        k_ref[...] = acc
        kvb_ref[...] = acc.astype(kvb_ref.dtype)

    @on("v")
    def _():
        v_ref[...] = acc
        kvb_ref[...] = acc.astype(kvb_ref.dtype)

    @on("qi")
    def _():
        for hh in range(hpt):
            qi_ref[hh] = acc[:, hh * IDX_DIM:(hh + 1) * IDX_DIM].astype(qi_ref.dtype)

    @on("qkv")
    def _():
        qkv_ref[...] = acc

    @on("gates")
    def _():
        gates_ref[...] = acc

    @on("misc")
    def _():
        misc_ref[...] = acc


def _proj(x, norm_w, w_packed, tm):
    M, D = x.shape
    tn = PROJ_TN
    widths = dict(q=2048, k=512, v=512, qi=4096, qkv=6144, gates=10240, misc=512)
    routes, lo = {}, 0
    for name, wd in widths.items():
        routes[name] = (lo, wd // tn)
        lo += wd // tn
    nj = lo
    assert w_packed.shape == (D, nj * tn)

    def col(name):
        a, n = routes[name]
        return lambda i, j: (i, jnp.clip(j - a, 0, n - 1))

    def heads(name):
        a, n = routes[name]
        return lambda i, j: (jnp.clip(j - a, 0, n - 1), i, 0)

    def kvb_map(i, j):
        return (i, jnp.clip(j - routes["k"][0], 0, 1))

    hpt = tn // HEAD_DIM
    out_shape = (
        jax.ShapeDtypeStruct((N_HEADS, M, HEAD_DIM), BF16),
        jax.ShapeDtypeStruct((M, 512), F32),
        jax.ShapeDtypeStruct((M, 512), F32),
        jax.ShapeDtypeStruct((M, 1024), BF16),
        jax.ShapeDtypeStruct((N_IDX_HEADS, M, IDX_DIM), BF16),
        jax.ShapeDtypeStruct((M, 6144), F32),
        jax.ShapeDtypeStruct((M, 10240), F32),
        jax.ShapeDtypeStruct((M, 512), F32),
    )
    out_specs = (
        pl.BlockSpec((hpt, tm, HEAD_DIM), heads("q")),
        pl.BlockSpec((tm, tn), col("k")),
        pl.BlockSpec((tm, tn), col("v")),
        pl.BlockSpec((tm, tn), kvb_map),
        pl.BlockSpec((hpt, tm, IDX_DIM), heads("qi")),
        pl.BlockSpec((tm, tn), col("qkv")),
        pl.BlockSpec((tm, tn), col("gates")),
        pl.BlockSpec((tm, tn), col("misc")),
    )
    return pl.pallas_call(
        functools.partial(_proj_body, routes=routes),
        grid=(M // tm, nj),
        in_specs=[
            pl.BlockSpec((tm, D), lambda i, j: (i, 0), pipeline_mode=pl.Buffered(1)),
            pl.BlockSpec((1, D), lambda i, j: (0, 0)),
            pl.BlockSpec((D, tn), lambda i, j: (0, j)),
        ],
        out_specs=out_specs,
        out_shape=out_shape,
        scratch_shapes=[pltpu.VMEM((tm, D), BF16)],
        compiler_params=_cparams("parallel", "arbitrary"),
        name="in_proj",
    )(x, norm_w.reshape(1, D), w_packed)


def _merge_body(attn_ref, o_ref, ga_ref, gb_ref, wa_ref, wg_ref, out_ref):
    a = jnp.dot(attn_ref[...], wa_ref[...], preferred_element_type=F32)
    b = jnp.dot(o_ref[...], wg_ref[...], preferred_element_type=F32)
    out_ref[...] = (jax.nn.sigmoid(ga_ref[...]) * a + jax.nn.sigmoid(gb_ref[...]) * b).astype(out_ref.dtype)


def _merge(attn, o, gates, wa, wg, tm, tn=512):
    M, KA = attn.shape
    D = wa.shape[1]
    a_off = 2048 // tn
    b_off = (2048 + D) // tn
    return pl.pallas_call(
        _merge_body,
        grid=(M // tm, D // tn),
        in_specs=[
            pl.BlockSpec((tm, KA), lambda i, j: (i, 0)),
            pl.BlockSpec((tm, KA), lambda i, j: (i, 0)),
            pl.BlockSpec((tm, tn), lambda i, j: (i, a_off + j)),
            pl.BlockSpec((tm, tn), lambda i, j: (i, b_off + j)),
            pl.BlockSpec((KA, tn), lambda i, j: (0, j)),
            pl.BlockSpec((KA, tn), lambda i, j: (0, j)),
        ],
        out_specs=pl.BlockSpec((tm, tn), lambda i, j: (i, j)),
        out_shape=jax.ShapeDtypeStruct((M, D), BF16),
        compiler_params=_cparams("parallel", "arbitrary"),
        name="merge",
    )(attn, o, gates, gates, wa, wg)


def _resid_mm_body(x_ref, a_ref, w_ref, out_ref):
    out_ref[...] = x_ref[...] + jnp.dot(a_ref[...], w_ref[...], preferred_element_type=F32)


def _resid_mm(x, a, w, tm, tn):
    M, N = x.shape
    Kd = a.shape[1]
    return pl.pallas_call(
        _resid_mm_body,
        grid=(M // tm, N // tn),
        in_specs=[
            pl.BlockSpec((tm, tn), lambda i, j: (i, j)),
            pl.BlockSpec((tm, Kd), lambda i, j: (i, 0), pipeline_mode=pl.Buffered(1)),
            pl.BlockSpec((Kd, tn), lambda i, j: (0, j)),
        ],
        out_specs=pl.BlockSpec((tm, tn), lambda i, j: (i, j)),
        out_shape=jax.ShapeDtypeStruct((M, N), F32),
        compiler_params=_cparams("parallel", "arbitrary"),
        name="resid_mm",
    )(x, a, w)


def _ffn_up_body(x_ref, nw_ref, wg_ref, wu_ref, out_ref, h_ref):
    @pl.when(pl.program_id(1) == 0)
    def _():
        _rms_rows(x_ref, nw_ref, h_ref, 64)

    h = h_ref[...]
    g = jnp.dot(h, wg_ref[...], preferred_element_type=F32)
    u = jnp.dot(h, wu_ref[...], preferred_element_type=F32)
    out_ref[...] = (jax.nn.silu(g) * u).astype(out_ref.dtype)


def _ffn_up(x, norm_w, w_gu, tm, tn=512):
    M, D = x.shape
    ffp = w_gu.shape[1] // 2
    nj = ffp // tn
    return pl.pallas_call(
        _ffn_up_body,
        grid=(M // tm, nj),
        in_specs=[
            pl.BlockSpec((tm, D), lambda i, j: (i, 0), pipeline_mode=pl.Buffered(1)),
            pl.BlockSpec((1, D), lambda i, j: (0, 0)),
            pl.BlockSpec((D, tn), lambda i, j: (0, j)),
            pl.BlockSpec((D, tn), lambda i, j: (0, nj + j)),
        ],
        out_specs=pl.BlockSpec((tm, tn), lambda i, j: (i, j)),
        out_shape=jax.ShapeDtypeStruct((M, ffp), BF16),
        scratch_shapes=[pltpu.VMEM((tm, D), BF16)],
        compiler_params=_cparams("parallel", "arbitrary"),
        name="ffn_up",
    )(x, norm_w.reshape(1, D), w_gu, w_gu)


def _ple_body(x_ref, nw_ref, wg_ref, p_ref, wp_ref, nf_ref, y_ref, h_ref, x3_ref, *, tn):
    j = pl.program_id(1)

    @pl.when(j == 0)
    def _():
        _rms_rows(x_ref, nw_ref, h_ref, 64)

    g = jnp.dot(h_ref[...], wg_ref[...], preferred_element_type=F32)
    e = jnp.dot(p_ref[...], wp_ref[...], preferred_element_type=F32)
    cols = pl.ds(pl.multiple_of(j * tn, tn), tn)
    x3_ref[:, cols] = x_ref[:, cols] + jax.nn.sigmoid(g) * e

    @pl.when(j == pl.num_programs(1) - 1)
    def _():
        _rms_rows(x3_ref, nf_ref, y_ref, 64)


def _ple_final(x, norm_w, w_gate, p, w_ple, norm_final, tm, tn=512):
    M, D = x.shape
    P = p.shape[1]
    return pl.pallas_call(
        functools.partial(_ple_body, tn=tn),
        grid=(M // tm, D // tn),
        in_specs=[
            pl.BlockSpec((tm, D), lambda i, j: (i, 0), pipeline_mode=pl.Buffered(1)),
            pl.BlockSpec((1, D), lambda i, j: (0, 0)),
            pl.BlockSpec((D, tn), lambda i, j: (0, j)),
            pl.BlockSpec((tm, P), lambda i, j: (i, 0)),
            pl.BlockSpec((P, tn), lambda i, j: (0, j)),
            pl.BlockSpec((1, D), lambda i, j: (0, 0)),
        ],
        out_specs=pl.BlockSpec((tm, D), lambda i, j: (i, 0)),
        out_shape=jax.ShapeDtypeStruct((M, D), F32),
        scratch_shapes=[pltpu.VMEM((tm, D), BF16), pltpu.VMEM((tm, D), F32)],
        compiler_params=_cparams("parallel", "arbitrary"),
        name="ple_final",
    )(x, norm_w.reshape(1, D), w_gate, p, w_ple, norm_final.reshape(1, D))


def _bucket_thresholds():
    d = np.arange(0, 4 * MAX_DISTANCE)
    max_exact = N_BUCKETS // 2
    large = max_exact + (np.log(np.maximum(d, 1) / max_exact) / math.log(MAX_DISTANCE / max_exact)
                         * (N_BUCKETS - max_exact)).astype(np.int32)
    b = np.where(d < max_exact, d, np.minimum(large, N_BUCKETS - 1))
    return [int(np.argmax(b >= k)) for k in range(N_BUCKETS)]


_BUCKET_THR = _bucket_thresholds()


def _bias_of_dist(rb_ref, h, d):
    v = jnp.full(d.shape, rb_ref[0, h], F32)
    for b in range(1, N_BUCKETS):
        v = jnp.where(d >= _BUCKET_THR[b], rb_ref[b, h], v)
    return v


def _bias_tables_body(rb_ref, tprev_ref, tdiag_ref, tsamp_ref, *, qb, past):
    h = pl.program_id(0)
    r = lax.broadcasted_iota(jnp.int32, (qb, qb), 0)
    c = lax.broadcasted_iota(jnp.int32, (qb, qb), 1)
    far = rb_ref[N_BUCKETS - 1, h]
    tprev_ref[0] = (_bias_of_dist(rb_ref, h, r + qb - c) - far) * LOG2E
    tdiag_ref[0] = (_bias_of_dist(rb_ref, h, r - c) - far) * LOG2E
    ls = tsamp_ref.shape[2]
    t = lax.broadcasted_iota(jnp.int32, (SUBLANES, ls), 0) % 4
    lane = lax.broadcasted_iota(jnp.int32, (SUBLANES, ls), 1)
    d = jnp.where(lane < past, past + t - lane, t - (lane - past) % 4)
    tsamp_ref[0] = _bias_of_dist(rb_ref, h, d)


def _bias_tables(rel_bias, qb, past, ls):
    return pl.pallas_call(
        functools.partial(_bias_tables_body, qb=qb, past=past),
        grid=(N_HEADS,),
        in_specs=[pl.BlockSpec(memory_space=pltpu.SMEM)],
        out_specs=(
            pl.BlockSpec((1, qb, qb), lambda h: (h, 0, 0)),
            pl.BlockSpec((1, qb, qb), lambda h: (h, 0, 0)),
            pl.BlockSpec((1, SUBLANES, ls), lambda h: (h, 0, 0)),
        ),
        out_shape=(
            jax.ShapeDtypeStruct((N_HEADS, qb, qb), F32),
            jax.ShapeDtypeStruct((N_HEADS, qb, qb), F32),
            jax.ShapeDtypeStruct((N_HEADS, SUBLANES, ls), F32),
        ),
        compiler_params=_cparams("arbitrary"),
        name="bias_tables",
    )(rel_bias)


INT_MIN = -2 ** 31


def _sort_key(score, valid):
    bits = lax.bitcast_convert_type(score, jnp.int32)
    key = jnp.where(bits < 0, bits ^ jnp.int32(0x7FFFFFFF), bits)
    return jnp.where(valid, key, jnp.int32(INT_MIN))


def _kth_largest(count_ge, rows, k):
    def step(b, t):
        cand = t + lax.shift_left(jnp.int32(1), jnp.asarray(31 - b, jnp.int32))
        n = count_ge(cand)
        return jnp.where(n >= k, cand, t)

    return lax.fori_loop(0, 32, step, jnp.full((rows, LANES), INT_MIN, jnp.int32))


PQB = 256
PKC = 256


def _pattn_body(qi_ref, wi_ref, ki_ref, q_ref, k_ref, v_ref, tprev_ref, tdiag_ref, out_ref,
                key_ref, wb_ref, m_ref, l_ref, acc_ref, *, idx_scale, n_sel):
    i = pl.program_id(0)
    qb, kc = PQB, PKC
    hg = 8

    wi = wi_ref[...] * idx_scale
    for h in range(N_IDX_HEADS):
        wb_ref[h] = jnp.broadcast_to(wi[:, h:h + 1], (qb, LANES))

    def chunk(c):
        return pl.ds(pl.multiple_of(c * kc, kc), kc)

    row = lax.broadcasted_iota(jnp.int32, (qb, kc), 0)
    col = lax.broadcasted_iota(jnp.int32, (qb, kc), 1)

    def score_chunk(c, _):
        kic = ki_ref[chunk(c), :]
        acc = [jnp.zeros((qb, LANES), F32) for _ in range(kc // LANES)]
        for g in range(N_IDX_HEADS // hg):
            qg = qi_ref[g * hg:(g + 1) * hg].reshape(hg * qb, IDX_DIM)
            s = lax.dot_general(qg, kic, (((1,), (1,)), ((), ())), preferred_element_type=F32)
            for hh in range(hg):
                w = wb_ref[g * hg + hh]
                for half in range(kc // LANES):
                    sh = s[hh * qb:(hh + 1) * qb, half * LANES:(half + 1) * LANES]
                    acc[half] = acc[half] + jnp.maximum(sh, 0.0) * w
        score = jnp.concatenate(acc, axis=1)
        valid = (c * kc + col) <= (i * qb + row)
        key_ref[:, chunk(c)] = _sort_key(score, valid)
        return 0

    lax.fori_loop(0, i + 1, score_chunk, 0)

    def count_ge(cand):
        def body(c, cnt):
            kk = key_ref[:, chunk(c)]
            for half in range(kc // LANES):
                cnt = cnt + jnp.where(kk[:, half * LANES:(half + 1) * LANES] >= cand, 1.0, 0.0)
            return cnt

        cnt = lax.fori_loop(0, i + 1, body, jnp.zeros((qb, LANES), F32))
        return jnp.sum(cnt, axis=1, keepdims=True)

    thr = _kth_largest(count_ge, qb, float(n_sel))

    def mask_chunk(c, _):
        kk = key_ref[:, chunk(c)]
        t2 = jnp.concatenate([thr] * (kc // LANES), axis=1)
        sel = (kk >= t2) & (kk > jnp.int32(INT_MIN))
        key_ref[:, chunk(c)] = lax.bitcast_convert_type(jnp.where(sel, 0.0, NEG).astype(F32), jnp.int32)
        return 0

    lax.fori_loop(0, i + 1, mask_chunk, 0)

    c1 = HEAD_DIM ** -0.5 * LOG2E

    def kv_head(kh, _):
        qs = q_ref[pl.ds(kh * GROUPS, GROUPS)].reshape(GROUPS * qb, HEAD_DIM)
        lanes = pl.ds(pl.multiple_of(kh * HEAD_DIM, HEAD_DIM), HEAD_DIM)
        m_ref[...] = jnp.full(m_ref.shape, NEG, F32)
        l_ref[...] = jnp.zeros(l_ref.shape, F32)
        acc_ref[...] = jnp.zeros(acc_ref.shape, F32)

        def attend(c, bias_of_group):
            kt = k_ref[chunk(c), lanes]
            vt = v_ref[chunk(c), lanes]
            vx = jnp.concatenate([vt, jnp.ones_like(vt)], axis=1)
            s = lax.dot_general(qs, kt, (((1,), (1,)), ((), ())), preferred_element_type=F32)
            mb = lax.bitcast_convert_type(key_ref[:, chunk(c)], F32)
            ts = []
            for g in range(GROUPS):
                t = s[g * qb:(g + 1) * qb] * c1 + mb
                if bias_of_group is not None:
                    t = t + bias_of_group(g)
                ts.append(t)
            t = jnp.concatenate(ts, axis=0)
            m_old = m_ref[...]
            m_new = jnp.maximum(m_old, jnp.max(t, axis=1, keepdims=True))
            alpha = jnp.exp2(m_old - m_new)
            p = jnp.exp2(t - jnp.concatenate([m_new] * (kc // LANES), axis=1))
            pv = jnp.dot(p.astype(BF16), vx, preferred_element_type=F32)
            acc_ref[...] = alpha * acc_ref[...] + pv[:, :HEAD_DIM]
            l_ref[...] = alpha * l_ref[...] + pv[:, HEAD_DIM:]
            m_ref[...] = m_new

        def far(c, _):
            attend(c, None)
            return 0

        lax.fori_loop(0, jnp.maximum(i - 1, 0), far, 0)

        @pl.when(i >= 1)
        def _():
            attend(i - 1, lambda g: tprev_ref[kh * GROUPS + g])

        attend(i, lambda g: tdiag_ref[kh * GROUPS + g])

        o = acc_ref[...] / l_ref[...]
        for g in range(GROUPS):
            out_ref[:, pl.ds(pl.multiple_of((kh * GROUPS + g) * HEAD_DIM, HEAD_DIM), HEAD_DIM)] = (
                o[g * qb:(g + 1) * qb].astype(out_ref.dtype))
        return 0

    lax.fori_loop(0, N_KV_HEADS, kv_head, 0)


def _prompt_attention(qi_hm, wi, ki_bf, q_hm, kv_bf, tprev, tdiag, seq, n_sel):
    qb = PQB
    resident = dict(pipeline_mode=pl.Buffered(1))
    return pl.pallas_call(
        functools.partial(_pattn_body, idx_scale=(N_IDX_HEADS * IDX_DIM) ** -0.5, n_sel=n_sel),
        grid=(seq // qb,),
        in_specs=[
            pl.BlockSpec((N_IDX_HEADS, qb, IDX_DIM), lambda i: (0, i, 0)),
            pl.BlockSpec((qb, N_IDX_HEADS), lambda i: (i, 0)),
            pl.BlockSpec((seq, IDX_DIM), lambda i: (0, 0), **resident),
            pl.BlockSpec((N_HEADS, qb, HEAD_DIM), lambda i: (0, i, 0)),
            pl.BlockSpec((seq, N_KV_HEADS * HEAD_DIM), lambda i: (0, 0), **resident),
            pl.BlockSpec((seq, N_KV_HEADS * HEAD_DIM), lambda i: (0, 1), **resident),
            pl.BlockSpec((N_HEADS, qb, qb), lambda i: (0, 0, 0), **resident),
            pl.BlockSpec((N_HEADS, qb, qb), lambda i: (0, 0, 0), **resident),
        ],
        out_specs=pl.BlockSpec((qb, N_HEADS * HEAD_DIM), lambda i: (i, 0)),
        out_shape=jax.ShapeDtypeStruct((seq, N_HEADS * HEAD_DIM), BF16),
        scratch_shapes=[
            pltpu.VMEM((qb, seq), jnp.int32),
            pltpu.VMEM((N_IDX_HEADS, qb, LANES), F32),
            pltpu.VMEM((GROUPS * qb, LANES), F32),
            pltpu.VMEM((GROUPS * qb, LANES), F32),
            pltpu.VMEM((GROUPS * qb, HEAD_DIM), F32),
        ],
        compiler_params=_cparams("arbitrary"),
        name="prompt_attention",
    )(qi_hm, wi, ki_bf, q_hm, kv_bf, kv_bf, tprev, tdiag)


GA_LANE = 32
GB_LANE = 48
HALO = SUBLANES
HI = lax.Precision.HIGHEST


def _mm(a, b):
    return jnp.dot(a, b, precision=HI, preferred_element_type=F32)


def _mm_nt(a, b):
    return lax.dot_general(a, b, (((1,), (1,)), ((), ())), precision=HI, preferred_element_type=F32)


def _bmm(a, b):
    return jnp.dot(a.astype(BF16), b.astype(BF16), preferred_element_type=F32)


def _bmm_nt(a, b):
    return lax.dot_general(a.astype(BF16), b.astype(BF16), (((1,), (1,)), ((), ())), preferred_element_type=F32)


def _bmm_tn(a, b):
    return lax.dot_general(a.astype(BF16), b.astype(BF16), (((0,), (0,)), ((), ())), preferred_element_type=F32)


def _gdn_body(qkv_ref, slab_ref, gz_ref, cw_ref, alog_ref, dtb_ref, gn_ref, halo0_ref, s0_ref,
              o_ref, s_ref, xp_ref, *, C, valid_rows):
    c = pl.program_id(1)
    kw, vw = GDN_HEADS * GDN_DK, GDN_HEADS * GDN_DV

    @pl.when(c == 0)
    def _():
        xp_ref[0:HALO] = halo0_ref[0]
        s_ref[...] = s0_ref[...]

    xp_ref[HALO:HALO + C] = qkv_ref[...]

    ri = lax.broadcasted_iota(jnp.int32, (C, C), 0)
    ci = lax.broadcasted_iota(jnp.int32, (C, C), 1)
    causal = ri >= ci
    strict = ri > ci
    eye = jnp.where(ri == ci, 1.0, 0.0).astype(F32)
    ltri = jnp.where(causal, 1.0, 0.0).astype(F32)

    slab = slab_ref[...]
    live = lax.broadcasted_iota(jnp.int32, slab.shape, 0) < valid_rows
    g_all = jnp.where(live, -jnp.exp(alog_ref[...]) * jax.nn.softplus(slab + dtb_ref[...]), 0.0)
    beta_all = jnp.where(live, jax.nn.sigmoid(slab), 0.0)
    gc_all = _mm(ltri, g_all)
    pad = jnp.zeros((LANES - C, LANES), F32)
    gc_t = jnp.concatenate([gc_all, pad], axis=0).T

    def conv(cols):
        y = cw_ref[0:1, cols] * xp_ref[HALO - 3:HALO - 3 + C, cols]
        for j in range(1, CONV_WIDTH):
            y = y + cw_ref[j:j + 1, cols] * xp_ref[HALO - 3 + j:HALO - 3 + j + C, cols]
        return jax.nn.silu(y)

    def l2n(x):
        return x * lax.rsqrt(jnp.sum(x * x, axis=-1, keepdims=True) + 1e-6)

    n_sq = max(1, (C - 1).bit_length() - 1)

    hs = range(GDN_HEADS)
    q = [l2n(conv(slice(h * GDN_DK, (h + 1) * GDN_DK))) * GDN_DK ** -0.5 for h in hs]
    k = [l2n(conv(slice(kw + h * GDN_DK, kw + (h + 1) * GDN_DK))) for h in hs]
    v = [conv(slice(2 * kw + h * GDN_DV, 2 * kw + (h + 1) * GDN_DV)) for h in hs]
    beta = [beta_all[:, GB_LANE + h:GB_LANE + h + 1] for h in hs]
    gc = [gc_all[:, GA_LANE + h:GA_LANE + h + 1] for h in hs]
    gc_row = [gc_t[GA_LANE + h:GA_LANE + h + 1, 0:C] for h in hs]
    gc_last = [gc_all[C - 1:C, GA_LANE + h:GA_LANE + h + 1] for h in hs]
    decay = [jnp.exp(jnp.where(causal, gc[h] - gc_row[h], NEG)) for h in hs]
    kb = [k[h] * beta[h] for h in hs]
    p = [-jnp.where(strict, _bmm_nt(kb[h], k[h]) * decay[h], 0.0) for h in hs]
    t_inv = [eye + p[h] for h in hs]
    for _ in range(n_sq):
        p = [_mm(x, x) for x in p]
        t_inv = [t + _mm(t, x) for t, x in zip(t_inv, p)]
    egc = [jnp.exp(gc[h]) for h in hs]
    u = [_bmm(t_inv[h], v[h] * beta[h]) for h in hs]
    w = [_bmm(t_inv[h], kb[h] * egc[h]) for h in hs]
    a_in = [jnp.where(causal, _bmm_nt(q[h], k[h]) * decay[h], 0.0) for h in hs]
    qd = [q[h] * egc[h] for h in hs]
    kt = [k[h] * jnp.exp(gc_last[h] - gc[h]) for h in hs]
    s_old = [s_ref[0, h] for h in hs]
    v_new = [u[h] - _bmm(w[h], s_old[h]) for h in hs]
    o = [_bmm(qd[h], s_old[h]) + _bmm(a_in[h], v_new[h]) for h in hs]
    for h in hs:
        s_ref[0, h] = s_old[h] * jnp.exp(gc_last[h]) + _bmm_tn(kt[h], v_new[h])
    for h in hs:
        on = o[h] * lax.rsqrt(jnp.mean(o[h] * o[h], axis=-1, keepdims=True) + EPS) * gn_ref[...]
        gz = gz_ref[:, h * GDN_DV:(h + 1) * GDN_DV]
        o_ref[:, h * GDN_DV:(h + 1) * GDN_DV] = (on * jax.nn.silu(gz)).astype(o_ref.dtype)

    xp_ref[0:HALO] = xp_ref[C:C + HALO]


def _gdn(qkv, misc, gates, conv_w, alog_vec, dtb_vec, gdn_norm, halo0, s0, *, n_seq, n_chunks, C, valid_rows,
         out_dtype):
    ch = qkv.shape[1]
    vw = GDN_HEADS * GDN_DV
    rows = n_seq * n_chunks * C

    def rmap(s, c):
        return (s * n_chunks + c, 0)

    return pl.pallas_call(
        functools.partial(_gdn_body, C=C, valid_rows=valid_rows),
        grid=(n_seq, n_chunks),
        in_specs=[
            pl.BlockSpec((C, ch), rmap),
            pl.BlockSpec((C, LANES), lambda s, c: (s * n_chunks + c, 1)),
            pl.BlockSpec((C, vw), rmap),
            pl.BlockSpec((CONV_WIDTH, ch), lambda s, c: (0, 0)),
            pl.BlockSpec((1, LANES), lambda s, c: (0, 0)),
            pl.BlockSpec((1, LANES), lambda s, c: (0, 0)),
            pl.BlockSpec((1, GDN_DV), lambda s, c: (0, 0)),
            pl.BlockSpec((1, HALO, ch), lambda s, c: (s, 0, 0)),
            pl.BlockSpec((1, GDN_HEADS, GDN_DK, GDN_DV), lambda s, c: (s, 0, 0, 0)),
        ],
        out_specs=(
            pl.BlockSpec((C, vw), rmap),
            pl.BlockSpec((1, GDN_HEADS, GDN_DK, GDN_DV), lambda s, c: (s, 0, 0, 0)),
        ),
        out_shape=(
            jax.ShapeDtypeStruct((rows, vw), out_dtype),
            jax.ShapeDtypeStruct((n_seq, GDN_HEADS, GDN_DK, GDN_DV), F32),
        ),
        scratch_shapes=[pltpu.VMEM((C + HALO, ch), F32)],
        compiler_params=_cparams("arbitrary", "arbitrary"),
        name=f"gdn_c{C}",
    )(qkv, misc, gates, conv_w, alog_vec, dtb_vec, gdn_norm.reshape(1, GDN_DV), halo0, s0)


SB = 2
ST = SUBLANES // SB


def _sattn_body(pt_ref, qi_ref, wi_ref, q_ref, kin_ref, kn_ref, vn_ref, tsamp_ref, cidx_hbm, ck_hbm, cv_hbm, out_ref,
                kibuf, kbuf, vbuf, sem, *, past, n_pages, n_sel, idx_scale):
    p = pl.program_id(0)
    n_steps = pl.num_programs(0)
    ls = kibuf.shape[1]
    rows = SB * ST

    def copies(step, bl):
        b = step * SB + bl
        out = []
        for j in range(n_pages):
            pg = pt_ref[b, j]
            dst = pl.ds(j * PAGE_SIZE, PAGE_SIZE)
            out.append(pltpu.make_async_copy(cidx_hbm.at[pg], kibuf.at[bl, dst], sem.at[bl]))
            for kh in range(N_KV_HEADS):
                out.append(pltpu.make_async_copy(ck_hbm.at[pg, :, kh], kbuf.at[bl, kh, dst], sem.at[bl]))
                out.append(pltpu.make_async_copy(cv_hbm.at[pg, :, kh], vbuf.at[bl, kh, dst], sem.at[bl]))
        return out

    @pl.when(p == 0)
    def _():
        for bl in range(SB):
            for cp in copies(0, bl):
                cp.start()
            kibuf[bl, past:ls] = jnp.zeros((ls - past, IDX_DIM), F32)
            kbuf[bl, :, past:ls] = jnp.zeros((N_KV_HEADS, ls - past, HEAD_DIM), F32)
            vbuf[bl, :, past:ls] = jnp.zeros((N_KV_HEADS, ls - past, HEAD_DIM), F32)

    bt = lax.broadcasted_iota(jnp.int32, (rows, ls), 0)
    lane = lax.broadcasted_iota(jnp.int32, (rows, ls), 1)
    r_new = lane - past
    valid = (lane < past) | ((r_new < rows) & (r_new // ST == bt // ST) & (r_new % ST <= bt % ST))

    for bl in range(SB):
        kibuf[bl, past:past + rows] = kin_ref[...]
        for kh in range(N_KV_HEADS):
            kbuf[bl, kh, past:past + rows] = kn_ref[:, kh * HEAD_DIM:(kh + 1) * HEAD_DIM]
            vbuf[bl, kh, past:past + rows] = vn_ref[:, kh * HEAD_DIM:(kh + 1) * HEAD_DIM]

    wi = wi_ref[...] * idx_scale
    qi = qi_ref[...].reshape(N_IDX_HEADS * rows, IDX_DIM).astype(BF16)
    key_chunks = [(s, min(512, ls - s)) for s in range(0, ls, 512)]

    def scores(bl):
        parts = []
        for s0, n in key_chunks:
            kic = kibuf[bl, s0:s0 + n].astype(BF16)
            s = lax.dot_general(qi, kic, (((1,), (1,)), ((), ())), preferred_element_type=F32)
            acc = jnp.zeros((rows, n), F32)
            for h in range(N_IDX_HEADS):
                acc = acc + jnp.maximum(s[h * rows:(h + 1) * rows], 0.0) * wi[:, h:h + 1]
            parts.append(acc)
        return jnp.concatenate(parts, axis=1)

    sc = []
    for bl in range(SB):
        for cp in copies(p, bl):
            cp.wait()
        sc.append(scores(bl))
    score = sc[0]
    for bl in range(1, SB):
        score = jnp.where(bt // ST == bl, sc[bl], score)
    key = _sort_key(score, valid)

    def count_ge(cand):
        cnt = jnp.zeros((rows, LANES), F32)
        for t in range(ls // LANES):
            cnt = cnt + jnp.where(key[:, t * LANES:(t + 1) * LANES] >= cand, 1.0, 0.0)
        return jnp.sum(cnt, axis=1, keepdims=True)

    thr = _kth_largest(count_ge, rows, float(n_sel))
    thr_full = jnp.concatenate([thr] * (ls // LANES), axis=1)
    mb = jnp.where((key >= thr_full) & valid, 0.0, NEG).astype(F32)

    scale = HEAD_DIM ** -0.5
    row_b = lax.broadcasted_iota(jnp.int32, (GROUPS * rows, HEAD_DIM), 0) % rows // ST
    outs = [None] * N_KV_HEADS
    for bl in range(SB):
        for kh in range(N_KV_HEADS):
            qs = q_ref[kh * GROUPS:(kh + 1) * GROUPS].reshape(GROUPS * rows, HEAD_DIM).astype(BF16)
            kt = kbuf[bl, kh].astype(BF16)
            vt = vbuf[bl, kh].astype(BF16)
            s = lax.dot_general(qs, kt, (((1,), (1,)), ((), ())), preferred_element_type=F32) * scale
            s = (s.reshape(GROUPS, rows, ls) + tsamp_ref[kh * GROUPS:(kh + 1) * GROUPS] + mb[None]).reshape(GROUPS * rows, ls)
            m = jnp.max(s, axis=1, keepdims=True)
            e = jnp.exp(s - m)
            o = jnp.dot(e.astype(BF16), vt, preferred_element_type=F32) / jnp.sum(e, axis=1, keepdims=True)
            outs[kh] = o if bl == 0 else jnp.where(row_b == bl, o, outs[kh])

        @pl.when(p + 1 < n_steps)
        def _():
            for cp in copies(p + 1, bl):
                cp.start()

    for kh in range(N_KV_HEADS):
        for g in range(GROUPS):
            hd = kh * GROUPS + g
            out_ref[:, hd * HEAD_DIM:(hd + 1) * HEAD_DIM] = outs[kh][g * rows:(g + 1) * rows]


def _sample_attention(page_table, qi_s, wi_s, q_s, ki_new, k_new, v_new, tsamp, cache_idx, cache_k, cache_v, n_sel):
    R = wi_s.shape[0]
    n_b, n_pages = page_table.shape
    past = n_pages * PAGE_SIZE
    ls = past + PAGE_SIZE
    rows = SB * ST
    assert R == n_b * ST and n_b % SB == 0
    kvw = N_KV_HEADS * HEAD_DIM
    grid_spec = pltpu.PrefetchScalarGridSpec(
        num_scalar_prefetch=1,
        grid=(n_b // SB,),
        in_specs=[
            pl.BlockSpec((N_IDX_HEADS, rows, IDX_DIM), lambda p, pt: (0, p, 0)),
            pl.BlockSpec((rows, N_IDX_HEADS), lambda p, pt: (p, 0)),
            pl.BlockSpec((N_HEADS, rows, HEAD_DIM), lambda p, pt: (0, p, 0)),
            pl.BlockSpec((rows, IDX_DIM), lambda p, pt: (p, 0)),
            pl.BlockSpec((rows, kvw), lambda p, pt: (p, 0)),
            pl.BlockSpec((rows, kvw), lambda p, pt: (p, 0)),
            pl.BlockSpec((N_HEADS, SUBLANES, ls), lambda p, pt: (0, 0, 0)),
            pl.BlockSpec(memory_space=pl.ANY),
            pl.BlockSpec(memory_space=pl.ANY),
            pl.BlockSpec(memory_space=pl.ANY),
        ],
        out_specs=pl.BlockSpec((rows, N_HEADS * HEAD_DIM), lambda p, pt: (p, 0)),
        scratch_shapes=[
            pltpu.VMEM((SB, ls, IDX_DIM), F32),
            pltpu.VMEM((SB, N_KV_HEADS, ls, HEAD_DIM), F32),
            pltpu.VMEM((SB, N_KV_HEADS, ls, HEAD_DIM), F32),
            pltpu.SemaphoreType.DMA((SB,)),
        ],
    )
    return pl.pallas_call(
        functools.partial(_sattn_body, past=past, n_pages=n_pages, n_sel=n_sel,
                          idx_scale=(N_IDX_HEADS * IDX_DIM) ** -0.5),
        grid_spec=grid_spec,
        out_shape=jax.ShapeDtypeStruct((R, N_HEADS * HEAD_DIM), F32),
        compiler_params=_cparams("arbitrary"),
        name="sample_attention",
    )(page_table, qi_s, wi_s, q_s, ki_new, k_new, v_new, tsamp, cache_idx, cache_k, cache_v)


ROW_TILE_CAP = 512
FF_TILE = 512


def _row_tile(m, cap=ROW_TILE_CAP):
    return max(t for t in range(16, cap + 1, 16) if m % t == 0)


def _pad_tokens(a, db, dt):
    a = a.reshape(db, dt, a.shape[-1])
    return jnp.pad(a, ((0, 0), (0, SUBLANES - dt), (0, 0))).reshape(db * SUBLANES, a.shape[-1])


def kernel(x_prompt, x_sample, cache_k, cache_v, cache_idx_k, state_conv, state_ssm, page_table, p_prompt, p_sample,
           rel_bias, norm_mix, w_in, conv_w, a_log, dt_bias, gdn_norm, w_attn_up, w_gdn_up, w_out, norm_ffn, w_gate_up,
           w_down, norm_ple, w_ple_gate, w_ple, norm_final):
    assert x_prompt.shape[0] == 1 and w_in.shape[0] == 1, "one prompt sequence, one layer"
    _, T, D = x_prompt.shape
    DB, DT, _ = x_sample.shape
    assert DT == ST and T % PQB == 0 and T % GDN_CHUNK == 0 and DT >= CONV_WIDTH - 1
    n_pool = cache_k.shape[1]
    past = page_table.shape[1] * PAGE_SIZE
    RS = DB * DT
    M = T + RS
    tm = _row_tile(M)
    kw, vw = GDN_HEADS * GDN_DK, GDN_HEADS * GDN_DV
    ch = 2 * kw + vw
    aw, kvw, iw = N_HEADS * HEAD_DIM, N_KV_HEADS * HEAD_DIM, N_IDX_HEADS * IDX_DIM

    w = w_in[0]
    o_ki = aw + 2 * kvw + iw
    o_wi = o_ki + IDX_DIM
    o_qkv = o_wi + N_IDX_HEADS
    o_ga = o_qkv + ch
    o_gz = o_ga + 2 * GDN_HEADS
    n_main = o_ki + ch + (vw + 2 * D)
    n_misc = IDX_DIM + N_IDX_HEADS + 2 * GDN_HEADS
    w_packed = jnp.concatenate([
        w[:, :o_ki], w[:, o_qkv:o_ga], w[:, o_gz:], w[:, o_ki:o_qkv], w[:, o_ga:o_gz],
        jnp.zeros((D, PROJ_TN - n_misc), w.dtype)], axis=1).astype(BF16)
    assert w_packed.shape[1] == n_main + PROJ_TN
    wa = w_attn_up[0].astype(BF16)
    wg = w_gdn_up[0].astype(BF16)
    wo = w_out[0].astype(BF16)
    dff = w_down.shape[1]
    ffp = -(-dff // FF_TILE) * FF_TILE
    zpad = jnp.zeros((D, ffp - dff), w_gate_up.dtype)
    w_gu = jnp.concatenate([w_gate_up[0][:, :dff], zpad, w_gate_up[0][:, dff:], zpad], axis=1).astype(BF16)
    w_dn = jnp.pad(w_down[0], ((0, ffp - dff), (0, 0))).astype(BF16)
    w_pg = w_ple_gate[0].astype(BF16)
    w_pe = w_ple[0].astype(BF16)
    alog_vec = jnp.zeros((1, LANES), F32).at[0, GA_LANE:GA_LANE + GDN_HEADS].set(a_log[0])
    dtb_vec = jnp.zeros((1, LANES), F32).at[0, GA_LANE:GA_LANE + GDN_HEADS].set(dt_bias[0])

    x = jnp.concatenate([x_prompt[0], x_sample.reshape(RS, D)], axis=0)
    p_all = jnp.concatenate([p_prompt[0, 0], p_sample[0].reshape(RS, -1)], axis=0).astype(BF16)

    q_hm, k_f, v_f, kv_bf, qi_hm, qkv, gates, misc = _proj(x, norm_mix[0], w_packed, tm)
    ki_f = misc[:, :IDX_DIM]
    wi_f = misc[:, IDX_DIM:IDX_DIM + N_IDX_HEADS]

    tprev, tdiag, tsamp = _bias_tables(rel_bias, PQB, past, past + PAGE_SIZE)
    attn_p = _prompt_attention(qi_hm, wi_f, ki_f.astype(BF16), q_hm, kv_bf, tprev, tdiag, T,
                               min(TOP_K_MAX, T // 4))
    attn_s = _sample_attention(
        page_table, qi_hm[:, T:].astype(F32), wi_f[T:], q_hm[:, T:].astype(F32), ki_f[T:], k_f[T:], v_f[T:], tsamp,
        cache_idx_k[0], cache_k[0], cache_v[0],
        min(TOP_K_MAX, (past + DT) // 4))
    attn = jnp.concatenate([attn_p, attn_s.astype(BF16)], axis=0)

    o_p, ssm_p = _gdn(qkv, misc, gates, conv_w[0], alog_vec, dtb_vec, gdn_norm[0],
                      jnp.zeros((1, HALO, ch), F32), jnp.zeros((1, GDN_HEADS, GDN_DK, GDN_DV), F32),
                      n_seq=1, n_chunks=T // GDN_CHUNK, C=GDN_CHUNK, valid_rows=GDN_CHUNK, out_dtype=BF16)
    halo_s = jnp.pad(state_conv[0], ((0, 0), (HALO - (CONV_WIDTH - 1), 0), (0, 0)))
    o_s, ssm_s = _gdn(_pad_tokens(qkv[T:], DB, DT), _pad_tokens(misc[T:], DB, DT), _pad_tokens(gates[T:, :vw], DB, DT),
                      conv_w[0], alog_vec, dtb_vec, gdn_norm[0], halo_s, state_ssm[0],
                      n_seq=DB, n_chunks=1, C=SUBLANES, valid_rows=DT, out_dtype=F32)
    o_s = o_s.reshape(DB, SUBLANES, vw)[:, :DT].reshape(RS, vw)
    o_all = jnp.concatenate([o_p, o_s.astype(BF16)], axis=0)

    merged = _merge(attn, o_all, gates, wa, wg, tm)
    x1 = _resid_mm(x, merged, wo, tm, 512)
    act = _ffn_up(x1, norm_ffn[0], w_gu, tm, FF_TILE)
    x2 = _resid_mm(x1, act, w_dn, tm, 256)
    y = _ple_final(x2, norm_ple[0], w_pg, p_all, w_pe, norm_final, tm)

    def heads(a, n):
        return a.reshape(a.shape[0], n, a.shape[1] // n)

    nc = CONV_WIDTH - 1
    return (
        y[:T].reshape(1, T, D),
        y[T:].reshape(DB, DT, D),
        heads(k_f[:T], N_KV_HEADS)[None, None],
        heads(v_f[:T], N_KV_HEADS)[None, None],
        ki_f[:T][None, None],
        qkv[T - nc:T][None, None],
        ssm_p.astype(state_ssm.dtype)[None],
        heads(k_f[T:], N_KV_HEADS).reshape(1, DB, DT, N_KV_HEADS, HEAD_DIM),
        heads(v_f[T:], N_KV_HEADS).reshape(1, DB, DT, N_KV_HEADS, HEAD_DIM),
        ki_f[T:].reshape(1, DB, DT, IDX_DIM),
        qkv[T:].reshape(DB, DT, ch)[:, DT - nc:][None],
        ssm_s.astype(state_ssm.dtype)[None],
    )
```

```python
import functools
import math

import numpy as np
import jax
import jax.numpy as jnp
from jax import lax
from jax.experimental import pallas as pl
from jax.experimental.pallas import tpu as pltpu

F32 = jnp.float32
BF16 = jnp.bfloat16

N_HEADS = 16
N_KV_HEADS = 4
HEAD_DIM = 128
GROUPS = N_HEADS // N_KV_HEADS
N_IDX_HEADS = 32
IDX_DIM = 128
TOP_K_MAX = 256
PAGE_SIZE = 128
N_BUCKETS = 32
MAX_DISTANCE = 128
GDN_HEADS = 16
GDN_DK = 128
GDN_DV = 128
CONV_WIDTH = 4
GDN_CHUNK = 64
EPS = 1e-6
NEG = -1e30
LOG2E = math.log2(math.e)

LANES = 128
SUBLANES = 8
VMEM_LIMIT = 56 * 1024 * 1024


def _cparams(*sem):
    return pltpu.CompilerParams(dimension_semantics=sem, vmem_limit_bytes=VMEM_LIMIT)


def _rms_rows(x_ref, nw_ref, h_ref, rows):
    tm = x_ref.shape[0]
    rows = math.gcd(rows, tm)
    nw = nw_ref[...]

    def body(r, _):
        sl = pl.ds(pl.multiple_of(r * rows, rows), rows)
        x = x_ref[sl, :]
        ms = jnp.mean(x * x, axis=-1, keepdims=True)
        h_ref[sl, :] = (x * lax.rsqrt(ms + EPS) * nw).astype(h_ref.dtype)
        return 0

    lax.fori_loop(0, tm // rows, body, 0)


PROJ_TN = 256


def _rmsnorm_body(x_ref, nw_ref, h_ref):
    _rms_rows(x_ref, nw_ref, h_ref, 64)


def _rmsnorm_bf16(x, norm_w, tm):
    M, D = x.shape
    return pl.pallas_call(
        _rmsnorm_body,
        grid=(M // tm,),
        in_specs=[pl.BlockSpec((tm, D), lambda i: (i, 0)), pl.BlockSpec((1, D), lambda i: (0, 0))],
        out_specs=pl.BlockSpec((tm, D), lambda i: (i, 0)),
        out_shape=jax.ShapeDtypeStruct((M, D), BF16),
        compiler_params=_cparams("parallel"),
        name="rmsnorm",
    )(x, norm_w.reshape(1, D))


def _proj_body(h_ref, wn_ref, wp_ref, qqi_ref, kv_ref, kvb_ref, qkv_ref, gates_ref, misc_ref, *, steps):
    j = pl.program_id(1)
    hpt = PROJ_TN // HEAD_DIM

    def on(name):
        lo, n = steps[name]
        return pl.when((j >= lo) & (j < lo + n))

    def native():
        return jnp.dot(h_ref[...], wn_ref[...].astype(BF16), preferred_element_type=F32)

    def packed():
        return jnp.dot(h_ref[...], wp_ref[...], preferred_element_type=F32)

    def heads_out():
        acc = native()
        for hh in range(hpt):
            qqi_ref[hh] = acc[:, hh * HEAD_DIM:(hh + 1) * HEAD_DIM].astype(qqi_ref.dtype)

    on("q")(heads_out)
    on("qi")(heads_out)

    @on("kv")
    def _():
        acc = native()
        kv_ref[...] = acc
        kvb_ref[...] = acc.astype(kvb_ref.dtype)

    @on("misc")
    def _():
        misc_ref[...] = native()

    @on("qkv")
    def _():
        qkv_ref[...] = packed()

    @on("gates")
    def _():
        gates_ref[...] = packed()


def _proj(h, w_in, w_packed, tm):
    M, D = h.shape
    tn = PROJ_TN
    aw, kvw, iw = N_HEADS * HEAD_DIM, N_KV_HEADS * HEAD_DIM, N_IDX_HEADS * IDX_DIM
    o_ki = aw + 2 * kvw + iw
    o_ga = o_ki + IDX_DIM + N_IDX_HEADS + (2 * GDN_HEADS * GDN_DK + GDN_HEADS * GDN_DV)
    assert o_ki % tn == 0 and o_ga % tn == LANES + GA_LANE and GB_LANE == GA_LANE + GDN_HEADS
    n_pk = w_packed.shape[1] // tn
    n_qkv = (2 * GDN_HEADS * GDN_DK + GDN_HEADS * GDN_DV) // tn
    steps, lo = {}, 0
    for name, n in (("q", aw // tn), ("kv", 2 * kvw // tn), ("qi", iw // tn), ("misc", 2),
                    ("qkv", n_qkv), ("gates", n_pk - n_qkv)):
        steps[name] = (lo, n)
        lo += n
    nj = lo
    n_nat = steps["misc"][0]
    t_ki, t_ga = o_ki // tn, o_ga // tn
    assert t_ki == n_nat

    def native_map(i, j):
        return (0, jnp.where(j <= n_nat, jnp.minimum(j, t_ki), t_ga))

    def col(name):
        a, n = steps[name]
        return lambda i, j: (i, jnp.clip(j - a, 0, n - 1))

    hpt = tn // HEAD_DIM
    n_qi_blk, n_q_blk = N_IDX_HEADS // hpt, N_HEADS // hpt

    def heads_map(i, j):
        q_blk = n_qi_blk + jnp.clip(j - steps["q"][0], 0, n_q_blk - 1)
        qi_blk = jnp.clip(j - steps["qi"][0], 0, n_qi_blk - 1)
        return (jnp.where(j < steps["qi"][0], q_blk, qi_blk), i, 0)

    out_shape = (
        jax.ShapeDtypeStruct((N_IDX_HEADS + N_HEADS, M, HEAD_DIM), BF16),
        jax.ShapeDtypeStruct((M, 2 * kvw), F32),
        jax.ShapeDtypeStruct((M, 2 * kvw), BF16),
        jax.ShapeDtypeStruct((M, n_qkv * tn), F32),
        jax.ShapeDtypeStruct((M, (n_pk - n_qkv) * tn), F32),
        jax.ShapeDtypeStruct((M, 2 * tn), F32),
    )
    out_specs = (
        pl.BlockSpec((hpt, tm, HEAD_DIM), heads_map),
        pl.BlockSpec((tm, tn), col("kv")),
        pl.BlockSpec((tm, tn), col("kv")),
        pl.BlockSpec((tm, tn), col("qkv")),
        pl.BlockSpec((tm, tn), col("gates")),
        pl.BlockSpec((tm, tn), col("misc")),
    )
    return pl.pallas_call(
        functools.partial(_proj_body, steps=steps),
        grid=(M // tm, nj),
        in_specs=[
            pl.BlockSpec((tm, D), lambda i, j: (i, 0), pipeline_mode=pl.Buffered(1)),
            pl.BlockSpec((D, tn), native_map),
            pl.BlockSpec((D, tn), lambda i, j: (0, jnp.clip(j - steps["qkv"][0], 0, n_pk - 1))),
        ],
        out_specs=out_specs,
        out_shape=out_shape,
        compiler_params=_cparams("parallel", "arbitrary"),
        name="in_proj",
    )(h, w_in, w_packed)


def _merge_body(attn_ref, o_ref, ga_ref, gb_ref, wa_ref, wg_ref, out_ref):
    a = jnp.dot(attn_ref[...], wa_ref[...].astype(BF16), preferred_element_type=F32)
    b = jnp.dot(o_ref[...], wg_ref[...].astype(BF16), preferred_element_type=F32)
    out_ref[...] = (jax.nn.sigmoid(ga_ref[...]) * a + jax.nn.sigmoid(gb_ref[...]) * b).astype(out_ref.dtype)


def _merge(attn, o, gates, wa, wg, tm, tn=512):
    M, KA = attn.shape
    D = wa.shape[1]
    a_off = GDN_HEADS * GDN_DV // tn
    b_off = (GDN_HEADS * GDN_DV + D) // tn
    return pl.pallas_call(
        _merge_body,
        grid=(M // tm, D // tn),
        in_specs=[
            pl.BlockSpec((tm, KA), lambda i, j: (i, 0), pipeline_mode=pl.Buffered(1)),
            pl.BlockSpec((tm, KA), lambda i, j: (i, 0), pipeline_mode=pl.Buffered(1)),
            pl.BlockSpec((tm, tn), lambda i, j: (i, a_off + j)),
            pl.BlockSpec((tm, tn), lambda i, j: (i, b_off + j)),
            pl.BlockSpec((KA, tn), lambda i, j: (0, j)),
            pl.BlockSpec((KA, tn), lambda i, j: (0, j)),
        ],
        out_specs=pl.BlockSpec((tm, tn), lambda i, j: (i, j)),
        out_shape=jax.ShapeDtypeStruct((M, D), BF16),
        compiler_params=_cparams("parallel", "arbitrary"),
        name="merge",
    )(attn, o, gates, gates, wa, wg)


def _resid_mm_body(x_ref, a_ref, w_ref, out_ref):
    out_ref[...] = x_ref[...] + jnp.dot(a_ref[...], w_ref[...].astype(BF16), preferred_element_type=F32)


def _resid_mm(x, a, w, tm, tn):
    M, N = x.shape
    Kd = a.shape[1]
    return pl.pallas_call(
        _resid_mm_body,
        grid=(M // tm, N // tn),
        in_specs=[
            pl.BlockSpec((tm, tn), lambda i, j: (i, j)),
            pl.BlockSpec((tm, Kd), lambda i, j: (i, 0), pipeline_mode=pl.Buffered(1)),
            pl.BlockSpec((Kd, tn), lambda i, j: (0, j)),
        ],
        out_specs=pl.BlockSpec((tm, tn), lambda i, j: (i, j)),
        out_shape=jax.ShapeDtypeStruct((M, N), F32),
        compiler_params=_cparams("parallel", "arbitrary"),
        name="resid_mm",
    )(x, a, w)


def _ffn_up_body(x_ref, nw_ref, wg_ref, wu_ref, out_ref, h_ref):
    @pl.when(pl.program_id(1) == 0)
    def _():
        _rms_rows(x_ref, nw_ref, h_ref, 64)

    h = h_ref[...]
    g = jnp.dot(h, wg_ref[...].astype(BF16), preferred_element_type=F32)
    u = jnp.dot(h, wu_ref[...].astype(BF16), preferred_element_type=F32)
    out_ref[...] = (jax.nn.silu(g) * u).astype(out_ref.dtype)


def _ffn_up(x, norm_w, w_gu, tm, tn):
    M, D = x.shape
    ffp = w_gu.shape[1] // 2
    nj = ffp // tn
    return pl.pallas_call(
        _ffn_up_body,
        grid=(M // tm, nj),
        in_specs=[
            pl.BlockSpec((tm, D), lambda i, j: (i, 0), pipeline_mode=pl.Buffered(1)),
            pl.BlockSpec((1, D), lambda i, j: (0, 0)),
            pl.BlockSpec((D, tn), lambda i, j: (0, j)),
            pl.BlockSpec((D, tn), lambda i, j: (0, nj + j)),
        ],
        out_specs=pl.BlockSpec((tm, tn), lambda i, j: (i, j)),
        out_shape=jax.ShapeDtypeStruct((M, ffp), BF16),
        scratch_shapes=[pltpu.VMEM((tm, D), BF16)],
        compiler_params=_cparams("parallel", "arbitrary"),
        name="ffn_up",
    )(x, norm_w.reshape(1, D), w_gu, w_gu)


def _ple_body(x_ref, nw_ref, wg_ref, p_ref, wp_ref, nf_ref, y_ref, h_ref, *, tn):
    j = pl.program_id(1)

    @pl.when(j == 0)
    def _():
        _rms_rows(x_ref, nw_ref, h_ref, 64)

    g = jnp.dot(h_ref[...], wg_ref[...].astype(BF16), preferred_element_type=F32)
    e = jnp.dot(p_ref[...], wp_ref[...].astype(BF16), preferred_element_type=F32)
    cols = pl.ds(pl.multiple_of(j * tn, tn), tn)
    y_ref[:, cols] = x_ref[:, cols] + jax.nn.sigmoid(g) * e

    @pl.when(j == pl.num_programs(1) - 1)
    def _():
        _rms_rows(y_ref, nf_ref, y_ref, 64)


def _ple_final(x, norm_w, w_gate, p, w_ple, norm_final, tm, tn=256):
    M, D = x.shape
    P = p.shape[1]
    return pl.pallas_call(
        functools.partial(_ple_body, tn=tn),
        grid=(M // tm, D // tn),
        in_specs=[
            pl.BlockSpec((tm, D), lambda i, j: (i, 0), pipeline_mode=pl.Buffered(1)),
            pl.BlockSpec((1, D), lambda i, j: (0, 0)),
            pl.BlockSpec((D, tn), lambda i, j: (0, j)),
            pl.BlockSpec((tm, P), lambda i, j: (i, 0)),
            pl.BlockSpec((P, tn), lambda i, j: (0, j)),
            pl.BlockSpec((1, D), lambda i, j: (0, 0)),
        ],
        out_specs=pl.BlockSpec((tm, D), lambda i, j: (i, 0)),
        out_shape=jax.ShapeDtypeStruct((M, D), F32),
        scratch_shapes=[pltpu.VMEM((tm, D), BF16)],
        compiler_params=_cparams("parallel", "arbitrary"),
        name="ple_final",
    )(x, norm_w.reshape(1, D), w_gate, p, w_ple, norm_final.reshape(1, D))


def _bucket_thresholds():
    d = np.arange(0, 4 * MAX_DISTANCE)
    max_exact = N_BUCKETS // 2
    large = max_exact + (np.log(np.maximum(d, 1) / max_exact) / math.log(MAX_DISTANCE / max_exact)
                         * (N_BUCKETS - max_exact)).astype(np.int32)
    b = np.where(d < max_exact, d, np.minimum(large, N_BUCKETS - 1))
    return [int(np.argmax(b >= k)) for k in range(N_BUCKETS)]


_BUCKET_THR = _bucket_thresholds()


def _bias_of_dist(rb_ref, h, d):
    v = jnp.full(d.shape, rb_ref[0, h], F32)
    for b in range(1, N_BUCKETS):
        v = jnp.where(d >= _BUCKET_THR[b], rb_ref[b, h], v)
    return v


def _bias_tables_body(rb_ref, tprev_ref, tdiag_ref, tsamp_ref, *, qb, past):
    h = pl.program_id(0)
    r = lax.broadcasted_iota(jnp.int32, (qb, qb), 0)
    c = lax.broadcasted_iota(jnp.int32, (qb, qb), 1)
    far = rb_ref[N_BUCKETS - 1, h]
    tprev_ref[0] = (_bias_of_dist(rb_ref, h, r + qb - c) - far) * LOG2E
    tdiag_ref[0] = (_bias_of_dist(rb_ref, h, r - c) - far) * LOG2E
    ls = tsamp_ref.shape[2]
    t = lax.broadcasted_iota(jnp.int32, (SUBLANES, ls), 0) % 4
    lane = lax.broadcasted_iota(jnp.int32, (SUBLANES, ls), 1)
    d = jnp.where(lane < past, past + t - lane, t - (lane - past) % 4)
    tsamp_ref[0] = _bias_of_dist(rb_ref, h, d)


def _bias_tables(rel_bias, qb, past, ls):
    return pl.pallas_call(
        functools.partial(_bias_tables_body, qb=qb, past=past),
        grid=(N_HEADS,),
        in_specs=[pl.BlockSpec(memory_space=pltpu.SMEM)],
        out_specs=(
            pl.BlockSpec((1, qb, qb), lambda h: (h, 0, 0)),
            pl.BlockSpec((1, qb, qb), lambda h: (h, 0, 0)),
            pl.BlockSpec((1, SUBLANES, ls), lambda h: (h, 0, 0)),
        ),
        out_shape=(
            jax.ShapeDtypeStruct((N_HEADS, qb, qb), F32),
            jax.ShapeDtypeStruct((N_HEADS, qb, qb), F32),
            jax.ShapeDtypeStruct((N_HEADS, SUBLANES, ls), F32),
        ),
        compiler_params=_cparams("arbitrary"),
        name="bias_tables",
    )(rel_bias)


INT_MIN = -2 ** 31


def _sort_key(score, valid):
    bits = lax.bitcast_convert_type(score, jnp.int32)
    key = jnp.where(bits < 0, bits ^ jnp.int32(0x7FFFFFFF), bits)
    return jnp.where(valid, key, jnp.int32(INT_MIN))


def _kth_largest(count_ge, rows, k):
    def step(b, t):
        cand = t + lax.shift_left(jnp.int32(1), jnp.asarray(31 - b, jnp.int32))
        n = count_ge(cand)
        return jnp.where(n >= k, cand, t)

    return lax.fori_loop(0, 32, step, jnp.full((rows, LANES), INT_MIN, jnp.int32))


I16_MIN, I16_MAX = -2 ** 15, 2 ** 15 - 1


def _kth_largest16(count_ge, rows, k):
    def step(b, t):
        cand = t + lax.shift_left(jnp.int32(1), jnp.asarray(15 - b, jnp.int32))
        n = count_ge(cand)
        return jnp.where(n >= k, cand, t)

    return lax.fori_loop(0, 16, step, jnp.full((rows, LANES), I16_MIN, jnp.int32))


PQB = 256
PKC = 256


def _pattn_body(qi_ref, wi_ref, ki_ref, q_ref, k_ref, v_ref, tprev_ref, tdiag_ref, out_ref,
                hi_ref, lo_ref, wb_ref, m_ref, l_ref, acc_ref, *, idx_scale, n_sel):
    i = pl.program_id(0)
    qb, kc = PQB, PKC
    hg = 8

    wi = wi_ref[...] * idx_scale
    for h in range(N_IDX_HEADS):
        wb_ref[h] = jnp.broadcast_to(wi[:, h:h + 1], (qb, LANES))

    def chunk(c):
        return pl.ds(pl.multiple_of(c * kc, kc), kc)

    row = lax.broadcasted_iota(jnp.int32, (qb, kc), 0)
    col = lax.broadcasted_iota(jnp.int32, (qb, kc), 1)

    def score_chunk(c, _):
        kic = ki_ref[chunk(c), :]
        acc = [jnp.zeros((qb, LANES), F32) for _ in range(kc // LANES)]
        for g in range(N_IDX_HEADS // hg):
            qg = qi_ref[g * hg:(g + 1) * hg].reshape(hg * qb, IDX_DIM)
            s = lax.dot_general(qg, kic, (((1,), (1,)), ((), ())), preferred_element_type=F32)
            for hh in range(hg):
                w = wb_ref[g * hg + hh]
                for half in range(kc // LANES):
                    sh = s[hh * qb:(hh + 1) * qb, half * LANES:(half + 1) * LANES]
                    acc[half] = acc[half] + jnp.maximum(sh, 0.0) * w
        score = jnp.concatenate(acc, axis=1)
        valid = (c * kc + col) <= (i * qb + row)
        key = _sort_key(score, valid)
        hi_ref[:, chunk(c)] = lax.shift_right_arithmetic(key, 16).astype(jnp.int16)
        lo_ref[:, chunk(c)] = ((key & 0xFFFF) + I16_MIN).astype(jnp.int16)
        return 0

    lax.fori_loop(0, i + 1, score_chunk, 0)

    def dup(x):
        return jnp.concatenate([x] * (kc // LANES), axis=1)

    def counter(ref):
        def count_ge(cand):
            c16 = cand.astype(jnp.int16)

            def body(c, cnt):
                kk = ref[:, chunk(c)]
                for half in range(kc // LANES):
                    cnt = cnt + jnp.where(kk[:, half * LANES:(half + 1) * LANES] >= c16, jnp.int16(1), jnp.int16(0))
                return cnt

            cnt = lax.fori_loop(0, i + 1, body, jnp.zeros((qb, LANES), jnp.int16))
            return jnp.sum(cnt.astype(F32), axis=1, keepdims=True)
        return count_ge

    count_hi = counter(hi_ref)
    t_hi = _kth_largest16(count_hi, qb, float(n_sel))
    above = jnp.where(t_hi < I16_MAX, count_hi(jnp.minimum(t_hi + 1, I16_MAX)), 0.0)
    t_hi16 = dup(t_hi.astype(jnp.int16))

    def bucket_chunk(c, _):
        lo_ref[:, chunk(c)] = jnp.where(hi_ref[:, chunk(c)] == t_hi16, lo_ref[:, chunk(c)], jnp.int16(I16_MIN))
        return 0

    lax.fori_loop(0, i + 1, bucket_chunk, 0)
    t_lo = _kth_largest16(counter(lo_ref), qb, float(n_sel) - above)
    t_lo16 = dup(t_lo.astype(jnp.int16))

    def mask_chunk(c, _):
        hi = hi_ref[:, chunk(c)]
        lo = lo_ref[:, chunk(c)]
        sel = (hi > t_hi16) | ((hi == t_hi16) & (lo >= t_lo16) & ((hi > I16_MIN) | (lo > I16_MIN)))
        mb16 = jnp.where(sel, jnp.asarray(0.0, BF16), jnp.asarray(NEG, BF16))
        hi_ref[:, chunk(c)] = lax.bitcast_convert_type(mb16, jnp.int16)
        return 0

    lax.fori_loop(0, i + 1, mask_chunk, 0)

    c1 = HEAD_DIM ** -0.5 * LOG2E

    def kv_head(kh, _):
        qs = q_ref[pl.ds(kh * GROUPS, GROUPS)].reshape(GROUPS * qb, HEAD_DIM)
        lanes = pl.ds(pl.multiple_of(kh * HEAD_DIM, HEAD_DIM), HEAD_DIM)
        m_ref[...] = jnp.full(m_ref.shape, NEG, F32)
        l_ref[...] = jnp.zeros(l_ref.shape, F32)
        acc_ref[...] = jnp.zeros(acc_ref.shape, F32)

        def attend(c, bias_of_group):
            kt = k_ref[chunk(c), lanes]
            vt = v_ref[chunk(c), lanes]
            vx = jnp.concatenate([vt, jnp.ones_like(vt)], axis=1)
            s = lax.dot_general(qs, kt, (((1,), (1,)), ((), ())), preferred_element_type=F32)
            mb = lax.bitcast_convert_type(hi_ref[:, chunk(c)], BF16).astype(F32)
            ts = []
            for g in range(GROUPS):
                t = s[g * qb:(g + 1) * qb] * c1 + mb
                if bias_of_group is not None:
                    t = t + bias_of_group(g)
                ts.append(t)
            t = jnp.concatenate(ts, axis=0)
            m_old = m_ref[...]
            m_new = jnp.maximum(m_old, jnp.max(t, axis=1, keepdims=True))
            alpha = jnp.exp2(m_old - m_new)
            p = jnp.exp2(t - jnp.concatenate([m_new] * (kc // LANES), axis=1))
            pv = jnp.dot(p.astype(BF16), vx, preferred_element_type=F32)
            acc_ref[...] = alpha * acc_ref[...] + pv[:, :HEAD_DIM]
            l_ref[...] = alpha * l_ref[...] + pv[:, HEAD_DIM:]
            m_ref[...] = m_new

        def far(c, _):
            attend(c, None)
            return 0

        lax.fori_loop(0, jnp.maximum(i - 1, 0), far, 0)

        @pl.when(i >= 1)
        def _():
            attend(i - 1, lambda g: tprev_ref[kh * GROUPS + g])

        attend(i, lambda g: tdiag_ref[kh * GROUPS + g])

        o = acc_ref[...] / l_ref[...]
        for g in range(GROUPS):
            out_ref[:, pl.ds(pl.multiple_of((kh * GROUPS + g) * HEAD_DIM, HEAD_DIM), HEAD_DIM)] = (
                o[g * qb:(g + 1) * qb].astype(out_ref.dtype))
        return 0

    lax.fori_loop(0, N_KV_HEADS, kv_head, 0)


def _prompt_attention(qqi, wi, ki_bf, kv_bf, tprev, tdiag, seq, n_sel):
    qb = PQB
    assert N_IDX_HEADS % N_HEADS == 0
    resident = dict(pipeline_mode=pl.Buffered(1))
    return pl.pallas_call(
        functools.partial(_pattn_body, idx_scale=(N_IDX_HEADS * IDX_DIM) ** -0.5, n_sel=n_sel),
        grid=(seq // qb,),
        in_specs=[
            pl.BlockSpec((N_IDX_HEADS, qb, IDX_DIM), lambda i: (0, i, 0)),
            pl.BlockSpec((qb, N_IDX_HEADS), lambda i: (i, 0)),
            pl.BlockSpec((seq, IDX_DIM), lambda i: (0, 0), **resident),
            pl.BlockSpec((N_HEADS, qb, HEAD_DIM), lambda i: (N_IDX_HEADS // N_HEADS, i, 0)),
            pl.BlockSpec((seq, N_KV_HEADS * HEAD_DIM), lambda i: (0, 0), **resident),
            pl.BlockSpec((seq, N_KV_HEADS * HEAD_DIM), lambda i: (0, 1), **resident),
            pl.BlockSpec((N_HEADS, qb, qb), lambda i: (0, 0, 0), **resident),
            pl.BlockSpec((N_HEADS, qb, qb), lambda i: (0, 0, 0), **resident),
        ],
        out_specs=pl.BlockSpec((qb, N_HEADS * HEAD_DIM), lambda i: (i, 0)),
        out_shape=jax.ShapeDtypeStruct((seq, N_HEADS * HEAD_DIM), BF16),
        scratch_shapes=[
            pltpu.VMEM((qb, seq), jnp.int16),
            pltpu.VMEM((qb, seq), jnp.int16),
            pltpu.VMEM((N_IDX_HEADS, qb, LANES), F32),
            pltpu.VMEM((GROUPS * qb, LANES), F32),
            pltpu.VMEM((GROUPS * qb, LANES), F32),
            pltpu.VMEM((GROUPS * qb, HEAD_DIM), F32),
        ],
        compiler_params=_cparams("arbitrary"),
        name="prompt_attention",
    )(qqi, wi, ki_bf, qqi, kv_bf, kv_bf, tprev, tdiag)


GA_LANE = 32
GB_LANE = 48
HALO = SUBLANES
HI = lax.Precision.HIGHEST


def _mm(a, b):
    return jnp.dot(a, b, precision=HI, preferred_element_type=F32)


def _mm_nt(a, b):
    return lax.dot_general(a, b, (((1,), (1,)), ((), ())), precision=HI, preferred_element_type=F32)


def _bmm(a, b):
    return jnp.dot(a.astype(BF16), b.astype(BF16), preferred_element_type=F32)


def _bmm_nt(a, b):
    return lax.dot_general(a.astype(BF16), b.astype(BF16), (((1,), (1,)), ((), ())), preferred_element_type=F32)


def _bmm_tn(a, b):
    return lax.dot_general(a.astype(BF16), b.astype(BF16), (((0,), (0,)), ((), ())), preferred_element_type=F32)


def _gdn_body(qkv_ref, slab_ref, gz_ref, cw_ref, alog_ref, dtb_ref, gn_ref, halo0_ref, s0_ref,
              o_ref, s_ref, xp_ref, *, C, valid_rows):
    c = pl.program_id(1)
    kw, vw = GDN_HEADS * GDN_DK, GDN_HEADS * GDN_DV

    @pl.when(c == 0)
    def _():
        xp_ref[0:HALO] = halo0_ref[0]
        s_ref[...] = s0_ref[...]

    xp_ref[HALO:HALO + C] = qkv_ref[...]

    ri = lax.broadcasted_iota(jnp.int32, (C, C), 0)
    ci = lax.broadcasted_iota(jnp.int32, (C, C), 1)
    causal = ri >= ci
    strict = ri > ci
    eye = jnp.where(ri == ci, 1.0, 0.0).astype(F32)
    ltri = jnp.where(causal, 1.0, 0.0).astype(F32)

    slab = slab_ref[...]
    live = lax.broadcasted_iota(jnp.int32, slab.shape, 0) < valid_rows
    g_all = jnp.where(live, -jnp.exp(alog_ref[...]) * jax.nn.softplus(slab + dtb_ref[...]), 0.0)
    beta_all = jnp.where(live, jax.nn.sigmoid(slab), 0.0)
    gc_all = _mm(ltri, g_all)
    pad = jnp.zeros((LANES - C, LANES), F32)
    gc_t = jnp.concatenate([gc_all, pad], axis=0).T

    def conv(cols):
        y = cw_ref[0:1, cols] * xp_ref[HALO - 3:HALO - 3 + C, cols]
        for j in range(1, CONV_WIDTH):
            y = y + cw_ref[j:j + 1, cols] * xp_ref[HALO - 3 + j:HALO - 3 + j + C, cols]
        return jax.nn.silu(y)

    def l2n(x):
        return x * lax.rsqrt(jnp.sum(x * x, axis=-1, keepdims=True) + 1e-6)

    n_sq = max(1, (C - 1).bit_length() - 1)

    hs = range(GDN_HEADS)
    q = [l2n(conv(slice(h * GDN_DK, (h + 1) * GDN_DK))) * GDN_DK ** -0.5 for h in hs]
    k = [l2n(conv(slice(kw + h * GDN_DK, kw + (h + 1) * GDN_DK))) for h in hs]
    v = [conv(slice(2 * kw + h * GDN_DV, 2 * kw + (h + 1) * GDN_DV)) for h in hs]
    beta = [beta_all[:, GB_LANE + h:GB_LANE + h + 1] for h in hs]
    gc = [gc_all[:, GA_LANE + h:GA_LANE + h + 1] for h in hs]
    gc_row = [gc_t[GA_LANE + h:GA_LANE + h + 1, 0:C] for h in hs]
    gc_last = [gc_all[C - 1:C, GA_LANE + h:GA_LANE + h + 1] for h in hs]
    decay = [jnp.exp(jnp.where(causal, gc[h] - gc_row[h], NEG)) for h in hs]
    kb = [k[h] * beta[h] for h in hs]
    p = [-jnp.where(strict, _bmm_nt(kb[h], k[h]) * decay[h], 0.0) for h in hs]
    t_inv = [eye + p[h] for h in hs]
    for _ in range(n_sq):
        p = [_mm(x, x) for x in p]
        t_inv = [t + _mm(t, x) for t, x in zip(t_inv, p)]
    egc = [jnp.exp(gc[h]) for h in hs]
    u = [_bmm(t_inv[h], v[h] * beta[h]) for h in hs]
    w = [_bmm(t_inv[h], kb[h] * egc[h]) for h in hs]
    a_in = [jnp.where(causal, _bmm_nt(q[h], k[h]) * decay[h], 0.0) for h in hs]
    qd = [q[h] * egc[h] for h in hs]
    kt = [k[h] * jnp.exp(gc_last[h] - gc[h]) for h in hs]
    s_old = [s_ref[0, h] for h in hs]
    v_new = [u[h] - _bmm(w[h], s_old[h]) for h in hs]
    o = [_bmm(qd[h], s_old[h]) + _bmm(a_in[h], v_new[h]) for h in hs]
    for h in hs:
        s_ref[0, h] = s_old[h] * jnp.exp(gc_last[h]) + _bmm_tn(kt[h], v_new[h])
    for h in hs:
        on = o[h] * lax.rsqrt(jnp.mean(o[h] * o[h], axis=-1, keepdims=True) + EPS) * gn_ref[...]
        gz = gz_ref[:, h * GDN_DV:(h + 1) * GDN_DV]
        o_ref[:, h * GDN_DV:(h + 1) * GDN_DV] = (on * jax.nn.silu(gz)).astype(o_ref.dtype)

    xp_ref[0:HALO] = xp_ref[C:C + HALO]


def _gdn(qkv, misc, gates, conv_w, alog_vec, dtb_vec, gdn_norm, halo0, s0, *, n_seq, n_chunks, C, valid_rows,
         out_dtype):
    ch = qkv.shape[1]
    vw = GDN_HEADS * GDN_DV
    rows = n_seq * n_chunks * C

    def rmap(s, c):
        return (s * n_chunks + c, 0)

    return pl.pallas_call(
        functools.partial(_gdn_body, C=C, valid_rows=valid_rows),
        grid=(n_seq, n_chunks),
        in_specs=[
            pl.BlockSpec((C, ch), rmap),
            pl.BlockSpec((C, LANES), lambda s, c: (s * n_chunks + c, misc.shape[1] // LANES - 1)),
            pl.BlockSpec((C, vw), rmap),
            pl.BlockSpec((CONV_WIDTH, ch), lambda s, c: (0, 0)),
            pl.BlockSpec((1, LANES), lambda s, c: (0, 0)),
            pl.BlockSpec((1, LANES), lambda s, c: (0, 0)),
            pl.BlockSpec((1, GDN_DV), lambda s, c: (0, 0)),
            pl.BlockSpec((1, HALO, ch), lambda s, c: (s, 0, 0)),
            pl.BlockSpec((1, GDN_HEADS, GDN_DK, GDN_DV), lambda s, c: (s, 0, 0, 0)),
        ],
        out_specs=(
            pl.BlockSpec((C, vw), rmap),
            pl.BlockSpec((1, GDN_HEADS, GDN_DK, GDN_DV), lambda s, c: (s, 0, 0, 0)),
        ),
        out_shape=(
            jax.ShapeDtypeStruct((rows, vw), out_dtype),
            jax.ShapeDtypeStruct((n_seq, GDN_HEADS, GDN_DK, GDN_DV), F32),
        ),
        scratch_shapes=[pltpu.VMEM((C + HALO, ch), F32)],
        compiler_params=_cparams("arbitrary", "arbitrary"),
        name=f"gdn_c{C}",
    )(qkv, misc, gates, conv_w, alog_vec, dtb_vec, gdn_norm.reshape(1, GDN_DV), halo0, s0)


SB = 2
ST = SUBLANES // SB


def _sattn_body(pt_ref, qi_ref, wi_ref, q_ref, kin_ref, kn_ref, vn_ref, tsamp_ref, cidx_hbm, ck_hbm, cv_hbm, out_ref,
                kibuf, kbuf, vbuf, sem, *, past, n_pages, n_sel, idx_scale):
    p = pl.program_id(0)
    n_steps = pl.num_programs(0)
    ls = kibuf.shape[1]
    rows = SB * ST

    def slot_of(step, bl):
        return (step % 2) * SB + bl

    def copies(step, bl):
        b = step * SB + bl
        sl = slot_of(step, bl)
        out = []
        for j in range(n_pages):
            pg = pt_ref[b, j]
            dst = pl.ds(j * PAGE_SIZE, PAGE_SIZE)
            out.append(pltpu.make_async_copy(cidx_hbm.at[pg], kibuf.at[sl, dst], sem.at[sl]))
            for kh in range(N_KV_HEADS):
                out.append(pltpu.make_async_copy(ck_hbm.at[pg, :, kh], kbuf.at[sl, kh, dst], sem.at[sl]))
                out.append(pltpu.make_async_copy(cv_hbm.at[pg, :, kh], vbuf.at[sl, kh, dst], sem.at[sl]))
        return out

    @pl.when(p == 0)
    def _():
        for bl in range(SB):
            for cp in copies(0, bl):
                cp.start()
        for sl in range(2 * SB):
            kibuf[sl, past:ls] = jnp.zeros((ls - past, IDX_DIM), F32)
            kbuf[sl, :, past:ls] = jnp.zeros((N_KV_HEADS, ls - past, HEAD_DIM), F32)
            vbuf[sl, :, past:ls] = jnp.zeros((N_KV_HEADS, ls - past, HEAD_DIM), F32)

    @pl.when(p + 1 < n_steps)
    def _():
        for bl in range(SB):
            for cp in copies(p + 1, bl):
                cp.start()

    slots = [slot_of(p, bl) for bl in range(SB)]
    bt = lax.broadcasted_iota(jnp.int32, (rows, ls), 0)
    lane = lax.broadcasted_iota(jnp.int32, (rows, ls), 1)
    r_new = lane - past
    valid = (lane < past) | ((r_new < rows) & (r_new // ST == bt // ST) & (r_new % ST <= bt % ST))

    for sl in slots:
        kibuf[sl, past:past + rows] = kin_ref[...]
        for kh in range(N_KV_HEADS):
            kbuf[sl, kh, past:past + rows] = kn_ref[:, kh * HEAD_DIM:(kh + 1) * HEAD_DIM]
            vbuf[sl, kh, past:past + rows] = vn_ref[:, kh * HEAD_DIM:(kh + 1) * HEAD_DIM]

    wi = wi_ref[...] * idx_scale
    qi = qi_ref[...].reshape(N_IDX_HEADS * rows, IDX_DIM).astype(BF16)
    key_chunks = [(s, min(512, ls - s)) for s in range(0, ls, 512)]

    def scores(bl):
        parts = []
        for s0, n in key_chunks:
            kic = kibuf[slots[bl], s0:s0 + n].astype(BF16)
            s = lax.dot_general(qi, kic, (((1,), (1,)), ((), ())), preferred_element_type=F32)
            acc = jnp.zeros((rows, n), F32)
            for h in range(N_IDX_HEADS):
                acc = acc + jnp.maximum(s[h * rows:(h + 1) * rows], 0.0) * wi[:, h:h + 1]
            parts.append(acc)
        return jnp.concatenate(parts, axis=1)

    sc = []
    for bl in range(SB):
        for cp in copies(p, bl):
            cp.wait()
        sc.append(scores(bl))
    score = sc[0]
    for bl in range(1, SB):
        score = jnp.where(bt // ST == bl, sc[bl], score)
    key = _sort_key(score, valid)

    def count_ge(cand):
        cnt = jnp.zeros((rows, LANES), F32)
        for t in range(ls // LANES):
            cnt = cnt + jnp.where(key[:, t * LANES:(t + 1) * LANES] >= cand, 1.0, 0.0)
        return jnp.sum(cnt, axis=1, keepdims=True)

    thr = _kth_largest(count_ge, rows, float(n_sel))
    thr_full = jnp.concatenate([thr] * (ls // LANES), axis=1)
    mb = jnp.where((key >= thr_full) & valid, 0.0, NEG).astype(F32)

    scale = HEAD_DIM ** -0.5
    row_b = lax.broadcasted_iota(jnp.int32, (GROUPS * rows, HEAD_DIM), 0) % rows // ST
    outs = [None] * N_KV_HEADS
    for bl in range(SB):
        for kh in range(N_KV_HEADS):
            qs = q_ref[kh * GROUPS:(kh + 1) * GROUPS].reshape(GROUPS * rows, HEAD_DIM).astype(BF16)
            kt = kbuf[slots[bl], kh].astype(BF16)
            vt = vbuf[slots[bl], kh].astype(BF16)
            s = lax.dot_general(qs, kt, (((1,), (1,)), ((), ())), preferred_element_type=F32) * scale
            s = (s.reshape(GROUPS, rows, ls) + tsamp_ref[kh * GROUPS:(kh + 1) * GROUPS] + mb[None]).reshape(GROUPS * rows, ls)
            m = jnp.max(s, axis=1, keepdims=True)
            e = jnp.exp(s - m)
            o = jnp.dot(e.astype(BF16), vt, preferred_element_type=F32) / jnp.sum(e, axis=1, keepdims=True)
            outs[kh] = o if bl == 0 else jnp.where(row_b == bl, o, outs[kh])

    for kh in range(N_KV_HEADS):
        for g in range(GROUPS):
            hd = kh * GROUPS + g
            out_ref[:, hd * HEAD_DIM:(hd + 1) * HEAD_DIM] = outs[kh][g * rows:(g + 1) * rows]


def _sample_attention(page_table, qi_s, wi_s, q_s, ki_new, k_new, v_new, tsamp, cache_idx, cache_k, cache_v, n_sel):
    R = wi_s.shape[0]
    n_b, n_pages = page_table.shape
    past = n_pages * PAGE_SIZE
    ls = past + PAGE_SIZE
    rows = SB * ST
    assert R == n_b * ST and n_b % SB == 0
    kvw = N_KV_HEADS * HEAD_DIM
    grid_spec = pltpu.PrefetchScalarGridSpec(
        num_scalar_prefetch=1,
        grid=(n_b // SB,),
        in_specs=[
            pl.BlockSpec((N_IDX_HEADS, rows, IDX_DIM), lambda p, pt: (0, p, 0)),
            pl.BlockSpec((rows, N_IDX_HEADS), lambda p, pt: (p, 0)),
            pl.BlockSpec((N_HEADS, rows, HEAD_DIM), lambda p, pt: (0, p, 0)),
            pl.BlockSpec((rows, IDX_DIM), lambda p, pt: (p, 0)),
            pl.BlockSpec((rows, kvw), lambda p, pt: (p, 0)),
            pl.BlockSpec((rows, kvw), lambda p, pt: (p, 0)),
            pl.BlockSpec((N_HEADS, SUBLANES, ls), lambda p, pt: (0, 0, 0)),
            pl.BlockSpec(memory_space=pl.ANY),
            pl.BlockSpec(memory_space=pl.ANY),
            pl.BlockSpec(memory_space=pl.ANY),
        ],
        out_specs=pl.BlockSpec((rows, N_HEADS * HEAD_DIM), lambda p, pt: (p, 0)),
        scratch_shapes=[
            pltpu.VMEM((2 * SB, ls, IDX_DIM), F32),
            pltpu.VMEM((2 * SB, N_KV_HEADS, ls, HEAD_DIM), F32),
            pltpu.VMEM((2 * SB, N_KV_HEADS, ls, HEAD_DIM), F32),
            pltpu.SemaphoreType.DMA((2 * SB,)),
        ],
    )
    return pl.pallas_call(
        functools.partial(_sattn_body, past=past, n_pages=n_pages, n_sel=n_sel,
                          idx_scale=(N_IDX_HEADS * IDX_DIM) ** -0.5),
        grid_spec=grid_spec,
        out_shape=jax.ShapeDtypeStruct((R, N_HEADS * HEAD_DIM), F32),
        compiler_params=_cparams("arbitrary"),
        name="sample_attention",
    )(page_table, qi_s, wi_s, q_s, ki_new, k_new, v_new, tsamp, cache_idx, cache_k, cache_v)


ROW_TILE_CAP = 1088
FF_TILE = 256


def _row_tile(m, cap=ROW_TILE_CAP):
    return max(t for t in range(16, cap + 1, 16) if m % t == 0)


def _pad_tokens(a, db, dt):
    a = a.reshape(db, dt, a.shape[-1])
    return jnp.pad(a, ((0, 0), (0, SUBLANES - dt), (0, 0))).reshape(db * SUBLANES, a.shape[-1])


def kernel(x_prompt, x_sample, cache_k, cache_v, cache_idx_k, state_conv, state_ssm, page_table, p_prompt, p_sample,
           rel_bias, norm_mix, w_in, conv_w, a_log, dt_bias, gdn_norm, w_attn_up, w_gdn_up, w_out, norm_ffn, w_gate_up,
           w_down, norm_ple, w_ple_gate, w_ple, norm_final):
    assert x_prompt.shape[0] == 1 and w_in.shape[0] == 1, "one prompt sequence, one layer"
    _, T, D = x_prompt.shape
    DB, DT, _ = x_sample.shape
    assert DT == ST and T % PQB == 0 and T % GDN_CHUNK == 0 and DT >= CONV_WIDTH - 1
    n_pool = cache_k.shape[1]
    past = page_table.shape[1] * PAGE_SIZE
    RS = DB * DT
    M = T + RS
    tm = _row_tile(M)
    kw, vw = GDN_HEADS * GDN_DK, GDN_HEADS * GDN_DV
    ch = 2 * kw + vw
    aw, kvw, iw = N_HEADS * HEAD_DIM, N_KV_HEADS * HEAD_DIM, N_IDX_HEADS * IDX_DIM

    w = w_in[0]
    o_qkv = aw + 2 * kvw + iw + IDX_DIM + N_IDX_HEADS
    o_gz = o_qkv + ch + 2 * GDN_HEADS
    w_packed = jnp.concatenate([w[:, o_qkv:o_qkv + ch], w[:, o_gz:]], axis=1).astype(BF16)
    w_dn = w_down[0].astype(BF16)
    alog_vec = jnp.zeros((1, LANES), F32).at[0, GA_LANE:GA_LANE + GDN_HEADS].set(a_log[0])
    dtb_vec = jnp.zeros((1, LANES), F32).at[0, GA_LANE:GA_LANE + GDN_HEADS].set(dt_bias[0])

    x = jnp.concatenate([x_prompt[0], x_sample.reshape(RS, D)], axis=0)
    p_all = jnp.concatenate([p_prompt[0, 0], p_sample[0].reshape(RS, -1)], axis=0).astype(BF16)

    h0 = _rmsnorm_bf16(x, norm_mix[0], _row_tile(M, 512))
    qqi, kv_f, kv_bf, qkv, gates, misc = _proj(h0, w, w_packed, tm)
    k_f, v_f = kv_f[:, :kvw], kv_f[:, kvw:]
    ki_f = misc[:, :IDX_DIM]
    wi_f = misc[:, IDX_DIM:IDX_DIM + N_IDX_HEADS]

    tprev, tdiag, tsamp = _bias_tables(rel_bias, PQB, past, past + PAGE_SIZE)
    attn_p = _prompt_attention(qqi, wi_f, ki_f.astype(BF16), kv_bf, tprev, tdiag, T, min(TOP_K_MAX, T // 4))
    qqi_s = qqi[:, T:].astype(F32)
    attn_s = _sample_attention(
        page_table, qqi_s[:N_IDX_HEADS], wi_f[T:], qqi_s[N_IDX_HEADS:], ki_f[T:], k_f[T:], v_f[T:], tsamp,
        cache_idx_k[0], cache_k[0], cache_v[0],
        min(TOP_K_MAX, (past + DT) // 4))
    attn = jnp.concatenate([attn_p, attn_s.astype(BF16)], axis=0)

    o_p, ssm_p = _gdn(qkv, misc, gates, conv_w[0], alog_vec, dtb_vec, gdn_norm[0],
                      jnp.zeros((1, HALO, ch), F32), jnp.zeros((1, GDN_HEADS, GDN_DK, GDN_DV), F32),
                      n_seq=1, n_chunks=T // GDN_CHUNK, C=GDN_CHUNK, valid_rows=GDN_CHUNK, out_dtype=BF16)
    halo_s = jnp.pad(state_conv[0], ((0, 0), (HALO - (CONV_WIDTH - 1), 0), (0, 0)))
    o_s, ssm_s = _gdn(_pad_tokens(qkv[T:], DB, DT), _pad_tokens(misc[T:], DB, DT), _pad_tokens(gates[T:, :vw], DB, DT),
                      conv_w[0], alog_vec, dtb_vec, gdn_norm[0], halo_s, state_ssm[0],
                      n_seq=DB, n_chunks=1, C=SUBLANES, valid_rows=DT, out_dtype=F32)
    o_s = o_s.reshape(DB, SUBLANES, vw)[:, :DT].reshape(RS, vw)
    o_all = jnp.concatenate([o_p, o_s.astype(BF16)], axis=0)

    merged = _merge(attn, o_all, gates, w_attn_up[0], w_gdn_up[0], tm)
    x1 = _resid_mm(x, merged, w_out[0], tm, 512)
    act = _ffn_up(x1, norm_ffn[0], w_gate_up[0], tm, FF_TILE)
    x2 = _resid_mm(x1, act, w_dn, tm, FF_TILE)
    y = _ple_final(x2, norm_ple[0], w_ple_gate[0], p_all, w_ple[0], norm_final, _row_tile(M, 512))

    def heads(a, n):
        return a.reshape(a.shape[0], n, a.shape[1] // n)

    nc = CONV_WIDTH - 1
    return (
        y[:T].reshape(1, T, D),
        y[T:].reshape(DB, DT, D),
        heads(k_f[:T], N_KV_HEADS)[None, None],
        heads(v_f[:T], N_KV_HEADS)[None, None],
        ki_f[:T][None, None],
        qkv[T - nc:T][None, None],
        ssm_p.astype(state_ssm.dtype)[None],
        heads(k_f[T:], N_KV_HEADS).reshape(1, DB, DT, N_KV_HEADS, HEAD_DIM),
        heads(v_f[T:], N_KV_HEADS).reshape(1, DB, DT, N_KV_HEADS, HEAD_DIM),
        ki_f[T:].reshape(1, DB, DT, IDX_DIM),
        qkv[T:].reshape(DB, DT, ch)[:, DT - nc:][None],
        ssm_s.astype(state_ssm.dtype)[None],
    )
```

```python
import functools
import math

import numpy as np
import jax
import jax.numpy as jnp
from jax import lax
from jax.experimental import pallas as pl
from jax.experimental.pallas import tpu as pltpu

F32 = jnp.float32
BF16 = jnp.bfloat16

N_HEADS = 16
N_KV_HEADS = 4
HEAD_DIM = 128
GROUPS = N_HEADS // N_KV_HEADS
N_IDX_HEADS = 32
IDX_DIM = 128
TOP_K_MAX = 256
PAGE_SIZE = 128
N_BUCKETS = 32
MAX_DISTANCE = 128
GDN_HEADS = 16
GDN_DK = 128
GDN_DV = 128
CONV_WIDTH = 4
GDN_CHUNK = 64
EPS = 1e-6
NEG = -1e30
LOG2E = math.log2(math.e)

LANES = 128
SUBLANES = 8
VMEM_LIMIT = 56 * 1024 * 1024


def _cparams(*sem):
    return pltpu.CompilerParams(dimension_semantics=sem, vmem_limit_bytes=VMEM_LIMIT)


def _rms_rows(x_ref, nw_ref, h_ref, rows):
    tm = x_ref.shape[0]
    rows = math.gcd(rows, tm)
    nw = nw_ref[...]

    def body(r, _):
        sl = pl.ds(pl.multiple_of(r * rows, rows), rows)
        x = x_ref[sl, :]
        ms = jnp.mean(x * x, axis=-1, keepdims=True)
        h_ref[sl, :] = (x * lax.rsqrt(ms + EPS) * nw).astype(h_ref.dtype)
        return 0

    lax.fori_loop(0, tm // rows, body, 0)


PROJ_TN = 512


def _rmsnorm_body(x_ref, nw_ref, h_ref):
    _rms_rows(x_ref, nw_ref, h_ref, 64)


def _rmsnorm_bf16(x, norm_w, tm):
    M, D = x.shape
    return pl.pallas_call(
        _rmsnorm_body,
        grid=(M // tm,),
        in_specs=[pl.BlockSpec((tm, D), lambda i: (i, 0)), pl.BlockSpec((1, D), lambda i: (0, 0))],
        out_specs=pl.BlockSpec((tm, D), lambda i: (i, 0)),
        out_shape=jax.ShapeDtypeStruct((M, D), BF16),
        compiler_params=_cparams("parallel"),
        name="rmsnorm",
    )(x, norm_w.reshape(1, D))


def _proj_body(h_ref, wt_ref, qqi_ref, kvb_ref, pf_ref, *, steps):
    j = pl.program_id(1)
    hpt = PROJ_TN // HEAD_DIM

    def within(name):
        lo, n = steps[name]
        return (j >= lo) & (j < lo + n)

    acc = lax.dot_general(h_ref[...], wt_ref[...].astype(BF16), (((1,), (1,)), ((), ())), preferred_element_type=F32)
    is_heads = within("q") | within("qi")

    @pl.when(is_heads)
    def _():
        for hh in range(hpt):
            qqi_ref[hh] = acc[:, hh * HEAD_DIM:(hh + 1) * HEAD_DIM].astype(qqi_ref.dtype)

    @pl.when(jnp.logical_not(is_heads))
    def _():
        pf_ref[...] = acc

    @pl.when(within("kv"))
    def _():
        kvb_ref[...] = acc.astype(kvb_ref.dtype)


class ProjLayout:
    def __init__(self, d_model):
        tn = PROJ_TN
        aw, kvw, iw = N_HEADS * HEAD_DIM, N_KV_HEADS * HEAD_DIM, N_IDX_HEADS * IDX_DIM
        ch = 2 * GDN_HEADS * GDN_DK + GDN_HEADS * GDN_DV
        gw = GDN_HEADS * GDN_DV + 2 * d_model
        o_ki = aw + 2 * kvw + iw
        o_qkv = o_ki + IDX_DIM + N_IDX_HEADS
        o_ga = o_qkv + ch
        o_gz = o_ga + 2 * GDN_HEADS
        assert GB_LANE == GA_LANE + GDN_HEADS
        groups = (("q", 0, aw), ("k", aw, kvw), ("v", aw + kvw, kvw), ("qi", aw + 2 * kvw, iw),
                  ("tile_a", o_ki, tn), ("tile_b", o_ga - LANES - GA_LANE, tn), ("qkv", o_qkv, ch), ("gates", o_gz, gw))
        self.steps, self.rows, lo = {}, {}, 0
        for name, start, width in groups:
            assert start % 16 == 0 and width % tn == 0
            self.steps[name] = (lo, width // tn)
            self.rows[name] = start
            lo += width // tn
        self.n_steps = lo
        self.ch, self.gw, self.kvw = ch, gw, kvw
        self.c_qkv, self.c_gz = 0, ch
        self.c_gate_a, self.c_gate_b = ch + GDN_HEADS * GDN_DV, ch + GDN_HEADS * GDN_DV + d_model
        self.c_k = ch + gw
        self.c_v = self.c_k + kvw
        self.c_ki = self.c_v + kvw
        self.c_wi = self.c_ki + IDX_DIM
        self.c_slab = self.c_ki + tn + LANES
        self.width = self.c_ki + 2 * tn


def _proj(h, w_t, tm):
    M, D = h.shape
    tn = PROJ_TN
    lay = ProjLayout(D)
    st = lay.steps
    assert st["v"][0] == st["k"][0] + st["k"][1] and st["tile_b"][0] == st["tile_a"][0] + 1
    assert st["gates"][0] == st["qkv"][0] + st["qkv"][1] and lay.c_gz == lay.ch and lay.kvw == tn

    def w_map(i, j):
        off = jnp.int32(0)
        for name, (lo, _) in st.items():
            off = jnp.where(j >= lo, lay.rows[name] + (j - lo) * tn, off)
        return (pl.multiple_of(off, 16), 0)

    hpt = tn // HEAD_DIM
    n_qi_blk, n_q_blk = N_IDX_HEADS // hpt, N_HEADS // hpt

    def heads_map(i, j):
        q_blk = n_qi_blk + jnp.clip(j - st["q"][0], 0, n_q_blk - 1)
        qi_blk = jnp.clip(j - st["qi"][0], 0, n_qi_blk - 1)
        return (jnp.where(j < st["qi"][0], q_blk, qi_blk), i, 0)

    def pf_map(i, j):
        blk = lay.c_k // tn + jnp.clip(j - st["k"][0], 0, 1)
        blk = jnp.where(j >= st["tile_a"][0], lay.c_ki // tn + j - st["tile_a"][0], blk)
        blk = jnp.where(j >= st["qkv"][0], j - st["qkv"][0], blk)
        return (i, blk)

    steps = dict(q=st["q"], qi=st["qi"], kv=(st["k"][0], 2))
    return pl.pallas_call(
        functools.partial(_proj_body, steps=steps),
        grid=(M // tm, lay.n_steps),
        in_specs=[
            pl.BlockSpec((tm, D), lambda i, j: (i, 0), pipeline_mode=pl.Buffered(1)),
            pl.BlockSpec((pl.Element(tn), pl.Element(D)), w_map),
        ],
        out_specs=(
            pl.BlockSpec((hpt, tm, HEAD_DIM), heads_map),
            pl.BlockSpec((tm, tn), lambda i, j: (i, jnp.clip(j - st["k"][0], 0, 1))),
            pl.BlockSpec((tm, tn), pf_map),
        ),
        out_shape=(
            jax.ShapeDtypeStruct((N_IDX_HEADS + N_HEADS, M, HEAD_DIM), BF16),
            jax.ShapeDtypeStruct((M, 2 * lay.kvw), BF16),
            jax.ShapeDtypeStruct((M, lay.width), F32),
        ),
        compiler_params=_cparams("parallel", "arbitrary"),
        name="in_proj",
    )(h, w_t)


def _merge_body(attn_ref, o_ref, ga_ref, gb_ref, wa_ref, wg_ref, out_ref):
    a = jnp.dot(attn_ref[...], wa_ref[...].astype(BF16), preferred_element_type=F32)
    b = jnp.dot(o_ref[...], wg_ref[...].astype(BF16), preferred_element_type=F32)
    out_ref[...] = (jax.nn.sigmoid(ga_ref[...]) * a + jax.nn.sigmoid(gb_ref[...]) * b).astype(out_ref.dtype)


def _merge(attn, o, gates, wa, wg, tm, col_a, col_b, tn=512):
    M, KA = attn.shape
    D = wa.shape[1]
    assert col_a % tn == 0 and col_b % tn == 0
    a_off, b_off = col_a // tn, col_b // tn
    return pl.pallas_call(
        _merge_body,
        grid=(M // tm, D // tn),
        in_specs=[
            pl.BlockSpec((tm, KA), lambda i, j: (i, 0), pipeline_mode=pl.Buffered(1)),
            pl.BlockSpec((tm, KA), lambda i, j: (i, 0), pipeline_mode=pl.Buffered(1)),
            pl.BlockSpec((tm, tn), lambda i, j: (i, a_off + j)),
            pl.BlockSpec((tm, tn), lambda i, j: (i, b_off + j)),
            pl.BlockSpec((KA, tn), lambda i, j: (0, j)),
            pl.BlockSpec((KA, tn), lambda i, j: (0, j)),
        ],
        out_specs=pl.BlockSpec((tm, tn), lambda i, j: (i, j)),
        out_shape=jax.ShapeDtypeStruct((M, D), BF16),
        compiler_params=_cparams("parallel", "arbitrary"),
        name="merge",
    )(attn, o, gates, gates, wa, wg)


def _resid_mm_body(x_ref, a_ref, w_ref, out_ref):
    out_ref[...] = x_ref[...] + jnp.dot(a_ref[...], w_ref[...].astype(BF16), preferred_element_type=F32)


def _resid_mm(x, a, w, tm, tn):
    M, N = x.shape
    Kd = a.shape[1]
    return pl.pallas_call(
        _resid_mm_body,
        grid=(M // tm, N // tn),
        in_specs=[
            pl.BlockSpec((tm, tn), lambda i, j: (i, j)),
            pl.BlockSpec((tm, Kd), lambda i, j: (i, 0), pipeline_mode=pl.Buffered(1)),
            pl.BlockSpec((Kd, tn), lambda i, j: (0, j)),
        ],
        out_specs=pl.BlockSpec((tm, tn), lambda i, j: (i, j)),
        out_shape=jax.ShapeDtypeStruct((M, N), F32),
        compiler_params=_cparams("parallel", "arbitrary"),
        name="resid_mm",
    )(x, a, w)


def _ffn_up_body(x_ref, nw_ref, wg_ref, wu_ref, out_ref, h_ref):
    @pl.when(pl.program_id(1) == 0)
    def _():
        _rms_rows(x_ref, nw_ref, h_ref, 64)

    h = h_ref[...]
    g = jnp.dot(h, wg_ref[...].astype(BF16), preferred_element_type=F32)
    u = jnp.dot(h, wu_ref[...].astype(BF16), preferred_element_type=F32)
    out_ref[...] = (jax.nn.silu(g) * u).astype(out_ref.dtype)


def _ffn_up(x, norm_w, w_gu, tm, tn):
    M, D = x.shape
    ffp = w_gu.shape[1] // 2
    nj = ffp // tn
    return pl.pallas_call(
        _ffn_up_body,
        grid=(M // tm, nj),
        in_specs=[
            pl.BlockSpec((tm, D), lambda i, j: (i, 0), pipeline_mode=pl.Buffered(1)),
            pl.BlockSpec((1, D), lambda i, j: (0, 0)),
            pl.BlockSpec((D, tn), lambda i, j: (0, j)),
            pl.BlockSpec((D, tn), lambda i, j: (0, nj + j)),
        ],
        out_specs=pl.BlockSpec((tm, tn), lambda i, j: (i, j)),
        out_shape=jax.ShapeDtypeStruct((M, ffp), BF16),
        scratch_shapes=[pltpu.VMEM((tm, D), BF16)],
        compiler_params=_cparams("parallel", "arbitrary"),
        name="ffn_up",
    )(x, norm_w.reshape(1, D), w_gu, w_gu)


def _ple_body(x_ref, nw_ref, wg_ref, p_ref, wp_ref, nf_ref, y_ref, h_ref, *, tn):
    j = pl.program_id(1)

    @pl.when(j == 0)
    def _():
        _rms_rows(x_ref, nw_ref, h_ref, 64)

    g = jnp.dot(h_ref[...], wg_ref[...].astype(BF16), preferred_element_type=F32)
    e = jnp.dot(p_ref[...], wp_ref[...].astype(BF16), preferred_element_type=F32)
    cols = pl.ds(pl.multiple_of(j * tn, tn), tn)
    y_ref[:, cols] = x_ref[:, cols] + jax.nn.sigmoid(g) * e

    @pl.when(j == pl.num_programs(1) - 1)
    def _():
        _rms_rows(y_ref, nf_ref, y_ref, 64)


def _ple_final(x, norm_w, w_gate, p, w_ple, norm_final, row0, n_rows, tm, tn=512):
    D = x.shape[1]
    P = p.shape[1]
    assert row0 % tm == 0 and n_rows % tm == 0
    r0 = row0 // tm
    return pl.pallas_call(
        functools.partial(_ple_body, tn=tn),
        grid=(n_rows // tm, D // tn),
        in_specs=[
            pl.BlockSpec((tm, D), lambda i, j: (r0 + i, 0), pipeline_mode=pl.Buffered(1)),
            pl.BlockSpec((1, D), lambda i, j: (0, 0)),
            pl.BlockSpec((D, tn), lambda i, j: (0, j)),
            pl.BlockSpec((tm, P), lambda i, j: (r0 + i, 0)),
            pl.BlockSpec((P, tn), lambda i, j: (0, j)),
            pl.BlockSpec((1, D), lambda i, j: (0, 0)),
        ],
        out_specs=pl.BlockSpec((tm, D), lambda i, j: (i, 0)),
        out_shape=jax.ShapeDtypeStruct((n_rows, D), F32),
        scratch_shapes=[pltpu.VMEM((tm, D), BF16)],
        compiler_params=_cparams("parallel", "arbitrary"),
        name="ple_final",
    )(x, norm_w.reshape(1, D), w_gate, p, w_ple, norm_final.reshape(1, D))


def _bucket_thresholds():
    d = np.arange(0, 4 * MAX_DISTANCE)
    max_exact = N_BUCKETS // 2
    large = max_exact + (np.log(np.maximum(d, 1) / max_exact) / math.log(MAX_DISTANCE / max_exact)
                         * (N_BUCKETS - max_exact)).astype(np.int32)
    b = np.where(d < max_exact, d, np.minimum(large, N_BUCKETS - 1))
    return [int(np.argmax(b >= k)) for k in range(N_BUCKETS)]


_BUCKET_THR = _bucket_thresholds()


def _bias_of_dist(rb_ref, h, d):
    v = jnp.full(d.shape, rb_ref[0, h], F32)
    for b in range(1, N_BUCKETS):
        v = jnp.where(d >= _BUCKET_THR[b], rb_ref[b, h], v)
    return v


def _bias_tables_body(rb_ref, tprev_ref, tdiag_ref, tsamp_ref, *, qb, past):
    h = pl.program_id(0)
    r = lax.broadcasted_iota(jnp.int32, (qb, qb), 0)
    c = lax.broadcasted_iota(jnp.int32, (qb, qb), 1)
    far = rb_ref[N_BUCKETS - 1, h]
    tprev_ref[0] = (_bias_of_dist(rb_ref, h, r + qb - c) - far) * LOG2E
    tdiag_ref[0] = (_bias_of_dist(rb_ref, h, r - c) - far) * LOG2E
    ls = tsamp_ref.shape[2]
    t = lax.broadcasted_iota(jnp.int32, (SUBLANES, ls), 0) % 4
    lane = lax.broadcasted_iota(jnp.int32, (SUBLANES, ls), 1)
    d = jnp.where(lane < past, past + t - lane, t - (lane - past) % 4)
    tsamp_ref[0] = _bias_of_dist(rb_ref, h, d)


def _bias_tables(rel_bias, qb, past, ls):
    return pl.pallas_call(
        functools.partial(_bias_tables_body, qb=qb, past=past),
        grid=(N_HEADS,),
        in_specs=[pl.BlockSpec(memory_space=pltpu.SMEM)],
        out_specs=(
            pl.BlockSpec((1, qb, qb), lambda h: (h, 0, 0)),
            pl.BlockSpec((1, qb, qb), lambda h: (h, 0, 0)),
            pl.BlockSpec((1, SUBLANES, ls), lambda h: (h, 0, 0)),
        ),
        out_shape=(
            jax.ShapeDtypeStruct((N_HEADS, qb, qb), F32),
            jax.ShapeDtypeStruct((N_HEADS, qb, qb), F32),
            jax.ShapeDtypeStruct((N_HEADS, SUBLANES, ls), F32),
        ),
        compiler_params=_cparams("arbitrary"),
        name="bias_tables",
    )(rel_bias)


INT_MIN = -2 ** 31


def _sort_key(score, valid):
    bits = lax.bitcast_convert_type(score, jnp.int32)
    key = jnp.where(bits < 0, bits ^ jnp.int32(0x7FFFFFFF), bits)
    return jnp.where(valid, key, jnp.int32(INT_MIN))


def _kth_largest(count_ge, rows, k):
    def step(b, t):
        cand = t + lax.shift_left(jnp.int32(1), jnp.asarray(31 - b, jnp.int32))
        n = count_ge(cand)
        return jnp.where(n >= k, cand, t)

    return lax.fori_loop(0, 32, step, jnp.full((rows, LANES), INT_MIN, jnp.int32))


PQB = 256
PKC = 256


def _pattn_body(qi_ref, wi_ref, ki_ref, q_ref, k_ref, v_ref, tprev_ref, tdiag_ref, out_ref,
                key_ref, wb_ref, m_ref, l_ref, acc_ref, *, idx_scale, n_sel):
    i = pl.program_id(0)
    qb, kc = PQB, PKC
    hg = 8

    wi = wi_ref[...] * idx_scale
    for h in range(N_IDX_HEADS):
        wb_ref[h] = jnp.broadcast_to(wi[:, h:h + 1], (qb, LANES))

    def chunk(c):
        return pl.ds(pl.multiple_of(c * kc, kc), kc)

    row = lax.broadcasted_iota(jnp.int32, (qb, kc), 0)
    col = lax.broadcasted_iota(jnp.int32, (qb, kc), 1)

    def score_chunk(c, _):
        kic = ki_ref[chunk(c), :]
        acc = [jnp.zeros((qb, LANES), F32) for _ in range(kc // LANES)]
        for g in range(N_IDX_HEADS // hg):
            qg = qi_ref[g * hg:(g + 1) * hg].reshape(hg * qb, IDX_DIM)
            s = lax.dot_general(qg, kic, (((1,), (1,)), ((), ())), preferred_element_type=F32)
            for hh in range(hg):
                w = wb_ref[g * hg + hh]
                for half in range(kc // LANES):
                    sh = s[hh * qb:(hh + 1) * qb, half * LANES:(half + 1) * LANES]
                    acc[half] = acc[half] + jnp.maximum(sh, 0.0) * w
        score = jnp.concatenate(acc, axis=1)
        valid = (c * kc + col) <= (i * qb + row)
        key_ref[:, chunk(c)] = _sort_key(score, valid)
        return 0

    lax.fori_loop(0, i + 1, score_chunk, 0)

    def count_ge(cand):
        def body(c, cnt):
            kk = key_ref[:, chunk(c)]
            for half in range(kc // LANES):
                cnt = cnt + jnp.where(kk[:, half * LANES:(half + 1) * LANES] >= cand, 1.0, 0.0)
            return cnt

        cnt = lax.fori_loop(0, i + 1, body, jnp.zeros((qb, LANES), F32))
        return jnp.sum(cnt, axis=1, keepdims=True)

    thr = _kth_largest(count_ge, qb, float(n_sel))

    def mask_chunk(c, _):
        kk = key_ref[:, chunk(c)]
        t2 = jnp.concatenate([thr] * (kc // LANES), axis=1)
        sel = (kk >= t2) & (kk > jnp.int32(INT_MIN))
        key_ref[:, chunk(c)] = lax.bitcast_convert_type(jnp.where(sel, 0.0, NEG).astype(F32), jnp.int32)
        return 0

    lax.fori_loop(0, i + 1, mask_chunk, 0)

    c1 = HEAD_DIM ** -0.5 * LOG2E

    def kv_head(kh, _):
        qs = q_ref[pl.ds(kh * GROUPS, GROUPS)].reshape(GROUPS * qb, HEAD_DIM)
        lanes = pl.ds(pl.multiple_of(kh * HEAD_DIM, HEAD_DIM), HEAD_DIM)
        m_ref[...] = jnp.full(m_ref.shape, NEG, F32)
        l_ref[...] = jnp.zeros(l_ref.shape, F32)
        acc_ref[...] = jnp.zeros(acc_ref.shape, F32)

        def attend(c, bias_of_group):
            kt = k_ref[chunk(c), lanes]
            vt = v_ref[chunk(c), lanes]
            vx = jnp.concatenate([vt, jnp.ones_like(vt)], axis=1)
            s = lax.dot_general(qs, kt, (((1,), (1,)), ((), ())), preferred_element_type=F32)
            mb = lax.bitcast_convert_type(key_ref[:, chunk(c)], F32)
            ts = []
            for g in range(GROUPS):
                t = s[g * qb:(g + 1) * qb] * c1 + mb
                if bias_of_group is not None:
                    t = t + bias_of_group(g)
                ts.append(t)
            t = jnp.concatenate(ts, axis=0)
            m_old = m_ref[...]
            m_new = jnp.maximum(m_old, jnp.max(t, axis=1, keepdims=True))
            alpha = jnp.exp2(m_old - m_new)
            p = jnp.exp2(t - jnp.concatenate([m_new] * (kc // LANES), axis=1))
            pv = jnp.dot(p.astype(BF16), vx, preferred_element_type=F32)
            acc_ref[...] = alpha * acc_ref[...] + pv[:, :HEAD_DIM]
            l_ref[...] = alpha * l_ref[...] + pv[:, HEAD_DIM:]
            m_ref[...] = m_new

        def far(c, _):
            attend(c, None)
            return 0

        lax.fori_loop(0, jnp.maximum(i - 1, 0), far, 0)

        @pl.when(i >= 1)
        def _():
            attend(i - 1, lambda g: tprev_ref[kh * GROUPS + g])

        attend(i, lambda g: tdiag_ref[kh * GROUPS + g])

        o = acc_ref[...] / l_ref[...]
        for g in range(GROUPS):
            out_ref[:, pl.ds(pl.multiple_of((kh * GROUPS + g) * HEAD_DIM, HEAD_DIM), HEAD_DIM)] = (
                o[g * qb:(g + 1) * qb].astype(out_ref.dtype))
        return 0

    lax.fori_loop(0, N_KV_HEADS, kv_head, 0)


def _prompt_attention(qqi, wi, ki_bf, kv_bf, tprev, tdiag, seq, n_sel):
    qb = PQB
    assert N_IDX_HEADS % N_HEADS == 0
    resident = dict(pipeline_mode=pl.Buffered(1))
    return pl.pallas_call(
        functools.partial(_pattn_body, idx_scale=(N_IDX_HEADS * IDX_DIM) ** -0.5, n_sel=n_sel),
        grid=(seq // qb,),
        in_specs=[
            pl.BlockSpec((N_IDX_HEADS, qb, IDX_DIM), lambda i: (0, i, 0)),
            pl.BlockSpec((qb, N_IDX_HEADS), lambda i: (i, 0)),
            pl.BlockSpec((seq, IDX_DIM), lambda i: (0, 0), **resident),
            pl.BlockSpec((N_HEADS, qb, HEAD_DIM), lambda i: (N_IDX_HEADS // N_HEADS, i, 0)),
            pl.BlockSpec((seq, N_KV_HEADS * HEAD_DIM), lambda i: (0, 0), **resident),
            pl.BlockSpec((seq, N_KV_HEADS * HEAD_DIM), lambda i: (0, 1), **resident),
            pl.BlockSpec((N_HEADS, qb, qb), lambda i: (0, 0, 0), **resident),
            pl.BlockSpec((N_HEADS, qb, qb), lambda i: (0, 0, 0), **resident),
        ],
        out_specs=pl.BlockSpec((qb, N_HEADS * HEAD_DIM), lambda i: (i, 0)),
        out_shape=jax.ShapeDtypeStruct((seq, N_HEADS * HEAD_DIM), BF16),
        scratch_shapes=[
            pltpu.VMEM((qb, seq), jnp.int32),
            pltpu.VMEM((N_IDX_HEADS, qb, LANES), F32),
            pltpu.VMEM((GROUPS * qb, LANES), F32),
            pltpu.VMEM((GROUPS * qb, LANES), F32),
            pltpu.VMEM((GROUPS * qb, HEAD_DIM), F32),
        ],
        compiler_params=_cparams("arbitrary"),
        name="prompt_attention",
    )(qqi, wi, ki_bf, qqi, kv_bf, kv_bf, tprev, tdiag)


GA_LANE = 32
GB_LANE = 48
HALO = SUBLANES
HI = lax.Precision.HIGHEST


def _mm(a, b):
    return jnp.dot(a, b, precision=HI, preferred_element_type=F32)


def _mm_nt(a, b):
    return lax.dot_general(a, b, (((1,), (1,)), ((), ())), precision=HI, preferred_element_type=F32)


def _bmm(a, b):
    return jnp.dot(a.astype(BF16), b.astype(BF16), preferred_element_type=F32)


def _bmm_nt(a, b):
    return lax.dot_general(a.astype(BF16), b.astype(BF16), (((1,), (1,)), ((), ())), preferred_element_type=F32)


def _bmm_tn(a, b):
    return lax.dot_general(a.astype(BF16), b.astype(BF16), (((0,), (0,)), ((), ())), preferred_element_type=F32)


def _gdn_body(qkv_ref, slab_ref, gz_ref, cw_ref, alog_ref, dtb_ref, gn_ref, halo0_ref, s0_ref,
              o_ref, s_ref, xp_ref, *, C, valid_rows):
    c = pl.program_id(1)
    kw, vw = GDN_HEADS * GDN_DK, GDN_HEADS * GDN_DV

    @pl.when(c == 0)
    def _():
        xp_ref[0:HALO] = halo0_ref[0]
        s_ref[...] = s0_ref[...]

    xp_ref[HALO:HALO + C] = qkv_ref[...]

    ri = lax.broadcasted_iota(jnp.int32, (C, C), 0)
    ci = lax.broadcasted_iota(jnp.int32, (C, C), 1)
    causal = ri >= ci
    strict = ri > ci
    eye = jnp.where(ri == ci, 1.0, 0.0).astype(F32)
    ltri = jnp.where(causal, 1.0, 0.0).astype(F32)

    slab = slab_ref[...]
    live = lax.broadcasted_iota(jnp.int32, slab.shape, 0) < valid_rows
    g_all = jnp.where(live, -jnp.exp(alog_ref[...]) * jax.nn.softplus(slab + dtb_ref[...]), 0.0)
    beta_all = jnp.where(live, jax.nn.sigmoid(slab), 0.0)
    gc_all = _mm(ltri, g_all)
    pad = jnp.zeros((LANES - C, LANES), F32)
    gc_t = jnp.concatenate([gc_all, pad], axis=0).T

    def conv(cols):
        y = cw_ref[0:1, cols] * xp_ref[HALO - 3:HALO - 3 + C, cols]
        for j in range(1, CONV_WIDTH):
            y = y + cw_ref[j:j + 1, cols] * xp_ref[HALO - 3 + j:HALO - 3 + j + C, cols]
        return jax.nn.silu(y)

    def l2n(x):
        return x * lax.rsqrt(jnp.sum(x * x, axis=-1, keepdims=True) + 1e-6)

    n_sq = max(1, (C - 1).bit_length() - 1)

    hs = range(GDN_HEADS)
    q = [l2n(conv(slice(h * GDN_DK, (h + 1) * GDN_DK))) * GDN_DK ** -0.5 for h in hs]
    k = [l2n(conv(slice(kw + h * GDN_DK, kw + (h + 1) * GDN_DK))) for h in hs]
    v = [conv(slice(2 * kw + h * GDN_DV, 2 * kw + (h + 1) * GDN_DV)) for h in hs]
    beta = [beta_all[:, GB_LANE + h:GB_LANE + h + 1] for h in hs]
    gc = [gc_all[:, GA_LANE + h:GA_LANE + h + 1] for h in hs]
    gc_row = [gc_t[GA_LANE + h:GA_LANE + h + 1, 0:C] for h in hs]
    gc_last = [gc_all[C - 1:C, GA_LANE + h:GA_LANE + h + 1] for h in hs]
    decay = [jnp.exp(jnp.where(causal, gc[h] - gc_row[h], NEG)) for h in hs]
    kb = [k[h] * beta[h] for h in hs]
    p = [-jnp.where(strict, _bmm_nt(kb[h], k[h]) * decay[h], 0.0) for h in hs]
    t_inv = [eye + p[h] for h in hs]
    for _ in range(n_sq):
        p = [_mm(x, x) for x in p]
        t_inv = [t + _mm(t, x) for t, x in zip(t_inv, p)]
    egc = [jnp.exp(gc[h]) for h in hs]
    u = [_bmm(t_inv[h], v[h] * beta[h]) for h in hs]
    w = [_bmm(t_inv[h], kb[h] * egc[h]) for h in hs]
    a_in = [jnp.where(causal, _bmm_nt(q[h], k[h]) * decay[h], 0.0) for h in hs]
    qd = [q[h] * egc[h] for h in hs]
    kt = [k[h] * jnp.exp(gc_last[h] - gc[h]) for h in hs]
    s_old = [s_ref[0, h] for h in hs]
    v_new = [u[h] - _bmm(w[h], s_old[h]) for h in hs]
    o = [_bmm(qd[h], s_old[h]) + _bmm(a_in[h], v_new[h]) for h in hs]
    for h in hs:
        s_ref[0, h] = s_old[h] * jnp.exp(gc_last[h]) + _bmm_tn(kt[h], v_new[h])
    for h in hs:
        on = o[h] * lax.rsqrt(jnp.mean(o[h] * o[h], axis=-1, keepdims=True) + EPS) * gn_ref[...]
        gz = gz_ref[:, h * GDN_DV:(h + 1) * GDN_DV]
        o_ref[:, h * GDN_DV:(h + 1) * GDN_DV] = (on * jax.nn.silu(gz)).astype(o_ref.dtype)

    xp_ref[0:HALO] = xp_ref[C:C + HALO]


def _gdn(qkv, misc, gates, conv_w, alog_vec, dtb_vec, gdn_norm, halo0, s0, *, n_seq, n_chunks, C, valid_rows,
         out_dtype, col_qkv=0, col_slab=None, col_gz=0):
    ch = conv_w.shape[1]
    vw = GDN_HEADS * GDN_DV
    rows = n_seq * n_chunks * C
    col_slab = misc.shape[1] - LANES if col_slab is None else col_slab
    assert col_qkv % ch == 0 and col_slab % LANES == 0 and col_gz % vw == 0

    def rmap(s, c):
        return (s * n_chunks + c, 0)

    def at_col(blk):
        return lambda s, c: (s * n_chunks + c, blk)

    return pl.pallas_call(
        functools.partial(_gdn_body, C=C, valid_rows=valid_rows),
        grid=(n_seq, n_chunks),
        in_specs=[
            pl.BlockSpec((C, ch), at_col(col_qkv // ch)),
            pl.BlockSpec((C, LANES), at_col(col_slab // LANES)),
            pl.BlockSpec((C, vw), at_col(col_gz // vw)),
            pl.BlockSpec((CONV_WIDTH, ch), lambda s, c: (0, 0)),
            pl.BlockSpec((1, LANES), lambda s, c: (0, 0)),
            pl.BlockSpec((1, LANES), lambda s, c: (0, 0)),
            pl.BlockSpec((1, GDN_DV), lambda s, c: (0, 0)),
            pl.BlockSpec((1, HALO, ch), lambda s, c: (s, 0, 0)),
            pl.BlockSpec((1, GDN_HEADS, GDN_DK, GDN_DV), lambda s, c: (s, 0, 0, 0)),
        ],
        out_specs=(
            pl.BlockSpec((C, vw), rmap),
            pl.BlockSpec((1, GDN_HEADS, GDN_DK, GDN_DV), lambda s, c: (s, 0, 0, 0)),
        ),
        out_shape=(
            jax.ShapeDtypeStruct((rows, vw), out_dtype),
            jax.ShapeDtypeStruct((n_seq, GDN_HEADS, GDN_DK, GDN_DV), F32),
        ),
        scratch_shapes=[pltpu.VMEM((C + HALO, ch), F32)],
        compiler_params=_cparams("arbitrary", "arbitrary"),
        name=f"gdn_c{C}",
    )(qkv, misc, gates, conv_w, alog_vec, dtb_vec, gdn_norm.reshape(1, GDN_DV), halo0, s0)


SB = 2
ST = SUBLANES // SB


def _sattn_body(pt_ref, qi_ref, wi_ref, q_ref, kin_ref, kn_ref, vn_ref, tsamp_ref, cidx_hbm, ck_hbm, cv_hbm, out_ref,
                kibuf, kbuf, vbuf, sem, *, past, n_pages, n_sel, idx_scale):
    p = pl.program_id(0)
    n_steps = pl.num_programs(0)
    ls = kibuf.shape[1]
    rows = SB * ST

    def slot_of(step, bl):
        return (step % 2) * SB + bl

    def copies(step, bl):
        b = step * SB + bl
        sl = slot_of(step, bl)
        out = []
        for j in range(n_pages):
            pg = pt_ref[b, j]
            dst = pl.ds(j * PAGE_SIZE, PAGE_SIZE)
            out.append(pltpu.make_async_copy(cidx_hbm.at[pg], kibuf.at[sl, dst], sem.at[sl]))
            for kh in range(N_KV_HEADS):
                out.append(pltpu.make_async_copy(ck_hbm.at[pg, :, kh], kbuf.at[sl, kh, dst], sem.at[sl]))
                out.append(pltpu.make_async_copy(cv_hbm.at[pg, :, kh], vbuf.at[sl, kh, dst], sem.at[sl]))
        return out

    @pl.when(p == 0)
    def _():
        for bl in range(SB):
            for cp in copies(0, bl):
                cp.start()
        for sl in range(2 * SB):
            kibuf[sl, past:ls] = jnp.zeros((ls - past, IDX_DIM), F32)
            kbuf[sl, :, past:ls] = jnp.zeros((N_KV_HEADS, ls - past, HEAD_DIM), F32)
            vbuf[sl, :, past:ls] = jnp.zeros((N_KV_HEADS, ls - past, HEAD_DIM), F32)

    @pl.when(p + 1 < n_steps)
    def _():
        for bl in range(SB):
            for cp in copies(p + 1, bl):
                cp.start()

    slots = [slot_of(p, bl) for bl in range(SB)]
    bt = lax.broadcasted_iota(jnp.int32, (rows, ls), 0)
    lane = lax.broadcasted_iota(jnp.int32, (rows, ls), 1)
    r_new = lane - past
    valid = (lane < past) | ((r_new < rows) & (r_new // ST == bt // ST) & (r_new % ST <= bt % ST))

    for sl in slots:
        kibuf[sl, past:past + rows] = kin_ref[...]
        for kh in range(N_KV_HEADS):
            kbuf[sl, kh, past:past + rows] = kn_ref[:, kh * HEAD_DIM:(kh + 1) * HEAD_DIM]
            vbuf[sl, kh, past:past + rows] = vn_ref[:, kh * HEAD_DIM:(kh + 1) * HEAD_DIM]

    wi = wi_ref[...] * idx_scale
    qi = qi_ref[...].reshape(N_IDX_HEADS * rows, IDX_DIM).astype(BF16)
    key_chunks = [(s, min(512, ls - s)) for s in range(0, ls, 512)]

    def scores(bl):
        parts = []
        for s0, n in key_chunks:
            kic = kibuf[slots[bl], s0:s0 + n].astype(BF16)
            s = lax.dot_general(qi, kic, (((1,), (1,)), ((), ())), preferred_element_type=F32)
            acc = jnp.zeros((rows, n), F32)
            for h in range(N_IDX_HEADS):
                acc = acc + jnp.maximum(s[h * rows:(h + 1) * rows], 0.0) * wi[:, h:h + 1]
            parts.append(acc)
        return jnp.concatenate(parts, axis=1)

    sc = []
    for bl in range(SB):
        for cp in copies(p, bl):
            cp.wait()
        sc.append(scores(bl))
    score = sc[0]
    for bl in range(1, SB):
        score = jnp.where(bt // ST == bl, sc[bl], score)
    key = _sort_key(score, valid)

    def count_ge(cand):
        cnt = jnp.zeros((rows, LANES), F32)
        for t in range(ls // LANES):
            cnt = cnt + jnp.where(key[:, t * LANES:(t + 1) * LANES] >= cand, 1.0, 0.0)
        return jnp.sum(cnt, axis=1, keepdims=True)

    thr = _kth_largest(count_ge, rows, float(n_sel))
    thr_full = jnp.concatenate([thr] * (ls // LANES), axis=1)
    mb = jnp.where((key >= thr_full) & valid, 0.0, NEG).astype(F32)

    scale = HEAD_DIM ** -0.5
    row_b = lax.broadcasted_iota(jnp.int32, (GROUPS * rows, HEAD_DIM), 0) % rows // ST
    outs = [None] * N_KV_HEADS
    for bl in range(SB):
        for kh in range(N_KV_HEADS):
            qs = q_ref[kh * GROUPS:(kh + 1) * GROUPS].reshape(GROUPS * rows, HEAD_DIM).astype(BF16)
            kt = kbuf[slots[bl], kh].astype(BF16)
            vt = vbuf[slots[bl], kh].astype(BF16)
            s = lax.dot_general(qs, kt, (((1,), (1,)), ((), ())), preferred_element_type=F32) * scale
            s = (s.reshape(GROUPS, rows, ls) + tsamp_ref[kh * GROUPS:(kh + 1) * GROUPS] + mb[None]).reshape(GROUPS * rows, ls)
            m = jnp.max(s, axis=1, keepdims=True)
            e = jnp.exp(s - m)
            o = jnp.dot(e.astype(BF16), vt, preferred_element_type=F32) / jnp.sum(e, axis=1, keepdims=True)
            outs[kh] = o if bl == 0 else jnp.where(row_b == bl, o, outs[kh])

    for kh in range(N_KV_HEADS):
        for g in range(GROUPS):
            hd = kh * GROUPS + g
            out_ref[:, hd * HEAD_DIM:(hd + 1) * HEAD_DIM] = outs[kh][g * rows:(g + 1) * rows]


def _sample_attention(page_table, qi_s, wi_s, q_s, ki_new, k_new, v_new, tsamp, cache_idx, cache_k, cache_v, n_sel):
    R = wi_s.shape[0]
    n_b, n_pages = page_table.shape
    past = n_pages * PAGE_SIZE
    ls = past + PAGE_SIZE
    rows = SB * ST
    assert R == n_b * ST and n_b % SB == 0
    kvw = N_KV_HEADS * HEAD_DIM
    grid_spec = pltpu.PrefetchScalarGridSpec(
        num_scalar_prefetch=1,
        grid=(n_b // SB,),
        in_specs=[
            pl.BlockSpec((N_IDX_HEADS, rows, IDX_DIM), lambda p, pt: (0, p, 0)),
            pl.BlockSpec((rows, N_IDX_HEADS), lambda p, pt: (p, 0)),
            pl.BlockSpec((N_HEADS, rows, HEAD_DIM), lambda p, pt: (0, p, 0)),
            pl.BlockSpec((rows, IDX_DIM), lambda p, pt: (p, 0)),
            pl.BlockSpec((rows, kvw), lambda p, pt: (p, 0)),
            pl.BlockSpec((rows, kvw), lambda p, pt: (p, 0)),
            pl.BlockSpec((N_HEADS, SUBLANES, ls), lambda p, pt: (0, 0, 0)),
            pl.BlockSpec(memory_space=pl.ANY),
            pl.BlockSpec(memory_space=pl.ANY),
            pl.BlockSpec(memory_space=pl.ANY),
        ],
        out_specs=pl.BlockSpec((rows, N_HEADS * HEAD_DIM), lambda p, pt: (p, 0)),
        scratch_shapes=[
            pltpu.VMEM((2 * SB, ls, IDX_DIM), F32),
            pltpu.VMEM((2 * SB, N_KV_HEADS, ls, HEAD_DIM), F32),
            pltpu.VMEM((2 * SB, N_KV_HEADS, ls, HEAD_DIM), F32),
            pltpu.SemaphoreType.DMA((2 * SB,)),
        ],
    )
    return pl.pallas_call(
        functools.partial(_sattn_body, past=past, n_pages=n_pages, n_sel=n_sel,
                          idx_scale=(N_IDX_HEADS * IDX_DIM) ** -0.5),
        grid_spec=grid_spec,
        out_shape=jax.ShapeDtypeStruct((R, N_HEADS * HEAD_DIM), F32),
        compiler_params=_cparams("arbitrary"),
        name="sample_attention",
    )(page_table, qi_s, wi_s, q_s, ki_new, k_new, v_new, tsamp, cache_idx, cache_k, cache_v)


ROW_TILE_CAP = 1088
FF_TILE = 256


def _row_tile(m, cap=ROW_TILE_CAP):
    return max(t for t in range(16, cap + 1, 16) if m % t == 0)


def _pad_tokens(a, db, dt):
    a = a.reshape(db, dt, a.shape[-1])
    return jnp.pad(a, ((0, 0), (0, SUBLANES - dt), (0, 0))).reshape(db * SUBLANES, a.shape[-1])


def kernel(x_prompt, x_sample, cache_k, cache_v, cache_idx_k, state_conv, state_ssm, page_table, p_prompt, p_sample,
           rel_bias, norm_mix, w_in, conv_w, a_log, dt_bias, gdn_norm, w_attn_up, w_gdn_up, w_out, norm_ffn, w_gate_up,
           w_down, norm_ple, w_ple_gate, w_ple, norm_final):
    assert x_prompt.shape[0] == 1 and w_in.shape[0] == 1, "one prompt sequence, one layer"
    _, T, D = x_prompt.shape
    DB, DT, _ = x_sample.shape
    assert DT == ST and T % PQB == 0 and T % GDN_CHUNK == 0 and DT >= CONV_WIDTH - 1
    n_pool = cache_k.shape[1]
    past = page_table.shape[1] * PAGE_SIZE
    RS = DB * DT
    M = T + RS
    tm = _row_tile(M)
    kw, vw = GDN_HEADS * GDN_DK, GDN_HEADS * GDN_DV
    ch = 2 * kw + vw
    aw, kvw, iw = N_HEADS * HEAD_DIM, N_KV_HEADS * HEAD_DIM, N_IDX_HEADS * IDX_DIM

    w_t = jnp.swapaxes(w_in[0], 0, 1)
    w_dn = w_down[0].astype(BF16)
    w_pg = w_ple_gate[0].astype(BF16)
    w_pe = w_ple[0].astype(BF16)
    lay = ProjLayout(D)
    alog_vec = jnp.zeros((1, LANES), F32).at[0, GA_LANE:GA_LANE + GDN_HEADS].set(a_log[0])
    dtb_vec = jnp.zeros((1, LANES), F32).at[0, GA_LANE:GA_LANE + GDN_HEADS].set(dt_bias[0])

    x = jnp.concatenate([x_prompt[0], x_sample.reshape(RS, D)], axis=0)
    p_all = jnp.concatenate([p_prompt[0, 0], p_sample[0].reshape(RS, -1)], axis=0).astype(BF16)

    h0 = _rmsnorm_bf16(x, norm_mix[0], _row_tile(M, 512))
    qqi, kv_bf, pf = _proj(h0, w_t, tm)
    k_f, v_f = pf[:, lay.c_k:lay.c_k + kvw], pf[:, lay.c_v:lay.c_v + kvw]
    ki_f = pf[:, lay.c_ki:lay.c_ki + IDX_DIM]
    wi_f = pf[:, lay.c_wi:lay.c_wi + N_IDX_HEADS]
    qkv_s = pf[T:, lay.c_qkv:lay.c_qkv + ch]

    tprev, tdiag, tsamp = _bias_tables(rel_bias, PQB, past, past + PAGE_SIZE)
    attn_p = _prompt_attention(qqi, wi_f, ki_f.astype(BF16), kv_bf, tprev, tdiag, T, min(TOP_K_MAX, T // 4))
    qqi_s = qqi[:, T:].astype(F32)
    attn_s = _sample_attention(
        page_table, qqi_s[:N_IDX_HEADS], wi_f[T:], qqi_s[N_IDX_HEADS:], ki_f[T:], k_f[T:], v_f[T:], tsamp,
        cache_idx_k[0], cache_k[0], cache_v[0],
        min(TOP_K_MAX, (past + DT) // 4))
    attn = jnp.concatenate([attn_p, attn_s.astype(BF16)], axis=0)

    o_p, ssm_p = _gdn(pf, pf, pf, conv_w[0], alog_vec, dtb_vec, gdn_norm[0],
                      jnp.zeros((1, HALO, ch), F32), jnp.zeros((1, GDN_HEADS, GDN_DK, GDN_DV), F32),
                      n_seq=1, n_chunks=T // GDN_CHUNK, C=GDN_CHUNK, valid_rows=GDN_CHUNK, out_dtype=BF16,
                      col_qkv=lay.c_qkv, col_slab=lay.c_slab, col_gz=lay.c_gz)
    halo_s = jnp.pad(state_conv[0], ((0, 0), (HALO - (CONV_WIDTH - 1), 0), (0, 0)))
    o_s, ssm_s = _gdn(_pad_tokens(qkv_s, DB, DT), _pad_tokens(pf[T:, lay.c_slab:lay.c_slab + LANES], DB, DT),
                      _pad_tokens(pf[T:, lay.c_gz:lay.c_gz + vw], DB, DT),
                      conv_w[0], alog_vec, dtb_vec, gdn_norm[0], halo_s, state_ssm[0],
                      n_seq=DB, n_chunks=1, C=SUBLANES, valid_rows=DT, out_dtype=F32)
    o_s = o_s.reshape(DB, SUBLANES, vw)[:, :DT].reshape(RS, vw)
    o_all = jnp.concatenate([o_p, o_s.astype(BF16)], axis=0)

    merged = _merge(attn, o_all, pf, w_attn_up[0], w_gdn_up[0], tm, lay.c_gate_a, lay.c_gate_b)
    x1 = _resid_mm(x, merged, w_out[0], tm, 512)
    act = _ffn_up(x1, norm_ffn[0], w_gate_up[0], tm, FF_TILE)
    x2 = _resid_mm(x1, act, w_dn, tm, FF_TILE)
    tm_y = math.gcd(math.gcd(T, RS), 512)
    y_p = _ple_final(x2, norm_ple[0], w_pg, p_all, w_pe, norm_final, 0, T, tm_y)
    y_s = _ple_final(x2, norm_ple[0], w_pg, p_all, w_pe, norm_final, T, RS, tm_y)

    def heads(a, n):
        return a.reshape(a.shape[0], n, a.shape[1] // n)

    nc = CONV_WIDTH - 1
    return (
        y_p.reshape(1, T, D),
        y_s.reshape(DB, DT, D),
        heads(k_f[:T], N_KV_HEADS)[None, None],
        heads(v_f[:T], N_KV_HEADS)[None, None],
        ki_f[:T][None, None],
        pf[T - nc:T, lay.c_qkv:lay.c_qkv + ch][None, None],
        ssm_p.astype(state_ssm.dtype)[None],
        heads(k_f[T:], N_KV_HEADS).reshape(1, DB, DT, N_KV_HEADS, HEAD_DIM),
        heads(v_f[T:], N_KV_HEADS).reshape(1, DB, DT, N_KV_HEADS, HEAD_DIM),
        ki_f[T:].reshape(1, DB, DT, IDX_DIM),
        qkv_s.reshape(DB, DT, ch)[:, DT - nc:][None],
        ssm_s.astype(state_ssm.dtype)[None],
    )
```

```python
import functools
import math

import numpy as np
import jax
import jax.numpy as jnp
from jax import lax
from jax.experimental import pallas as pl
from jax.experimental.pallas import tpu as pltpu

F32 = jnp.float32
BF16 = jnp.bfloat16

N_HEADS = 16
N_KV_HEADS = 4
HEAD_DIM = 128
GROUPS = N_HEADS // N_KV_HEADS
N_IDX_HEADS = 32
IDX_DIM = 128
TOP_K_MAX = 256
PAGE_SIZE = 128
N_BUCKETS = 32
MAX_DISTANCE = 128
GDN_HEADS = 16
GDN_DK = 128
GDN_DV = 128
CONV_WIDTH = 4
GDN_CHUNK = 64
EPS = 1e-6
NEG = -1e30
LOG2E = math.log2(math.e)

LANES = 128
SUBLANES = 8
VMEM_LIMIT = 56 * 1024 * 1024


def _cparams(*sem):
    return pltpu.CompilerParams(dimension_semantics=sem, vmem_limit_bytes=VMEM_LIMIT)


def _rms_rows(x_ref, nw_ref, h_ref, rows):
    tm = x_ref.shape[0]
    rows = math.gcd(rows, tm)
    nw = nw_ref[...]

    def body(r, _):
        sl = pl.ds(pl.multiple_of(r * rows, rows), rows)
        x = x_ref[sl, :]
        ms = jnp.mean(x * x, axis=-1, keepdims=True)
        h_ref[sl, :] = (x * lax.rsqrt(ms + EPS) * nw).astype(h_ref.dtype)
        return 0

    lax.fori_loop(0, tm // rows, body, 0)


PROJ_TN = 512


def _rmsnorm_body(x_ref, nw_ref, h_ref):
    _rms_rows(x_ref, nw_ref, h_ref, 64)


def _rmsnorm_bf16(x, norm_w, tm):
    M, D = x.shape
    return pl.pallas_call(
        _rmsnorm_body,
        grid=(M // tm,),
        in_specs=[pl.BlockSpec((tm, D), lambda i: (i, 0)), pl.BlockSpec((1, D), lambda i: (0, 0))],
        out_specs=pl.BlockSpec((tm, D), lambda i: (i, 0)),
        out_shape=jax.ShapeDtypeStruct((M, D), BF16),
        compiler_params=_cparams("parallel"),
        name="rmsnorm",
    )(x, norm_w.reshape(1, D))


def _proj_body(h_ref, wt_ref, qqi_ref, kvb_ref, pf_ref, *, steps):
    j = pl.program_id(1)
    hpt = PROJ_TN // HEAD_DIM

    def within(name):
        lo, n = steps[name]
        return (j >= lo) & (j < lo + n)

    acc = lax.dot_general(h_ref[...], wt_ref[...].astype(BF16), (((1,), (1,)), ((), ())), preferred_element_type=F32)
    is_heads = within("q") | within("qi")

    @pl.when(is_heads)
    def _():
        for hh in range(hpt):
            qqi_ref[hh] = acc[:, hh * HEAD_DIM:(hh + 1) * HEAD_DIM].astype(qqi_ref.dtype)

    @pl.when(jnp.logical_not(is_heads))
    def _():
        pf_ref[...] = acc

    @pl.when(within("kv"))
    def _():
        kvb_ref[...] = acc.astype(kvb_ref.dtype)


class ProjLayout:
    def __init__(self, d_model):
        tn = PROJ_TN
        aw, kvw, iw = N_HEADS * HEAD_DIM, N_KV_HEADS * HEAD_DIM, N_IDX_HEADS * IDX_DIM
        ch = 2 * GDN_HEADS * GDN_DK + GDN_HEADS * GDN_DV
        gw = GDN_HEADS * GDN_DV + 2 * d_model
        o_ki = aw + 2 * kvw + iw
        o_qkv = o_ki + IDX_DIM + N_IDX_HEADS
        o_ga = o_qkv + ch
        o_gz = o_ga + 2 * GDN_HEADS
        assert GB_LANE == GA_LANE + GDN_HEADS
        groups = (("q", 0, aw), ("k", aw, kvw), ("v", aw + kvw, kvw), ("qi", aw + 2 * kvw, iw),
                  ("tile_a", o_ki, tn), ("tile_b", o_ga - LANES - GA_LANE, tn), ("qkv", o_qkv, ch), ("gates", o_gz, gw))
        self.steps, self.rows, lo = {}, {}, 0
        for name, start, width in groups:
            assert start % 16 == 0 and width % tn == 0
            self.steps[name] = (lo, width // tn)
            self.rows[name] = start
            lo += width // tn
        self.n_steps = lo
        self.ch, self.gw, self.kvw = ch, gw, kvw
        self.c_qkv, self.c_gz = 0, ch
        self.c_gate_a, self.c_gate_b = ch + GDN_HEADS * GDN_DV, ch + GDN_HEADS * GDN_DV + d_model
        self.c_k = ch + gw
        self.c_v = self.c_k + kvw
        self.c_ki = self.c_v + kvw
        self.c_wi = self.c_ki + IDX_DIM
        self.c_slab = self.c_ki + tn + LANES
        self.width = self.c_ki + 2 * tn


def _proj(h, w_t, tm):
    M, D = h.shape
    tn = PROJ_TN
    lay = ProjLayout(D)
    st = lay.steps
    assert st["v"][0] == st["k"][0] + st["k"][1] and st["tile_b"][0] == st["tile_a"][0] + 1
    assert st["gates"][0] == st["qkv"][0] + st["qkv"][1] and lay.c_gz == lay.ch and lay.kvw == tn

    def w_map(i, j):
        off = jnp.int32(0)
        for name, (lo, _) in st.items():
            off = jnp.where(j >= lo, lay.rows[name] + (j - lo) * tn, off)
        return (pl.multiple_of(off, 16), 0)

    hpt = tn // HEAD_DIM
    n_qi_blk, n_q_blk = N_IDX_HEADS // hpt, N_HEADS // hpt

    def heads_map(i, j):
        q_blk = n_qi_blk + jnp.clip(j - st["q"][0], 0, n_q_blk - 1)
        qi_blk = jnp.clip(j - st["qi"][0], 0, n_qi_blk - 1)
        return (jnp.where(j < st["qi"][0], q_blk, qi_blk), i, 0)

    def pf_map(i, j):
        blk = lay.c_k // tn + jnp.clip(j - st["k"][0], 0, 1)
        blk = jnp.where(j >= st["tile_a"][0], lay.c_ki // tn + j - st["tile_a"][0], blk)
        blk = jnp.where(j >= st["qkv"][0], j - st["qkv"][0], blk)
        return (i, blk)

    steps = dict(q=st["q"], qi=st["qi"], kv=(st["k"][0], 2))
    return pl.pallas_call(
        functools.partial(_proj_body, steps=steps),
        grid=(M // tm, lay.n_steps),
        in_specs=[
            pl.BlockSpec((tm, D), lambda i, j: (i, 0), pipeline_mode=pl.Buffered(1)),
            pl.BlockSpec((pl.Element(tn), pl.Element(D)), w_map),
        ],
        out_specs=(
            pl.BlockSpec((hpt, tm, HEAD_DIM), heads_map),
            pl.BlockSpec((tm, tn), lambda i, j: (i, jnp.clip(j - st["k"][0], 0, 1))),
            pl.BlockSpec((tm, tn), pf_map),
        ),
        out_shape=(
            jax.ShapeDtypeStruct((N_IDX_HEADS + N_HEADS, M, HEAD_DIM), BF16),
            jax.ShapeDtypeStruct((M, 2 * lay.kvw), BF16),
            jax.ShapeDtypeStruct((M, lay.width), F32),
        ),
        compiler_params=_cparams("parallel", "arbitrary"),
        name="in_proj",
    )(h, w_t)


def _merge_body(attn_ref, o_ref, ga_ref, gb_ref, wa_ref, wg_ref, out_ref):
    a = jnp.dot(attn_ref[...], wa_ref[...].astype(BF16), preferred_element_type=F32)
    b = jnp.dot(o_ref[...], wg_ref[...].astype(BF16), preferred_element_type=F32)
    out_ref[...] = (jax.nn.sigmoid(ga_ref[...]) * a + jax.nn.sigmoid(gb_ref[...]) * b).astype(out_ref.dtype)


def _merge(attn, o, gates, wa, wg, tm, col_a, col_b, tn=512):
    M, KA = attn.shape
    D = wa.shape[1]
    assert col_a % tn == 0 and col_b % tn == 0
    a_off, b_off = col_a // tn, col_b // tn
    return pl.pallas_call(
        _merge_body,
        grid=(M // tm, D // tn),
        in_specs=[
            pl.BlockSpec((tm, KA), lambda i, j: (i, 0), pipeline_mode=pl.Buffered(1)),
            pl.BlockSpec((tm, KA), lambda i, j: (i, 0), pipeline_mode=pl.Buffered(1)),
            pl.BlockSpec((tm, tn), lambda i, j: (i, a_off + j)),
            pl.BlockSpec((tm, tn), lambda i, j: (i, b_off + j)),
            pl.BlockSpec((KA, tn), lambda i, j: (0, j)),
            pl.BlockSpec((KA, tn), lambda i, j: (0, j)),
        ],
        out_specs=pl.BlockSpec((tm, tn), lambda i, j: (i, j)),
        out_shape=jax.ShapeDtypeStruct((M, D), BF16),
        compiler_params=_cparams("parallel", "arbitrary"),
        name="merge",
    )(attn, o, gates, gates, wa, wg)


def _resid_mm_body(x_ref, a_ref, w_ref, out_ref):
    out_ref[...] = x_ref[...] + jnp.dot(a_ref[...], w_ref[...].astype(BF16), preferred_element_type=F32)


def _resid_mm(x, a, w, tm, tn):
    M, N = x.shape
    Kd = a.shape[1]
    return pl.pallas_call(
        _resid_mm_body,
        grid=(M // tm, N // tn),
        in_specs=[
            pl.BlockSpec((tm, tn), lambda i, j: (i, j)),
            pl.BlockSpec((tm, Kd), lambda i, j: (i, 0), pipeline_mode=pl.Buffered(1)),
            pl.BlockSpec((Kd, tn), lambda i, j: (0, j)),
        ],
        out_specs=pl.BlockSpec((tm, tn), lambda i, j: (i, j)),
        out_shape=jax.ShapeDtypeStruct((M, N), F32),
        compiler_params=_cparams("parallel", "arbitrary"),
        name="resid_mm",
    )(x, a, w)


def _ffn_up_body(x_ref, nw_ref, wg_ref, wu_ref, out_ref, h_ref):
    @pl.when(pl.program_id(1) == 0)
    def _():
        _rms_rows(x_ref, nw_ref, h_ref, 64)

    h = h_ref[...]
    g = jnp.dot(h, wg_ref[...].astype(BF16), preferred_element_type=F32)
    u = jnp.dot(h, wu_ref[...].astype(BF16), preferred_element_type=F32)
    out_ref[...] = (jax.nn.silu(g) * u).astype(out_ref.dtype)


def _ffn_up(x, norm_w, w_gu, tm, tn):
    M, D = x.shape
    ffp = w_gu.shape[1] // 2
    nj = ffp // tn
    return pl.pallas_call(
        _ffn_up_body,
        grid=(M // tm, nj),
        in_specs=[
            pl.BlockSpec((tm, D), lambda i, j: (i, 0), pipeline_mode=pl.Buffered(1)),
            pl.BlockSpec((1, D), lambda i, j: (0, 0)),
            pl.BlockSpec((D, tn), lambda i, j: (0, j)),
            pl.BlockSpec((D, tn), lambda i, j: (0, nj + j)),
        ],
        out_specs=pl.BlockSpec((tm, tn), lambda i, j: (i, j)),
        out_shape=jax.ShapeDtypeStruct((M, ffp), BF16),
        scratch_shapes=[pltpu.VMEM((tm, D), BF16)],
        compiler_params=_cparams("parallel", "arbitrary"),
        name="ffn_up",
    )(x, norm_w.reshape(1, D), w_gu, w_gu)


def _ple_body(x_ref, nw_ref, wg_ref, p_ref, wp_ref, nf_ref, y_ref, h_ref, *, tn):
    j = pl.program_id(1)

    @pl.when(j == 0)
    def _():
        _rms_rows(x_ref, nw_ref, h_ref, 64)

    g = jnp.dot(h_ref[...], wg_ref[...].astype(BF16), preferred_element_type=F32)
    e = jnp.dot(p_ref[...], wp_ref[...].astype(BF16), preferred_element_type=F32)
    cols = pl.ds(pl.multiple_of(j * tn, tn), tn)
    y_ref[:, cols] = x_ref[:, cols] + jax.nn.sigmoid(g) * e

    @pl.when(j == pl.num_programs(1) - 1)
    def _():
        _rms_rows(y_ref, nf_ref, y_ref, 64)


def _ple_final(x, norm_w, w_gate, p, w_ple, norm_final, row0, n_rows, tm, tn=512):
    D = x.shape[1]
    P = p.shape[1]
    assert row0 % tm == 0 and n_rows % tm == 0
    r0 = row0 // tm
    return pl.pallas_call(
        functools.partial(_ple_body, tn=tn),
        grid=(n_rows // tm, D // tn),
        in_specs=[
            pl.BlockSpec((tm, D), lambda i, j: (r0 + i, 0), pipeline_mode=pl.Buffered(1)),
            pl.BlockSpec((1, D), lambda i, j: (0, 0)),
            pl.BlockSpec((D, tn), lambda i, j: (0, j)),
            pl.BlockSpec((tm, P), lambda i, j: (r0 + i, 0)),
            pl.BlockSpec((P, tn), lambda i, j: (0, j)),
            pl.BlockSpec((1, D), lambda i, j: (0, 0)),
        ],
        out_specs=pl.BlockSpec((tm, D), lambda i, j: (i, 0)),
        out_shape=jax.ShapeDtypeStruct((n_rows, D), F32),
        scratch_shapes=[pltpu.VMEM((tm, D), BF16)],
        compiler_params=_cparams("parallel", "arbitrary"),
        name="ple_final",
    )(x, norm_w.reshape(1, D), w_gate, p, w_ple, norm_final.reshape(1, D))


def _bucket_thresholds():
    d = np.arange(0, 4 * MAX_DISTANCE)
    max_exact = N_BUCKETS // 2
    large = max_exact + (np.log(np.maximum(d, 1) / max_exact) / math.log(MAX_DISTANCE / max_exact)
                         * (N_BUCKETS - max_exact)).astype(np.int32)
    b = np.where(d < max_exact, d, np.minimum(large, N_BUCKETS - 1))
    return [int(np.argmax(b >= k)) for k in range(N_BUCKETS)]


_BUCKET_THR = _bucket_thresholds()


def _bias_of_dist(rb_ref, h, d):
    v = jnp.full(d.shape, rb_ref[0, h], F32)
    for b in range(1, N_BUCKETS):
        v = jnp.where(d >= _BUCKET_THR[b], rb_ref[b, h], v)
    return v


def _bias_tables_body(rb_ref, tprev_ref, tdiag_ref, tsamp_ref, *, qb, past):
    h = pl.program_id(0)
    r = lax.broadcasted_iota(jnp.int32, (qb, qb), 0)
    c = lax.broadcasted_iota(jnp.int32, (qb, qb), 1)
    far = rb_ref[N_BUCKETS - 1, h]
    tprev_ref[0] = (_bias_of_dist(rb_ref, h, r + qb - c) - far) * LOG2E
    tdiag_ref[0] = (_bias_of_dist(rb_ref, h, r - c) - far) * LOG2E
    ls = tsamp_ref.shape[2]
    t = lax.broadcasted_iota(jnp.int32, (SUBLANES, ls), 0) % 4
    lane = lax.broadcasted_iota(jnp.int32, (SUBLANES, ls), 1)
    d = jnp.where(lane < past, past + t - lane, t - (lane - past) % 4)
    tsamp_ref[0] = _bias_of_dist(rb_ref, h, d)


def _bias_tables(rel_bias, qb, past, ls):
    return pl.pallas_call(
        functools.partial(_bias_tables_body, qb=qb, past=past),
        grid=(N_HEADS,),
        in_specs=[pl.BlockSpec(memory_space=pltpu.SMEM)],
        out_specs=(
            pl.BlockSpec((1, qb, qb), lambda h: (h, 0, 0)),
            pl.BlockSpec((1, qb, qb), lambda h: (h, 0, 0)),
            pl.BlockSpec((1, SUBLANES, ls), lambda h: (h, 0, 0)),
        ),
        out_shape=(
            jax.ShapeDtypeStruct((N_HEADS, qb, qb), F32),
            jax.ShapeDtypeStruct((N_HEADS, qb, qb), F32),
            jax.ShapeDtypeStruct((N_HEADS, SUBLANES, ls), F32),
        ),
        compiler_params=_cparams("arbitrary"),
        name="bias_tables",
    )(rel_bias)


INT_MIN = -2 ** 31


def _sort_key(score, valid):
    bits = lax.bitcast_convert_type(score, jnp.int32)
    key = jnp.where(bits < 0, bits ^ jnp.int32(0x7FFFFFFF), bits)
    return jnp.where(valid, key, jnp.int32(INT_MIN))


def _kth_largest(count_ge, rows, k):
    def step(b, t):
        cand = t + lax.shift_left(jnp.int32(1), jnp.asarray(31 - b, jnp.int32))
        n = count_ge(cand)
        return jnp.where(n >= k, cand, t)

    return lax.fori_loop(0, 32, step, jnp.full((rows, LANES), INT_MIN, jnp.int32))


PQB = 256
PKC = 256


def _pattn_body(qi_ref, wi_ref, ki_ref, q_ref, k_ref, v_ref, tprev_ref, tdiag_ref, out_ref,
                key_ref, wb_ref, m_ref, l_ref, acc_ref, *, idx_scale, n_sel):
    i = pl.program_id(0)
    qb, kc = PQB, PKC
    hg = 8

    wi = wi_ref[...] * idx_scale
    for h in range(N_IDX_HEADS):
        wb_ref[h] = jnp.broadcast_to(wi[:, h:h + 1], (qb, LANES))

    def chunk(c):
        return pl.ds(pl.multiple_of(c * kc, kc), kc)

    row = lax.broadcasted_iota(jnp.int32, (qb, kc), 0)
    col = lax.broadcasted_iota(jnp.int32, (qb, kc), 1)

    def score_chunk(c, _):
        kic = ki_ref[chunk(c), :]
        acc = [jnp.zeros((qb, LANES), F32) for _ in range(kc // LANES)]
        for g in range(N_IDX_HEADS // hg):
            qg = qi_ref[g * hg:(g + 1) * hg].reshape(hg * qb, IDX_DIM)
            s = lax.dot_general(qg, kic, (((1,), (1,)), ((), ())), preferred_element_type=F32)
            for hh in range(hg):
                w = wb_ref[g * hg + hh]
                for half in range(kc // LANES):
                    sh = s[hh * qb:(hh + 1) * qb, half * LANES:(half + 1) * LANES]
                    acc[half] = acc[half] + jnp.maximum(sh, 0.0) * w
        score = jnp.concatenate(acc, axis=1)
        valid = (c * kc + col) <= (i * qb + row)
        key_ref[:, chunk(c)] = _sort_key(score, valid)
        return 0

    lax.fori_loop(0, i + 1, score_chunk, 0)

    def count_ge(cand):
        def body(c, cnt):
            kk = key_ref[:, chunk(c)]
            for half in range(kc // LANES):
                cnt = cnt + jnp.where(kk[:, half * LANES:(half + 1) * LANES] >= cand, 1.0, 0.0)
            return cnt

        cnt = lax.fori_loop(0, i + 1, body, jnp.zeros((qb, LANES), F32))
        return jnp.sum(cnt, axis=1, keepdims=True)

    thr = _kth_largest(count_ge, qb, float(n_sel))

    def mask_chunk(c, _):
        kk = key_ref[:, chunk(c)]
        t2 = jnp.concatenate([thr] * (kc // LANES), axis=1)
        sel = (kk >= t2) & (kk > jnp.int32(INT_MIN))
        key_ref[:, chunk(c)] = lax.bitcast_convert_type(jnp.where(sel, 0.0, NEG).astype(F32), jnp.int32)
        return 0

    lax.fori_loop(0, i + 1, mask_chunk, 0)

    c1 = HEAD_DIM ** -0.5 * LOG2E

    def kv_head(kh, _):
        qs = q_ref[pl.ds(kh * GROUPS, GROUPS)].reshape(GROUPS * qb, HEAD_DIM)
        lanes = pl.ds(pl.multiple_of(kh * HEAD_DIM, HEAD_DIM), HEAD_DIM)
        m_ref[...] = jnp.full(m_ref.shape, NEG, F32)
        l_ref[...] = jnp.zeros(l_ref.shape, F32)
        acc_ref[...] = jnp.zeros(acc_ref.shape, F32)

        def attend(items):
            ss = [lax.dot_general(qs, k_ref[chunk(c), lanes], (((1,), (1,)), ((), ())), preferred_element_type=F32)
                  for c, _ in items]
            m, l, acc = m_ref[...], l_ref[...], acc_ref[...]
            for (c, bias_of_group), s in zip(items, ss):
                vt = v_ref[chunk(c), lanes]
                vx = jnp.concatenate([vt, jnp.ones_like(vt)], axis=1)
                mb = lax.bitcast_convert_type(key_ref[:, chunk(c)], F32)
                ts = []
                for g in range(GROUPS):
                    t = s[g * qb:(g + 1) * qb] * c1 + mb
                    if bias_of_group is not None:
                        t = t + bias_of_group(g)
                    ts.append(t)
                t = jnp.concatenate(ts, axis=0)
                m_new = jnp.maximum(m, jnp.max(t, axis=1, keepdims=True))
                alpha = jnp.exp2(m - m_new)
                p = jnp.exp2(t - jnp.concatenate([m_new] * (kc // LANES), axis=1))
                pv = jnp.dot(p.astype(BF16), vx, preferred_element_type=F32)
                acc = alpha * acc + pv[:, :HEAD_DIM]
                l = alpha * l + pv[:, HEAD_DIM:]
                m = m_new
            m_ref[...], l_ref[...], acc_ref[...] = m, l, acc

        n_far = jnp.maximum(i - 1, 0)

        def far_pair(cc, _):
            attend([(2 * cc, None), (2 * cc + 1, None)])
            return 0

        lax.fori_loop(0, n_far // 2, far_pair, 0)

        @pl.when(n_far % 2 == 1)
        def _():
            attend([(n_far - 1, None)])

        @pl.when(i >= 1)
        def _():
            attend([(i - 1, lambda g: tprev_ref[kh * GROUPS + g]), (i, lambda g: tdiag_ref[kh * GROUPS + g])])

        @pl.when(i == 0)
        def _():
            attend([(i, lambda g: tdiag_ref[kh * GROUPS + g])])

        o = acc_ref[...] / l_ref[...]
        for g in range(GROUPS):
            out_ref[:, pl.ds(pl.multiple_of((kh * GROUPS + g) * HEAD_DIM, HEAD_DIM), HEAD_DIM)] = (
                o[g * qb:(g + 1) * qb].astype(out_ref.dtype))
        return 0

    lax.fori_loop(0, N_KV_HEADS, kv_head, 0)


def _prompt_attention(qqi, wi, ki_bf, kv_bf, tprev, tdiag, seq, n_sel):
    qb = PQB
    assert N_IDX_HEADS % N_HEADS == 0
    resident = dict(pipeline_mode=pl.Buffered(1))
    return pl.pallas_call(
        functools.partial(_pattn_body, idx_scale=(N_IDX_HEADS * IDX_DIM) ** -0.5, n_sel=n_sel),
        grid=(seq // qb,),
        in_specs=[
            pl.BlockSpec((N_IDX_HEADS, qb, IDX_DIM), lambda i: (0, i, 0)),
            pl.BlockSpec((qb, N_IDX_HEADS), lambda i: (i, 0)),
            pl.BlockSpec((seq, IDX_DIM), lambda i: (0, 0), **resident),
            pl.BlockSpec((N_HEADS, qb, HEAD_DIM), lambda i: (N_IDX_HEADS // N_HEADS, i, 0)),
            pl.BlockSpec((seq, N_KV_HEADS * HEAD_DIM), lambda i: (0, 0), **resident),
            pl.BlockSpec((seq, N_KV_HEADS * HEAD_DIM), lambda i: (0, 1), **resident),
            pl.BlockSpec((N_HEADS, qb, qb), lambda i: (0, 0, 0), **resident),
            pl.BlockSpec((N_HEADS, qb, qb), lambda i: (0, 0, 0), **resident),
        ],
        out_specs=pl.BlockSpec((qb, N_HEADS * HEAD_DIM), lambda i: (i, 0)),
        out_shape=jax.ShapeDtypeStruct((seq, N_HEADS * HEAD_DIM), BF16),
        scratch_shapes=[
            pltpu.VMEM((qb, seq), jnp.int32),
            pltpu.VMEM((N_IDX_HEADS, qb, LANES), F32),
            pltpu.VMEM((GROUPS * qb, LANES), F32),
            pltpu.VMEM((GROUPS * qb, LANES), F32),
            pltpu.VMEM((GROUPS * qb, HEAD_DIM), F32),
        ],
        compiler_params=_cparams("arbitrary"),
        name="prompt_attention",
    )(qqi, wi, ki_bf, qqi, kv_bf, kv_bf, tprev, tdiag)


GA_LANE = 32
GB_LANE = 48
HALO = SUBLANES
def _split2(a):
    hi = a.astype(BF16)
    return hi, (a - hi.astype(F32)).astype(BF16)


def _split3(a):
    hi = a.astype(BF16)
    r = a - hi.astype(F32)
    mid = r.astype(BF16)
    return hi, mid, (r - mid.astype(F32)).astype(BF16)


def _mm3(a2, b2):
    (ah, al), (bh, bl) = a2, b2
    d = lambda x, y: jnp.dot(x, y, preferred_element_type=F32)
    return d(ah, bh) + (d(ah, bl) + d(al, bh))


def _bmm(a, b):
    return jnp.dot(a.astype(BF16), b.astype(BF16), preferred_element_type=F32)


def _bmm_nt(a, b):
    return lax.dot_general(a.astype(BF16), b.astype(BF16), (((1,), (1,)), ((), ())), preferred_element_type=F32)


def _bmm_tn(a, b):
    return lax.dot_general(a.astype(BF16), b.astype(BF16), (((0,), (0,)), ((), ())), preferred_element_type=F32)


def _gdn_body(qkv_ref, slab_ref, gz_ref, cw_ref, alog_ref, dtb_ref, gn_ref, halo0_ref, s0_ref,
              o_ref, s_ref, xp_ref, *, C, valid_rows):
    c = pl.program_id(1)
    kw, vw = GDN_HEADS * GDN_DK, GDN_HEADS * GDN_DV

    @pl.when(c == 0)
    def _():
        xp_ref[0:HALO] = halo0_ref[0]
        s_ref[...] = s0_ref[...]

    xp_ref[HALO:HALO + C] = qkv_ref[...]

    ri = lax.broadcasted_iota(jnp.int32, (C, C), 0)
    ci = lax.broadcasted_iota(jnp.int32, (C, C), 1)
    causal = ri >= ci
    strict = ri > ci
    eye = jnp.where(ri == ci, 1.0, 0.0).astype(F32)
    ltri = jnp.where(causal, 1.0, 0.0).astype(F32)

    slab = slab_ref[...]
    live = lax.broadcasted_iota(jnp.int32, slab.shape, 0) < valid_rows
    g_all = jnp.where(live, -jnp.exp(alog_ref[...]) * jax.nn.softplus(slab + dtb_ref[...]), 0.0)
    beta_all = jnp.where(live, jax.nn.sigmoid(slab), 0.0)
    ltri_b = ltri.astype(BF16)
    gc_all = sum(jnp.dot(ltri_b, part, preferred_element_type=F32) for part in _split3(g_all))
    pad = jnp.zeros((LANES - C, LANES), F32)
    gc_t = jnp.concatenate([gc_all, pad], axis=0).T

    def conv(cols):
        y = cw_ref[0:1, cols] * xp_ref[HALO - 3:HALO - 3 + C, cols]
        for j in range(1, CONV_WIDTH):
            y = y + cw_ref[j:j + 1, cols] * xp_ref[HALO - 3 + j:HALO - 3 + j + C, cols]
        return jax.nn.silu(y)

    def l2n(x):
        return x * lax.rsqrt(jnp.sum(x * x, axis=-1, keepdims=True) + 1e-6)

    n_sq = max(1, (C - 1).bit_length() - 1)

    hs = range(GDN_HEADS)
    q = [l2n(conv(slice(h * GDN_DK, (h + 1) * GDN_DK))) * GDN_DK ** -0.5 for h in hs]
    k = [l2n(conv(slice(kw + h * GDN_DK, kw + (h + 1) * GDN_DK))) for h in hs]
    v = [conv(slice(2 * kw + h * GDN_DV, 2 * kw + (h + 1) * GDN_DV)) for h in hs]
    beta = [beta_all[:, GB_LANE + h:GB_LANE + h + 1] for h in hs]
    gc = [gc_all[:, GA_LANE + h:GA_LANE + h + 1] for h in hs]
    gc_row = [gc_t[GA_LANE + h:GA_LANE + h + 1, 0:C] for h in hs]
    gc_last = [gc_all[C - 1:C, GA_LANE + h:GA_LANE + h + 1] for h in hs]
    decay = [jnp.exp(jnp.where(causal, gc[h] - gc_row[h], NEG)) for h in hs]
    kb = [k[h] * beta[h] for h in hs]
    p = [-jnp.where(strict, _bmm_nt(kb[h], k[h]) * decay[h], 0.0) for h in hs]
    t_inv = [eye + p[h] for h in hs]
    p2 = [_split2(x) for x in p]
    for _ in range(n_sq):
        p = [_mm3(x2, x2) for x2 in p2]
        p2 = [_split2(x) for x in p]
        t_inv = [t + _mm3(_split2(t), x2) for t, x2 in zip(t_inv, p2)]
    egc = [jnp.exp(gc[h]) for h in hs]
    u = [_bmm(t_inv[h], v[h] * beta[h]) for h in hs]
    w = [_bmm(t_inv[h], kb[h] * egc[h]) for h in hs]
    a_in = [jnp.where(causal, _bmm_nt(q[h], k[h]) * decay[h], 0.0) for h in hs]
    qd = [q[h] * egc[h] for h in hs]
    kt = [k[h] * jnp.exp(gc_last[h] - gc[h]) for h in hs]
    s_old = [s_ref[0, h] for h in hs]
    v_new = [u[h] - _bmm(w[h], s_old[h]) for h in hs]
    o = [_bmm(qd[h], s_old[h]) + _bmm(a_in[h], v_new[h]) for h in hs]
    for h in hs:
        s_ref[0, h] = s_old[h] * jnp.exp(gc_last[h]) + _bmm_tn(kt[h], v_new[h])
    for h in hs:
        on = o[h] * lax.rsqrt(jnp.mean(o[h] * o[h], axis=-1, keepdims=True) + EPS) * gn_ref[...]
        gz = gz_ref[:, h * GDN_DV:(h + 1) * GDN_DV]
        o_ref[:, h * GDN_DV:(h + 1) * GDN_DV] = (on * jax.nn.silu(gz)).astype(o_ref.dtype)

    xp_ref[0:HALO] = xp_ref[C:C + HALO]


def _gdn(qkv, misc, gates, conv_w, alog_vec, dtb_vec, gdn_norm, halo0, s0, *, n_seq, n_chunks, C, valid_rows,
         out_dtype, col_qkv=0, col_slab=None, col_gz=0):
    ch = conv_w.shape[1]
    vw = GDN_HEADS * GDN_DV
    rows = n_seq * n_chunks * C
    col_slab = misc.shape[1] - LANES if col_slab is None else col_slab
    assert col_qkv % ch == 0 and col_slab % LANES == 0 and col_gz % vw == 0

    def rmap(s, c):
        return (s * n_chunks + c, 0)

    def at_col(blk):
        return lambda s, c: (s * n_chunks + c, blk)

    return pl.pallas_call(
        functools.partial(_gdn_body, C=C, valid_rows=valid_rows),
        grid=(n_seq, n_chunks),
        in_specs=[
            pl.BlockSpec((C, ch), at_col(col_qkv // ch)),
            pl.BlockSpec((C, LANES), at_col(col_slab // LANES)),
            pl.BlockSpec((C, vw), at_col(col_gz // vw)),
            pl.BlockSpec((CONV_WIDTH, ch), lambda s, c: (0, 0)),
            pl.BlockSpec((1, LANES), lambda s, c: (0, 0)),
            pl.BlockSpec((1, LANES), lambda s, c: (0, 0)),
            pl.BlockSpec((1, GDN_DV), lambda s, c: (0, 0)),
            pl.BlockSpec((1, HALO, ch), lambda s, c: (s, 0, 0)),
            pl.BlockSpec((1, GDN_HEADS, GDN_DK, GDN_DV), lambda s, c: (s, 0, 0, 0)),
        ],
        out_specs=(
            pl.BlockSpec((C, vw), rmap),
            pl.BlockSpec((1, GDN_HEADS, GDN_DK, GDN_DV), lambda s, c: (s, 0, 0, 0)),
        ),
        out_shape=(
            jax.ShapeDtypeStruct((rows, vw), out_dtype),
            jax.ShapeDtypeStruct((n_seq, GDN_HEADS, GDN_DK, GDN_DV), F32),
        ),
        scratch_shapes=[pltpu.VMEM((C + HALO, ch), F32)],
        compiler_params=_cparams("arbitrary", "arbitrary"),
        name=f"gdn_c{C}",
    )(qkv, misc, gates, conv_w, alog_vec, dtb_vec, gdn_norm.reshape(1, GDN_DV), halo0, s0)


SB = 2
ST = SUBLANES // SB


def _sattn_body(pt_ref, qi_ref, wi_ref, q_ref, kin_ref, kn_ref, vn_ref, tsamp_ref, cidx_hbm, ck_hbm, cv_hbm, out_ref,
                kibuf, kbuf, vbuf, sem, *, past, n_pages, n_sel, idx_scale):
    p = pl.program_id(0)
    n_steps = pl.num_programs(0)
    ls = kibuf.shape[1]
    rows = SB * ST

    def slot_of(step, bl):
        return (step % 2) * SB + bl

    def copies(step, bl):
        b = step * SB + bl
        sl = slot_of(step, bl)
        out = []
        for j in range(n_pages):
            pg = pt_ref[b, j]
            dst = pl.ds(j * PAGE_SIZE, PAGE_SIZE)
            out.append(pltpu.make_async_copy(cidx_hbm.at[pg], kibuf.at[sl, dst], sem.at[sl]))
            for kh in range(N_KV_HEADS):
                out.append(pltpu.make_async_copy(ck_hbm.at[pg, :, kh], kbuf.at[sl, kh, dst], sem.at[sl]))
                out.append(pltpu.make_async_copy(cv_hbm.at[pg, :, kh], vbuf.at[sl, kh, dst], sem.at[sl]))
        return out

    @pl.when(p == 0)
    def _():
        for bl in range(SB):
            for cp in copies(0, bl):
                cp.start()
        for sl in range(2 * SB):
            kibuf[sl, past:ls] = jnp.zeros((ls - past, IDX_DIM), F32)
            kbuf[sl, :, past:ls] = jnp.zeros((N_KV_HEADS, ls - past, HEAD_DIM), F32)
            vbuf[sl, :, past:ls] = jnp.zeros((N_KV_HEADS, ls - past, HEAD_DIM), F32)

    @pl.when(p + 1 < n_steps)
    def _():
        for bl in range(SB):
            for cp in copies(p + 1, bl):
                cp.start()

    slots = [slot_of(p, bl) for bl in range(SB)]
    bt = lax.broadcasted_iota(jnp.int32, (rows, ls), 0)
    lane = lax.broadcasted_iota(jnp.int32, (rows, ls), 1)
    r_new = lane - past
    valid = (lane < past) | ((r_new < rows) & (r_new // ST == bt // ST) & (r_new % ST <= bt % ST))

    for sl in slots:
        kibuf[sl, past:past + rows] = kin_ref[...]
        for kh in range(N_KV_HEADS):
            kbuf[sl, kh, past:past + rows] = kn_ref[:, kh * HEAD_DIM:(kh + 1) * HEAD_DIM]
            vbuf[sl, kh, past:past + rows] = vn_ref[:, kh * HEAD_DIM:(kh + 1) * HEAD_DIM]

    wi = wi_ref[...] * idx_scale
    qi = qi_ref[...].reshape(N_IDX_HEADS * rows, IDX_DIM).astype(BF16)
    key_chunks = [(s, min(512, ls - s)) for s in range(0, ls, 512)]

    def scores(bl):
        parts = []
        for s0, n in key_chunks:
            kic = kibuf[slots[bl], s0:s0 + n].astype(BF16)
            s = lax.dot_general(qi, kic, (((1,), (1,)), ((), ())), preferred_element_type=F32)
            acc = jnp.zeros((rows, n), F32)
            for h in range(N_IDX_HEADS):
                acc = acc + jnp.maximum(s[h * rows:(h + 1) * rows], 0.0) * wi[:, h:h + 1]
            parts.append(acc)
        return jnp.concatenate(parts, axis=1)

    sc = []
    for bl in range(SB):
        for cp in copies(p, bl):
            cp.wait()
        sc.append(scores(bl))
    score = sc[0]
    for bl in range(1, SB):
        score = jnp.where(bt // ST == bl, sc[bl], score)
    key = _sort_key(score, valid)

    def count_ge(cand):
        cnt = jnp.zeros((rows, LANES), F32)
        for t in range(ls // LANES):
            cnt = cnt + jnp.where(key[:, t * LANES:(t + 1) * LANES] >= cand, 1.0, 0.0)
        return jnp.sum(cnt, axis=1, keepdims=True)

    thr = _kth_largest(count_ge, rows, float(n_sel))
    thr_full = jnp.concatenate([thr] * (ls // LANES), axis=1)
    mb = jnp.where((key >= thr_full) & valid, 0.0, NEG).astype(F32)

    scale = HEAD_DIM ** -0.5
    row_b = lax.broadcasted_iota(jnp.int32, (GROUPS * rows, HEAD_DIM), 0) % rows // ST
    outs = [None] * N_KV_HEADS
    for bl in range(SB):
        for kh in range(N_KV_HEADS):
            qs = q_ref[kh * GROUPS:(kh + 1) * GROUPS].reshape(GROUPS * rows, HEAD_DIM).astype(BF16)
            kt = kbuf[slots[bl], kh].astype(BF16)
            vt = vbuf[slots[bl], kh].astype(BF16)
            s = lax.dot_general(qs, kt, (((1,), (1,)), ((), ())), preferred_element_type=F32) * scale
            s = (s.reshape(GROUPS, rows, ls) + tsamp_ref[kh * GROUPS:(kh + 1) * GROUPS] + mb[None]).reshape(GROUPS * rows, ls)
            m = jnp.max(s, axis=1, keepdims=True)
            e = jnp.exp(s - m)
            o = jnp.dot(e.astype(BF16), vt, preferred_element_type=F32) / jnp.sum(e, axis=1, keepdims=True)
            outs[kh] = o if bl == 0 else jnp.where(row_b == bl, o, outs[kh])

    for kh in range(N_KV_HEADS):
        for g in range(GROUPS):
            hd = kh * GROUPS + g
            out_ref[:, hd * HEAD_DIM:(hd + 1) * HEAD_DIM] = outs[kh][g * rows:(g + 1) * rows]


def _sample_attention(page_table, qi_s, wi_s, q_s, ki_new, k_new, v_new, tsamp, cache_idx, cache_k, cache_v, n_sel):
    R = wi_s.shape[0]
    n_b, n_pages = page_table.shape
    past = n_pages * PAGE_SIZE
    ls = past + PAGE_SIZE
    rows = SB * ST
    assert R == n_b * ST and n_b % SB == 0
    kvw = N_KV_HEADS * HEAD_DIM
    grid_spec = pltpu.PrefetchScalarGridSpec(
        num_scalar_prefetch=1,
        grid=(n_b // SB,),
        in_specs=[
            pl.BlockSpec((N_IDX_HEADS, rows, IDX_DIM), lambda p, pt: (0, p, 0)),
            pl.BlockSpec((rows, N_IDX_HEADS), lambda p, pt: (p, 0)),
            pl.BlockSpec((N_HEADS, rows, HEAD_DIM), lambda p, pt: (0, p, 0)),
            pl.BlockSpec((rows, IDX_DIM), lambda p, pt: (p, 0)),
            pl.BlockSpec((rows, kvw), lambda p, pt: (p, 0)),
            pl.BlockSpec((rows, kvw), lambda p, pt: (p, 0)),
            pl.BlockSpec((N_HEADS, SUBLANES, ls), lambda p, pt: (0, 0, 0)),
            pl.BlockSpec(memory_space=pl.ANY),
            pl.BlockSpec(memory_space=pl.ANY),
            pl.BlockSpec(memory_space=pl.ANY),
        ],
        out_specs=pl.BlockSpec((rows, N_HEADS * HEAD_DIM), lambda p, pt: (p, 0)),
        scratch_shapes=[
            pltpu.VMEM((2 * SB, ls, IDX_DIM), F32),
            pltpu.VMEM((2 * SB, N_KV_HEADS, ls, HEAD_DIM), F32),
            pltpu.VMEM((2 * SB, N_KV_HEADS, ls, HEAD_DIM), F32),
            pltpu.SemaphoreType.DMA((2 * SB,)),
        ],
    )
    return pl.pallas_call(
        functools.partial(_sattn_body, past=past, n_pages=n_pages, n_sel=n_sel,
                          idx_scale=(N_IDX_HEADS * IDX_DIM) ** -0.5),
        grid_spec=grid_spec,
        out_shape=jax.ShapeDtypeStruct((R, N_HEADS * HEAD_DIM), F32),
        compiler_params=_cparams("arbitrary"),
        name="sample_attention",
    )(page_table, qi_s, wi_s, q_s, ki_new, k_new, v_new, tsamp, cache_idx, cache_k, cache_v)


ROW_TILE_CAP = 1088
FF_TILE = 256


def _row_tile(m, cap=ROW_TILE_CAP):
    return max(t for t in range(16, cap + 1, 16) if m % t == 0)


def _pad_tokens(a, db, dt):
    a = a.reshape(db, dt, a.shape[-1])
    return jnp.pad(a, ((0, 0), (0, SUBLANES - dt), (0, 0))).reshape(db * SUBLANES, a.shape[-1])


def kernel(x_prompt, x_sample, cache_k, cache_v, cache_idx_k, state_conv, state_ssm, page_table, p_prompt, p_sample,
           rel_bias, norm_mix, w_in, conv_w, a_log, dt_bias, gdn_norm, w_attn_up, w_gdn_up, w_out, norm_ffn, w_gate_up,
           w_down, norm_ple, w_ple_gate, w_ple, norm_final):
    assert x_prompt.shape[0] == 1 and w_in.shape[0] == 1, "one prompt sequence, one layer"
    _, T, D = x_prompt.shape
    DB, DT, _ = x_sample.shape
    assert DT == ST and T % PQB == 0 and T % GDN_CHUNK == 0 and DT >= CONV_WIDTH - 1
    n_pool = cache_k.shape[1]
    past = page_table.shape[1] * PAGE_SIZE
    RS = DB * DT
    M = T + RS
    tm = _row_tile(M)
    kw, vw = GDN_HEADS * GDN_DK, GDN_HEADS * GDN_DV
    ch = 2 * kw + vw
    aw, kvw, iw = N_HEADS * HEAD_DIM, N_KV_HEADS * HEAD_DIM, N_IDX_HEADS * IDX_DIM

    w_t = jnp.swapaxes(w_in[0], 0, 1)
    w_dn = w_down[0].astype(BF16)
    w_pg = w_ple_gate[0].astype(BF16)
    w_pe = w_ple[0].astype(BF16)
    lay = ProjLayout(D)
    alog_vec = jnp.zeros((1, LANES), F32).at[0, GA_LANE:GA_LANE + GDN_HEADS].set(a_log[0])
    dtb_vec = jnp.zeros((1, LANES), F32).at[0, GA_LANE:GA_LANE + GDN_HEADS].set(dt_bias[0])

    x = jnp.concatenate([x_prompt[0], x_sample.reshape(RS, D)], axis=0)
    p_all = jnp.concatenate([p_prompt[0, 0], p_sample[0].reshape(RS, -1)], axis=0).astype(BF16)

    h0 = _rmsnorm_bf16(x, norm_mix[0], _row_tile(M, 512))
    qqi, kv_bf, pf = _proj(h0, w_t, tm)
    k_f, v_f = pf[:, lay.c_k:lay.c_k + kvw], pf[:, lay.c_v:lay.c_v + kvw]
    ki_f = pf[:, lay.c_ki:lay.c_ki + IDX_DIM]
    wi_f = pf[:, lay.c_wi:lay.c_wi + N_IDX_HEADS]
    qkv_s = pf[T:, lay.c_qkv:lay.c_qkv + ch]

    tprev, tdiag, tsamp = _bias_tables(rel_bias, PQB, past, past + PAGE_SIZE)
    attn_p = _prompt_attention(qqi, wi_f, ki_f.astype(BF16), kv_bf, tprev, tdiag, T, min(TOP_K_MAX, T // 4))
    qqi_s = qqi[:, T:].astype(F32)
    attn_s = _sample_attention(
        page_table, qqi_s[:N_IDX_HEADS], wi_f[T:], qqi_s[N_IDX_HEADS:], ki_f[T:], k_f[T:], v_f[T:], tsamp,
        cache_idx_k[0], cache_k[0], cache_v[0],
        min(TOP_K_MAX, (past + DT) // 4))
    attn = jnp.concatenate([attn_p, attn_s.astype(BF16)], axis=0)

    o_p, ssm_p = _gdn(pf, pf, pf, conv_w[0], alog_vec, dtb_vec, gdn_norm[0],
                      jnp.zeros((1, HALO, ch), F32), jnp.zeros((1, GDN_HEADS, GDN_DK, GDN_DV), F32),
                      n_seq=1, n_chunks=T // GDN_CHUNK, C=GDN_CHUNK, valid_rows=GDN_CHUNK, out_dtype=BF16,
                      col_qkv=lay.c_qkv, col_slab=lay.c_slab, col_gz=lay.c_gz)
    halo_s = jnp.pad(state_conv[0], ((0, 0), (HALO - (CONV_WIDTH - 1), 0), (0, 0)))
    o_s, ssm_s = _gdn(_pad_tokens(qkv_s, DB, DT), _pad_tokens(pf[T:, lay.c_slab:lay.c_slab + LANES], DB, DT),
                      _pad_tokens(pf[T:, lay.c_gz:lay.c_gz + vw], DB, DT),
                      conv_w[0], alog_vec, dtb_vec, gdn_norm[0], halo_s, state_ssm[0],
                      n_seq=DB, n_chunks=1, C=SUBLANES, valid_rows=DT, out_dtype=F32)
    o_s = o_s.reshape(DB, SUBLANES, vw)[:, :DT].reshape(RS, vw)
    o_all = jnp.concatenate([o_p, o_s.astype(BF16)], axis=0)

    merged = _merge(attn, o_all, pf, w_attn_up[0], w_gdn_up[0], tm, lay.c_gate_a, lay.c_gate_b)
    x1 = _resid_mm(x, merged, w_out[0], tm, 512)
    act = _ffn_up(x1, norm_ffn[0], w_gate_up[0], tm, FF_TILE)
    x2 = _resid_mm(x1, act, w_dn, tm, FF_TILE)
    tm_y = math.gcd(math.gcd(T, RS), 512)
    y_p = _ple_final(x2, norm_ple[0], w_pg, p_all, w_pe, norm_final, 0, T, tm_y)
    y_s = _ple_final(x2, norm_ple[0], w_pg, p_all, w_pe, norm_final, T, RS, tm_y)

    def heads(a, n):
        return a.reshape(a.shape[0], n, a.shape[1] // n)

    nc = CONV_WIDTH - 1
    return (
        y_p.reshape(1, T, D),
        y_s.reshape(DB, DT, D),
        heads(k_f[:T], N_KV_HEADS)[None, None],
        heads(v_f[:T], N_KV_HEADS)[None, None],
        ki_f[:T][None, None],
        pf[T - nc:T, lay.c_qkv:lay.c_qkv + ch][None, None],
        ssm_p.astype(state_ssm.dtype)[None],
        heads(k_f[T:], N_KV_HEADS).reshape(1, DB, DT, N_KV_HEADS, HEAD_DIM),
        heads(v_f[T:], N_KV_HEADS).reshape(1, DB, DT, N_KV_HEADS, HEAD_DIM),
        ki_f[T:].reshape(1, DB, DT, IDX_DIM),
        qkv_s.reshape(DB, DT, ch)[:, DT - nc:][None],
        ssm_s.astype(state_ssm.dtype)[None],
    )
```

```python
import functools
import math

import numpy as np
import jax
import jax.numpy as jnp
from jax import lax
from jax.experimental import pallas as pl
from jax.experimental.pallas import tpu as pltpu

F32 = jnp.float32
BF16 = jnp.bfloat16

N_HEADS = 16
N_KV_HEADS = 4
HEAD_DIM = 128
GROUPS = N_HEADS // N_KV_HEADS
N_IDX_HEADS = 32
IDX_DIM = 128
TOP_K_MAX = 256
PAGE_SIZE = 128
N_BUCKETS = 32
MAX_DISTANCE = 128
GDN_HEADS = 16
GDN_DK = 128
GDN_DV = 128
CONV_WIDTH = 4
GDN_CHUNK = 64
EPS = 1e-6
NEG = -1e30
LOG2E = math.log2(math.e)

LANES = 128
SUBLANES = 8
VMEM_LIMIT = 56 * 1024 * 1024


def _cparams(*sem):
    return pltpu.CompilerParams(dimension_semantics=sem, vmem_limit_bytes=VMEM_LIMIT)


def _rms_rows(x_ref, nw_ref, h_ref, rows):
    tm = x_ref.shape[0]
    rows = math.gcd(rows, tm)
    nw = nw_ref[...]

    def body(r, _):
        sl = pl.ds(pl.multiple_of(r * rows, rows), rows)
        x = x_ref[sl, :]
        ms = jnp.mean(x * x, axis=-1, keepdims=True)
        h_ref[sl, :] = (x * lax.rsqrt(ms + EPS) * nw).astype(h_ref.dtype)
        return 0

    lax.fori_loop(0, tm // rows, body, 0)


PROJ_TN = 512


def _rmsnorm_body(x_ref, nw_ref, h_ref):
    _rms_rows(x_ref, nw_ref, h_ref, 64)


def _rmsnorm_bf16(x, norm_w, tm):
    M, D = x.shape
    return pl.pallas_call(
        _rmsnorm_body,
        grid=(M // tm,),
        in_specs=[pl.BlockSpec((tm, D), lambda i: (i, 0)), pl.BlockSpec((1, D), lambda i: (0, 0))],
        out_specs=pl.BlockSpec((tm, D), lambda i: (i, 0)),
        out_shape=jax.ShapeDtypeStruct((M, D), BF16),
        compiler_params=_cparams("parallel"),
        name="rmsnorm",
    )(x, norm_w.reshape(1, D))


def _proj_body(h_ref, wt_ref, qqi_ref, kvb_ref, pf_ref, *, steps):
    j = pl.program_id(1)
    hpt = PROJ_TN // HEAD_DIM

    def within(name):
        lo, n = steps[name]
        return (j >= lo) & (j < lo + n)

    acc = lax.dot_general(h_ref[...], wt_ref[...].astype(BF16), (((1,), (1,)), ((), ())), preferred_element_type=F32)
    is_heads = within("q") | within("qi")

    @pl.when(is_heads)
    def _():
        for hh in range(hpt):
            qqi_ref[hh] = acc[:, hh * HEAD_DIM:(hh + 1) * HEAD_DIM].astype(qqi_ref.dtype)

    @pl.when(jnp.logical_not(is_heads))
    def _():
        pf_ref[...] = acc

    @pl.when(within("kv"))
    def _():
        kvb_ref[...] = acc.astype(kvb_ref.dtype)


class ProjLayout:
    def __init__(self, d_model):
        tn = PROJ_TN
        aw, kvw, iw = N_HEADS * HEAD_DIM, N_KV_HEADS * HEAD_DIM, N_IDX_HEADS * IDX_DIM
        ch = 2 * GDN_HEADS * GDN_DK + GDN_HEADS * GDN_DV
        gw = GDN_HEADS * GDN_DV + 2 * d_model
        o_ki = aw + 2 * kvw + iw
        o_qkv = o_ki + IDX_DIM + N_IDX_HEADS
        o_ga = o_qkv + ch
        o_gz = o_ga + 2 * GDN_HEADS
        assert GB_LANE == GA_LANE + GDN_HEADS
        groups = (("q", 0, aw), ("k", aw, kvw), ("v", aw + kvw, kvw), ("qi", aw + 2 * kvw, iw),
                  ("tile_a", o_ki, tn), ("tile_b", o_ga - LANES - GA_LANE, tn), ("qkv", o_qkv, ch), ("gates", o_gz, gw))
        self.steps, self.rows, lo = {}, {}, 0
        for name, start, width in groups:
            assert start % 16 == 0 and width % tn == 0
            self.steps[name] = (lo, width // tn)
            self.rows[name] = start
            lo += width // tn
        self.n_steps = lo
        self.ch, self.gw, self.kvw = ch, gw, kvw
        self.c_qkv, self.c_gz = 0, ch
        self.c_gate_a, self.c_gate_b = ch + GDN_HEADS * GDN_DV, ch + GDN_HEADS * GDN_DV + d_model
        self.c_k = ch + gw
        self.c_v = self.c_k + kvw
        self.c_ki = self.c_v + kvw
        self.c_wi = self.c_ki + IDX_DIM
        self.c_slab = self.c_ki + tn + LANES
        self.width = self.c_ki + 2 * tn


def _proj(h, w_t, tm):
    M, D = h.shape
    tn = PROJ_TN
    lay = ProjLayout(D)
    st = lay.steps
    assert st["v"][0] == st["k"][0] + st["k"][1] and st["tile_b"][0] == st["tile_a"][0] + 1
    assert st["gates"][0] == st["qkv"][0] + st["qkv"][1] and lay.c_gz == lay.ch and lay.kvw == tn

    def w_map(i, j):
        off = jnp.int32(0)
        for name, (lo, _) in st.items():
            off = jnp.where(j >= lo, lay.rows[name] + (j - lo) * tn, off)
        return (pl.multiple_of(off, 16), 0)

    hpt = tn // HEAD_DIM
    n_qi_blk, n_q_blk = N_IDX_HEADS // hpt, N_HEADS // hpt

    def heads_map(i, j):
        q_blk = n_qi_blk + jnp.clip(j - st["q"][0], 0, n_q_blk - 1)
        qi_blk = jnp.clip(j - st["qi"][0], 0, n_qi_blk - 1)
        return (jnp.where(j < st["qi"][0], q_blk, qi_blk), i, 0)

    def pf_map(i, j):
        blk = lay.c_k // tn + jnp.clip(j - st["k"][0], 0, 1)
        blk = jnp.where(j >= st["tile_a"][0], lay.c_ki // tn + j - st["tile_a"][0], blk)
        blk = jnp.where(j >= st["qkv"][0], j - st["qkv"][0], blk)
        return (i, blk)

    steps = dict(q=st["q"], qi=st["qi"], kv=(st["k"][0], 2))
    return pl.pallas_call(
        functools.partial(_proj_body, steps=steps),
        grid=(M // tm, lay.n_steps),
        in_specs=[
            pl.BlockSpec((tm, D), lambda i, j: (i, 0), pipeline_mode=pl.Buffered(1)),
            pl.BlockSpec((pl.Element(tn), pl.Element(D)), w_map),
        ],
        out_specs=(
            pl.BlockSpec((hpt, tm, HEAD_DIM), heads_map),
            pl.BlockSpec((tm, tn), lambda i, j: (i, jnp.clip(j - st["k"][0], 0, 1))),
            pl.BlockSpec((tm, tn), pf_map),
        ),
        out_shape=(
            jax.ShapeDtypeStruct((N_IDX_HEADS + N_HEADS, M, HEAD_DIM), BF16),
            jax.ShapeDtypeStruct((M, 2 * lay.kvw), BF16),
            jax.ShapeDtypeStruct((M, lay.width), F32),
        ),
        compiler_params=_cparams("parallel", "arbitrary"),
        name="in_proj",
    )(h, w_t)


def _merge_body(attn_ref, o_ref, ga_ref, gb_ref, wa_ref, wg_ref, out_ref):
    a = jnp.dot(attn_ref[...], wa_ref[...].astype(BF16), preferred_element_type=F32)
    b = jnp.dot(o_ref[...], wg_ref[...].astype(BF16), preferred_element_type=F32)
    out_ref[...] = (jax.nn.sigmoid(ga_ref[...]) * a + jax.nn.sigmoid(gb_ref[...]) * b).astype(out_ref.dtype)


def _merge(attn, o, gates, wa, wg, tm, col_a, col_b, tn=512):
    M, KA = attn.shape
    D = wa.shape[1]
    assert col_a % tn == 0 and col_b % tn == 0
    a_off, b_off = col_a // tn, col_b // tn
    return pl.pallas_call(
        _merge_body,
        grid=(M // tm, D // tn),
        in_specs=[
            pl.BlockSpec((tm, KA), lambda i, j: (i, 0), pipeline_mode=pl.Buffered(1)),
            pl.BlockSpec((tm, KA), lambda i, j: (i, 0), pipeline_mode=pl.Buffered(1)),
            pl.BlockSpec((tm, tn), lambda i, j: (i, a_off + j)),
            pl.BlockSpec((tm, tn), lambda i, j: (i, b_off + j)),
            pl.BlockSpec((KA, tn), lambda i, j: (0, j)),
            pl.BlockSpec((KA, tn), lambda i, j: (0, j)),
        ],
        out_specs=pl.BlockSpec((tm, tn), lambda i, j: (i, j)),
        out_shape=jax.ShapeDtypeStruct((M, D), BF16),
        compiler_params=_cparams("parallel", "arbitrary"),
        name="merge",
    )(attn, o, gates, gates, wa, wg)


def _resid_mm_body(x_ref, a_ref, w_ref, out_ref):
    out_ref[...] = x_ref[...] + jnp.dot(a_ref[...], w_ref[...].astype(BF16), preferred_element_type=F32)


def _resid_mm(x, a, w, tm, tn):
    M, N = x.shape
    Kd = a.shape[1]
    return pl.pallas_call(
        _resid_mm_body,
        grid=(M // tm, N // tn),
        in_specs=[
            pl.BlockSpec((tm, tn), lambda i, j: (i, j)),
            pl.BlockSpec((tm, Kd), lambda i, j: (i, 0), pipeline_mode=pl.Buffered(1)),
            pl.BlockSpec((Kd, tn), lambda i, j: (0, j)),
        ],
        out_specs=pl.BlockSpec((tm, tn), lambda i, j: (i, j)),
        out_shape=jax.ShapeDtypeStruct((M, N), F32),
        compiler_params=_cparams("parallel", "arbitrary"),
        name="resid_mm",
    )(x, a, w)


def _ffn_up_body(x_ref, nw_ref, wg_ref, wu_ref, out_ref, h_ref):
    @pl.when(pl.program_id(1) == 0)
    def _():
        _rms_rows(x_ref, nw_ref, h_ref, 64)

    h = h_ref[...]
    g = jnp.dot(h, wg_ref[...].astype(BF16), preferred_element_type=F32)
    u = jnp.dot(h, wu_ref[...].astype(BF16), preferred_element_type=F32)
    out_ref[...] = (jax.nn.silu(g) * u).astype(out_ref.dtype)


def _ffn_up(x, norm_w, w_gu, tm, tn):
    M, D = x.shape
    ffp = w_gu.shape[1] // 2
    nj = ffp // tn
    return pl.pallas_call(
        _ffn_up_body,
        grid=(M // tm, nj),
        in_specs=[
            pl.BlockSpec((tm, D), lambda i, j: (i, 0), pipeline_mode=pl.Buffered(1)),
            pl.BlockSpec((1, D), lambda i, j: (0, 0)),
            pl.BlockSpec((D, tn), lambda i, j: (0, j)),
            pl.BlockSpec((D, tn), lambda i, j: (0, nj + j)),
        ],
        out_specs=pl.BlockSpec((tm, tn), lambda i, j: (i, j)),
        out_shape=jax.ShapeDtypeStruct((M, ffp), BF16),
        scratch_shapes=[pltpu.VMEM((tm, D), BF16)],
        compiler_params=_cparams("parallel", "arbitrary"),
        name="ffn_up",
    )(x, norm_w.reshape(1, D), w_gu, w_gu)


def _ple_body(x_ref, nw_ref, wg_ref, p_ref, wp_ref, nf_ref, y_ref, h_ref, *, tn):
    j = pl.program_id(1)

    @pl.when(j == 0)
    def _():
        _rms_rows(x_ref, nw_ref, h_ref, 64)

    g = jnp.dot(h_ref[...], wg_ref[...].astype(BF16), preferred_element_type=F32)
    e = jnp.dot(p_ref[...], wp_ref[...].astype(BF16), preferred_element_type=F32)
    cols = pl.ds(pl.multiple_of(j * tn, tn), tn)
    y_ref[:, cols] = x_ref[:, cols] + jax.nn.sigmoid(g) * e

    @pl.when(j == pl.num_programs(1) - 1)
    def _():
        _rms_rows(y_ref, nf_ref, y_ref, 64)


def _ple_final(x, norm_w, w_gate, p, w_ple, norm_final, row0, n_rows, tm, tn=512):
    D = x.shape[1]
    P = p.shape[1]
    assert row0 % tm == 0 and n_rows % tm == 0
    r0 = row0 // tm
    return pl.pallas_call(
        functools.partial(_ple_body, tn=tn),
        grid=(n_rows // tm, D // tn),
        in_specs=[
            pl.BlockSpec((tm, D), lambda i, j: (r0 + i, 0), pipeline_mode=pl.Buffered(1)),
            pl.BlockSpec((1, D), lambda i, j: (0, 0)),
            pl.BlockSpec((D, tn), lambda i, j: (0, j)),
            pl.BlockSpec((tm, P), lambda i, j: (r0 + i, 0)),
            pl.BlockSpec((P, tn), lambda i, j: (0, j)),
            pl.BlockSpec((1, D), lambda i, j: (0, 0)),
        ],
        out_specs=pl.BlockSpec((tm, D), lambda i, j: (i, 0)),
        out_shape=jax.ShapeDtypeStruct((n_rows, D), F32),
        scratch_shapes=[pltpu.VMEM((tm, D), BF16)],
        compiler_params=_cparams("parallel", "arbitrary"),
        name="ple_final",
    )(x, norm_w.reshape(1, D), w_gate, p, w_ple, norm_final.reshape(1, D))


def _bucket_thresholds():
    d = np.arange(0, 4 * MAX_DISTANCE)
    max_exact = N_BUCKETS // 2
    large = max_exact + (np.log(np.maximum(d, 1) / max_exact) / math.log(MAX_DISTANCE / max_exact)
                         * (N_BUCKETS - max_exact)).astype(np.int32)
    b = np.where(d < max_exact, d, np.minimum(large, N_BUCKETS - 1))
    return [int(np.argmax(b >= k)) for k in range(N_BUCKETS)]


_BUCKET_THR = _bucket_thresholds()


def _bias_of_dist(rb_ref, h, d):
    v = jnp.full(d.shape, rb_ref[0, h], F32)
    for b in range(1, N_BUCKETS):
        v = jnp.where(d >= _BUCKET_THR[b], rb_ref[b, h], v)
    return v


def _bias_tables_body(rb_ref, tprev_ref, tdiag_ref, tsamp_ref, *, qb, past):
    h = pl.program_id(0)
    r = lax.broadcasted_iota(jnp.int32, (qb, qb), 0)
    c = lax.broadcasted_iota(jnp.int32, (qb, qb), 1)
    far = rb_ref[N_BUCKETS - 1, h]
    tprev_ref[0] = (_bias_of_dist(rb_ref, h, r + qb - c) - far) * LOG2E
    tdiag_ref[0] = (_bias_of_dist(rb_ref, h, r - c) - far) * LOG2E
    ls = tsamp_ref.shape[2]
    t = lax.broadcasted_iota(jnp.int32, (SUBLANES, ls), 0) % 4
    lane = lax.broadcasted_iota(jnp.int32, (SUBLANES, ls), 1)
    d = jnp.where(lane < past, past + t - lane, t - (lane - past) % 4)
    tsamp_ref[0] = _bias_of_dist(rb_ref, h, d)


def _bias_tables(rel_bias, qb, past, ls):
    return pl.pallas_call(
        functools.partial(_bias_tables_body, qb=qb, past=past),
        grid=(N_HEADS,),
        in_specs=[pl.BlockSpec(memory_space=pltpu.SMEM)],
        out_specs=(
            pl.BlockSpec((1, qb, qb), lambda h: (h, 0, 0)),
            pl.BlockSpec((1, qb, qb), lambda h: (h, 0, 0)),
            pl.BlockSpec((1, SUBLANES, ls), lambda h: (h, 0, 0)),
        ),
        out_shape=(
            jax.ShapeDtypeStruct((N_HEADS, qb, qb), F32),
            jax.ShapeDtypeStruct((N_HEADS, qb, qb), F32),
            jax.ShapeDtypeStruct((N_HEADS, SUBLANES, ls), F32),
        ),
        compiler_params=_cparams("arbitrary"),
        name="bias_tables",
    )(rel_bias)


INT_MIN = -2 ** 31


def _sort_key(score, valid):
    bits = lax.bitcast_convert_type(score, jnp.int32)
    bits = jnp.where(bits == jnp.int32(INT_MIN), 0, bits)
    key = jnp.where(bits < 0, bits ^ jnp.int32(0x7FFFFFFF), bits)
    return jnp.where(valid, key, jnp.int32(INT_MIN))


def _ties_needed(count_ge, thr, k):
    above = jnp.where(thr[:, :1] < jnp.int32(2 ** 31 - 1), count_ge(jnp.minimum(thr, jnp.int32(2 ** 31 - 2)) + 1), 0.0)
    return k - above


def _prefix_matrix(n):
    j = lax.broadcasted_iota(jnp.int32, (n, 2 * n), 0)
    l = lax.broadcasted_iota(jnp.int32, (n, 2 * n), 1)
    return jnp.where((j <= l) | (l >= n), 1.0, 0.0).astype(BF16)


def _kth_largest(count_ge, rows, k):
    def step(b, t):
        cand = t + lax.shift_left(jnp.int32(1), jnp.asarray(31 - b, jnp.int32))
        n = count_ge(cand)
        return jnp.where(n >= k, cand, t)

    return lax.fori_loop(0, 32, step, jnp.full((rows, LANES), INT_MIN, jnp.int32))


PQB = 256
PKC = 256
FAR_GROUP = 2

def _pattn_body(qi_ref, wi_ref, ki_ref, q_ref, k_ref, v_ref, tprev_ref, tdiag_ref, out_ref,
                key_ref, wb_ref, m_ref, l_ref, acc_ref, *, idx_scale, n_sel):
    i = pl.program_id(0)
    qb, kc = PQB, PKC
    hg = 8

    wi = wi_ref[...] * idx_scale
    for h in range(N_IDX_HEADS):
        wb_ref[h] = jnp.broadcast_to(wi[:, h:h + 1], (qb, LANES))

    def chunk(c):
        return pl.ds(pl.multiple_of(c * kc, kc), kc)

    row = lax.broadcasted_iota(jnp.int32, (qb, kc), 0)
    col = lax.broadcasted_iota(jnp.int32, (qb, kc), 1)

    def score_chunk(c, _):
        kic = ki_ref[chunk(c), :]
        acc = [jnp.zeros((qb, LANES), F32) for _ in range(kc // LANES)]
        for g in range(N_IDX_HEADS // hg):
            qg = qi_ref[g * hg:(g + 1) * hg].reshape(hg * qb, IDX_DIM)
            s = lax.dot_general(qg, kic, (((1,), (1,)), ((), ())), preferred_element_type=F32)
            for hh in range(hg):
                w = wb_ref[g * hg + hh]
                for half in range(kc // LANES):
                    sh = s[hh * qb:(hh + 1) * qb, half * LANES:(half + 1) * LANES]
                    acc[half] = acc[half] + jnp.maximum(sh, 0.0) * w
        score = jnp.concatenate(acc, axis=1)
        valid = (c * kc + col) <= (i * qb + row)
        key_ref[:, chunk(c)] = _sort_key(score, valid)
        return 0

    lax.fori_loop(0, i + 1, score_chunk, 0)

    def count_ge(cand):
        def body(c, cnt):
            kk = key_ref[:, chunk(c)]
            for half in range(kc // LANES):
                cnt = cnt + jnp.where(kk[:, half * LANES:(half + 1) * LANES] >= cand, 1.0, 0.0)
            return cnt

        cnt = lax.fori_loop(0, i + 1, body, jnp.zeros((qb, LANES), F32))
        return jnp.sum(cnt, axis=1, keepdims=True)

    thr = _kth_largest(count_ge, qb, float(n_sel))
    t2 = jnp.concatenate([thr] * (kc // LANES), axis=1)
    surplus = jnp.max(count_ge(thr)) > float(n_sel)

    def store_mask(c, sel):
        sel = sel & (key_ref[:, chunk(c)] > jnp.int32(INT_MIN))
        key_ref[:, chunk(c)] = lax.bitcast_convert_type(jnp.where(sel, 0.0, NEG).astype(F32), jnp.int32)

    @pl.when(jnp.logical_not(surplus))
    def _():
        def mask_chunk(c, _):
            store_mask(c, key_ref[:, chunk(c)] >= t2)
            return 0

        lax.fori_loop(0, i + 1, mask_chunk, 0)

    @pl.when(surplus)
    def _():
        need = _ties_needed(count_ge, thr, float(n_sel))
        prefix = _prefix_matrix(kc)

        def mask_chunk(c, seen):
            kk = key_ref[:, chunk(c)]
            eq = kk == t2
            cnt = jnp.dot(jnp.where(eq, 1.0, 0.0).astype(BF16), prefix, preferred_element_type=F32)
            rank = cnt[:, :kc] + jnp.concatenate([seen] * (kc // LANES), axis=1)
            store_mask(c, (kk > t2) | (eq & (rank <= need)))
            return seen + cnt[:, kc:kc + LANES]

        lax.fori_loop(0, i + 1, mask_chunk, jnp.zeros((qb, LANES), F32))

    c1 = HEAD_DIM ** -0.5 * LOG2E

    def kv_head(kh, _):
        qs = q_ref[pl.ds(kh * GROUPS, GROUPS)].reshape(GROUPS * qb, HEAD_DIM)
        lanes = pl.ds(pl.multiple_of(kh * HEAD_DIM, HEAD_DIM), HEAD_DIM)
        m_ref[...] = jnp.full(m_ref.shape, NEG, F32)
        l_ref[...] = jnp.zeros(l_ref.shape, F32)
        acc_ref[...] = jnp.zeros(acc_ref.shape, F32)

        def attend(items):
            ss = [lax.dot_general(qs, k_ref[chunk(c), lanes], (((1,), (1,)), ((), ())), preferred_element_type=F32)
                  for c, _ in items]
            m, l, acc = m_ref[...], l_ref[...], acc_ref[...]
            for (c, bias_of_group), s in zip(items, ss):
                vt = v_ref[chunk(c), lanes]
                vx = jnp.concatenate([vt, jnp.ones_like(vt)], axis=1)
                mb = lax.bitcast_convert_type(key_ref[:, chunk(c)], F32)
                ts = []
                for g in range(GROUPS):
                    t = s[g * qb:(g + 1) * qb] * c1 + mb
                    if bias_of_group is not None:
                        t = t + bias_of_group(g)
                    ts.append(t)
                t = jnp.concatenate(ts, axis=0)
                m_new = jnp.maximum(m, jnp.max(t, axis=1, keepdims=True))
                alpha = jnp.exp2(m - m_new)
                p = jnp.exp2(t - jnp.concatenate([m_new] * (kc // LANES), axis=1))
                pv = jnp.dot(p.astype(BF16), vx, preferred_element_type=F32)
                acc = alpha * acc + pv[:, :HEAD_DIM]
                l = alpha * l + pv[:, HEAD_DIM:]
                m = m_new
            m_ref[...], l_ref[...], acc_ref[...] = m, l, acc

        n_far = jnp.maximum(i - 1, 0)

        def far_group(cc, _):
            attend([(FAR_GROUP * cc + u, None) for u in range(FAR_GROUP)])
            return 0

        def far_single(c, _):
            attend([(c, None)])
            return 0

        n_grouped = n_far // FAR_GROUP * FAR_GROUP
        lax.fori_loop(0, n_far // FAR_GROUP, far_group, 0)
        lax.fori_loop(n_grouped, n_far, far_single, 0)

        @pl.when(i >= 1)
        def _():
            attend([(i - 1, lambda g: tprev_ref[kh * GROUPS + g]), (i, lambda g: tdiag_ref[kh * GROUPS + g])])

        @pl.when(i == 0)
        def _():
            attend([(i, lambda g: tdiag_ref[kh * GROUPS + g])])

        o = acc_ref[...] / l_ref[...]
        for g in range(GROUPS):
            out_ref[:, pl.ds(pl.multiple_of((kh * GROUPS + g) * HEAD_DIM, HEAD_DIM), HEAD_DIM)] = (
                o[g * qb:(g + 1) * qb].astype(out_ref.dtype))
        return 0

    lax.fori_loop(0, N_KV_HEADS, kv_head, 0)


def _prompt_attention(qqi, wi, ki_bf, kv_bf, tprev, tdiag, seq, n_sel):
    qb = PQB
    assert N_IDX_HEADS % N_HEADS == 0
    resident = dict(pipeline_mode=pl.Buffered(1))
    return pl.pallas_call(
        functools.partial(_pattn_body, idx_scale=(N_IDX_HEADS * IDX_DIM) ** -0.5, n_sel=n_sel),
        grid=(seq // qb,),
        in_specs=[
            pl.BlockSpec((N_IDX_HEADS, qb, IDX_DIM), lambda i: (0, i, 0)),
            pl.BlockSpec((qb, N_IDX_HEADS), lambda i: (i, 0)),
            pl.BlockSpec((seq, IDX_DIM), lambda i: (0, 0), **resident),
            pl.BlockSpec((N_HEADS, qb, HEAD_DIM), lambda i: (N_IDX_HEADS // N_HEADS, i, 0)),
            pl.BlockSpec((seq, N_KV_HEADS * HEAD_DIM), lambda i: (0, 0), **resident),
            pl.BlockSpec((seq, N_KV_HEADS * HEAD_DIM), lambda i: (0, 1), **resident),
            pl.BlockSpec((N_HEADS, qb, qb), lambda i: (0, 0, 0), **resident),
            pl.BlockSpec((N_HEADS, qb, qb), lambda i: (0, 0, 0), **resident),
        ],
        out_specs=pl.BlockSpec((qb, N_HEADS * HEAD_DIM), lambda i: (i, 0)),
        out_shape=jax.ShapeDtypeStruct((seq, N_HEADS * HEAD_DIM), BF16),
        scratch_shapes=[
            pltpu.VMEM((qb, seq), jnp.int32),
            pltpu.VMEM((N_IDX_HEADS, qb, LANES), F32),
            pltpu.VMEM((GROUPS * qb, LANES), F32),
            pltpu.VMEM((GROUPS * qb, LANES), F32),
            pltpu.VMEM((GROUPS * qb, HEAD_DIM), F32),
        ],
        compiler_params=_cparams("arbitrary"),
        name="prompt_attention",
    )(qqi, wi, ki_bf, qqi, kv_bf, kv_bf, tprev, tdiag)


GA_LANE = 32
GB_LANE = 48
HALO = SUBLANES
def _split2(a):
    hi = a.astype(BF16)
    return hi, (a - hi.astype(F32)).astype(BF16)


def _split3(a):
    hi = a.astype(BF16)
    r = a - hi.astype(F32)
    mid = r.astype(BF16)
    return hi, mid, (r - mid.astype(F32)).astype(BF16)


def _mm3(a2, b2):
    (ah, al), (bh, bl) = a2, b2
    d = lambda x, y: jnp.dot(x, y, preferred_element_type=F32)
    return d(ah, bh) + (d(ah, bl) + d(al, bh))


def _bmm(a, b):
    return jnp.dot(a.astype(BF16), b.astype(BF16), preferred_element_type=F32)


def _bmm_nt(a, b):
    return lax.dot_general(a.astype(BF16), b.astype(BF16), (((1,), (1,)), ((), ())), preferred_element_type=F32)


def _bmm_tn(a, b):
    return lax.dot_general(a.astype(BF16), b.astype(BF16), (((0,), (0,)), ((), ())), preferred_element_type=F32)


def _gdn_body(qkv_ref, slab_ref, gz_ref, cw_ref, alog_ref, dtb_ref, gn_ref, halo0_ref, s0_ref,
              o_ref, s_ref, xp_ref, *, C, valid_rows):
    c = pl.program_id(1)
    kw, vw = GDN_HEADS * GDN_DK, GDN_HEADS * GDN_DV

    @pl.when(c == 0)
    def _():
        xp_ref[0:HALO] = halo0_ref[0]
        s_ref[...] = s0_ref[...]

    xp_ref[HALO:HALO + C] = qkv_ref[...]

    ri = lax.broadcasted_iota(jnp.int32, (C, C), 0)
    ci = lax.broadcasted_iota(jnp.int32, (C, C), 1)
    causal = ri >= ci
    strict = ri > ci
    eye = jnp.where(ri == ci, 1.0, 0.0).astype(F32)
    ltri = jnp.where(causal, 1.0, 0.0).astype(F32)

    slab = slab_ref[...]
    live = lax.broadcasted_iota(jnp.int32, slab.shape, 0) < valid_rows
    g_all = jnp.where(live, -jnp.exp(alog_ref[...]) * jax.nn.softplus(slab + dtb_ref[...]), 0.0)
    beta_all = jnp.where(live, jax.nn.sigmoid(slab), 0.0)
    ltri_b = ltri.astype(BF16)
    gc_all = sum(jnp.dot(ltri_b, part, preferred_element_type=F32) for part in _split3(g_all))
    pad = jnp.zeros((LANES - C, LANES), F32)
    gc_t = jnp.concatenate([gc_all, pad], axis=0).T

    def conv(cols):
        y = cw_ref[0:1, cols] * xp_ref[HALO - 3:HALO - 3 + C, cols]
        for j in range(1, CONV_WIDTH):
            y = y + cw_ref[j:j + 1, cols] * xp_ref[HALO - 3 + j:HALO - 3 + j + C, cols]
        return jax.nn.silu(y)

    def l2n(x):
        return x * lax.rsqrt(jnp.sum(x * x, axis=-1, keepdims=True) + 1e-6)

    n_sq = max(1, (C - 1).bit_length() - 1)

    hs = range(GDN_HEADS)
    q = [l2n(conv(slice(h * GDN_DK, (h + 1) * GDN_DK))) * GDN_DK ** -0.5 for h in hs]
    k = [l2n(conv(slice(kw + h * GDN_DK, kw + (h + 1) * GDN_DK))) for h in hs]
    v = [conv(slice(2 * kw + h * GDN_DV, 2 * kw + (h + 1) * GDN_DV)) for h in hs]
    beta = [beta_all[:, GB_LANE + h:GB_LANE + h + 1] for h in hs]
    gc = [gc_all[:, GA_LANE + h:GA_LANE + h + 1] for h in hs]
    gc_row = [gc_t[GA_LANE + h:GA_LANE + h + 1, 0:C] for h in hs]
    gc_last = [gc_all[C - 1:C, GA_LANE + h:GA_LANE + h + 1] for h in hs]
    decay = [jnp.exp(jnp.where(causal, gc[h] - gc_row[h], NEG)) for h in hs]
    kb = [k[h] * beta[h] for h in hs]
    p = [-jnp.where(strict, _bmm_nt(kb[h], k[h]) * decay[h], 0.0) for h in hs]
    t_inv = [eye + p[h] for h in hs]
    p2 = [_split2(x) for x in p]
    for _ in range(n_sq):
        p = [_mm3(x2, x2) for x2 in p2]
        p2 = [_split2(x) for x in p]
        t_inv = [t + _mm3(_split2(t), x2) for t, x2 in zip(t_inv, p2)]
    egc = [jnp.exp(gc[h]) for h in hs]
    u = [_bmm(t_inv[h], v[h] * beta[h]) for h in hs]
    w = [_bmm(t_inv[h], kb[h] * egc[h]) for h in hs]
    a_in = [jnp.where(causal, _bmm_nt(q[h], k[h]) * decay[h], 0.0) for h in hs]
    qd = [q[h] * egc[h] for h in hs]
    kt = [k[h] * jnp.exp(gc_last[h] - gc[h]) for h in hs]
    s_old = [s_ref[0, h] for h in hs]
    v_new = [u[h] - _bmm(w[h], s_old[h]) for h in hs]
    o = [_bmm(qd[h], s_old[h]) + _bmm(a_in[h], v_new[h]) for h in hs]
    for h in hs:
        s_ref[0, h] = s_old[h] * jnp.exp(gc_last[h]) + _bmm_tn(kt[h], v_new[h])
    for h in hs:
        on = o[h] * lax.rsqrt(jnp.mean(o[h] * o[h], axis=-1, keepdims=True) + EPS) * gn_ref[...]
        gz = gz_ref[:, h * GDN_DV:(h + 1) * GDN_DV]
        o_ref[:, h * GDN_DV:(h + 1) * GDN_DV] = (on * jax.nn.silu(gz)).astype(o_ref.dtype)

    xp_ref[0:HALO] = xp_ref[C:C + HALO]


def _gdn(qkv, misc, gates, conv_w, alog_vec, dtb_vec, gdn_norm, halo0, s0, *, n_seq, n_chunks, C, valid_rows,
         out_dtype, col_qkv=0, col_slab=None, col_gz=0):
    ch = conv_w.shape[1]
    vw = GDN_HEADS * GDN_DV
    rows = n_seq * n_chunks * C
    col_slab = misc.shape[1] - LANES if col_slab is None else col_slab
    assert col_qkv % ch == 0 and col_slab % LANES == 0 and col_gz % vw == 0

    def rmap(s, c):
        return (s * n_chunks + c, 0)

    def at_col(blk):
        return lambda s, c: (s * n_chunks + c, blk)

    return pl.pallas_call(
        functools.partial(_gdn_body, C=C, valid_rows=valid_rows),
        grid=(n_seq, n_chunks),
        in_specs=[
            pl.BlockSpec((C, ch), at_col(col_qkv // ch)),
            pl.BlockSpec((C, LANES), at_col(col_slab // LANES)),
            pl.BlockSpec((C, vw), at_col(col_gz // vw)),
            pl.BlockSpec((CONV_WIDTH, ch), lambda s, c: (0, 0)),
            pl.BlockSpec((1, LANES), lambda s, c: (0, 0)),
            pl.BlockSpec((1, LANES), lambda s, c: (0, 0)),
            pl.BlockSpec((1, GDN_DV), lambda s, c: (0, 0)),
            pl.BlockSpec((1, HALO, ch), lambda s, c: (s, 0, 0)),
            pl.BlockSpec((1, GDN_HEADS, GDN_DK, GDN_DV), lambda s, c: (s, 0, 0, 0)),
        ],
        out_specs=(
            pl.BlockSpec((C, vw), rmap),
            pl.BlockSpec((1, GDN_HEADS, GDN_DK, GDN_DV), lambda s, c: (s, 0, 0, 0)),
        ),
        out_shape=(
            jax.ShapeDtypeStruct((rows, vw), out_dtype),
            jax.ShapeDtypeStruct((n_seq, GDN_HEADS, GDN_DK, GDN_DV), F32),
        ),
        scratch_shapes=[pltpu.VMEM((C + HALO, ch), F32)],
        compiler_params=_cparams("arbitrary", "arbitrary"),
        name=f"gdn_c{C}",
    )(qkv, misc, gates, conv_w, alog_vec, dtb_vec, gdn_norm.reshape(1, GDN_DV), halo0, s0)


SB = 2
ST = SUBLANES // SB


def _sattn_body(pt_ref, qi_ref, wi_ref, q_ref, kin_ref, kn_ref, vn_ref, tsamp_ref, cidx_hbm, ck_hbm, cv_hbm, out_ref,
                kibuf, kbuf, vbuf, sem, *, past, n_pages, n_sel, idx_scale):
    p = pl.program_id(0)
    n_steps = pl.num_programs(0)
    ls = kibuf.shape[1]
    rows = SB * ST

    def slot_of(step, bl):
        return (step % 2) * SB + bl

    def copies(step, bl):
        b = step * SB + bl
        sl = slot_of(step, bl)
        out = []
        for j in range(n_pages):
            pg = pt_ref[b, j]
            dst = pl.ds(j * PAGE_SIZE, PAGE_SIZE)
            out.append(pltpu.make_async_copy(cidx_hbm.at[pg], kibuf.at[sl, dst], sem.at[sl]))
            for kh in range(N_KV_HEADS):
                out.append(pltpu.make_async_copy(ck_hbm.at[pg, :, kh], kbuf.at[sl, kh, dst], sem.at[sl]))
                out.append(pltpu.make_async_copy(cv_hbm.at[pg, :, kh], vbuf.at[sl, kh, dst], sem.at[sl]))
        return out

    @pl.when(p == 0)
    def _():
        for bl in range(SB):
            for cp in copies(0, bl):
                cp.start()
        for sl in range(2 * SB):
            kibuf[sl, past:ls] = jnp.zeros((ls - past, IDX_DIM), F32)
            kbuf[sl, :, past:ls] = jnp.zeros((N_KV_HEADS, ls - past, HEAD_DIM), F32)
            vbuf[sl, :, past:ls] = jnp.zeros((N_KV_HEADS, ls - past, HEAD_DIM), F32)

    @pl.when(p + 1 < n_steps)
    def _():
        for bl in range(SB):
            for cp in copies(p + 1, bl):
                cp.start()

    slots = [slot_of(p, bl) for bl in range(SB)]
    bt = lax.broadcasted_iota(jnp.int32, (rows, ls), 0)
    lane = lax.broadcasted_iota(jnp.int32, (rows, ls), 1)
    r_new = lane - past
    valid = (lane < past) | ((r_new < rows) & (r_new // ST == bt // ST) & (r_new % ST <= bt % ST))

    for sl in slots:
        kibuf[sl, past:past + rows] = kin_ref[...]
        for kh in range(N_KV_HEADS):
            kbuf[sl, kh, past:past + rows] = kn_ref[:, kh * HEAD_DIM:(kh + 1) * HEAD_DIM]
            vbuf[sl, kh, past:past + rows] = vn_ref[:, kh * HEAD_DIM:(kh + 1) * HEAD_DIM]

    wi = wi_ref[...] * idx_scale
    qi = qi_ref[...].reshape(N_IDX_HEADS * rows, IDX_DIM).astype(BF16)
    key_chunks = [(s, min(512, ls - s)) for s in range(0, ls, 512)]

    def scores(bl):
        parts = []
        for s0, n in key_chunks:
            kic = kibuf[slots[bl], s0:s0 + n].astype(BF16)
            s = lax.dot_general(qi, kic, (((1,), (1,)), ((), ())), preferred_element_type=F32)
            acc = jnp.zeros((rows, n), F32)
            for h in range(N_IDX_HEADS):
                acc = acc + jnp.maximum(s[h * rows:(h + 1) * rows], 0.0) * wi[:, h:h + 1]
            parts.append(acc)
        return jnp.concatenate(parts, axis=1)

    sc = []
    for bl in range(SB):
        for cp in copies(p, bl):
            cp.wait()
        sc.append(scores(bl))
    score = sc[0]
    for bl in range(1, SB):
        score = jnp.where(bt // ST == bl, sc[bl], score)
    key = _sort_key(score, valid)

    def count_ge(cand):
        cnt = jnp.zeros((rows, LANES), F32)
        for t in range(ls // LANES):
            cnt = cnt + jnp.where(key[:, t * LANES:(t + 1) * LANES] >= cand, 1.0, 0.0)
        return jnp.sum(cnt, axis=1, keepdims=True)

    thr = _kth_largest(count_ge, rows, float(n_sel))
    need = _ties_needed(count_ge, thr, float(n_sel))
    nt = ls // LANES
    thr_full = jnp.concatenate([thr] * nt, axis=1)
    eq = key == thr_full
    tiles = [jnp.where(eq[:, t * LANES:(t + 1) * LANES], 1.0, 0.0) for t in range(nt)]
    tiles.append(jnp.zeros((rows, LANES), F32))
    cnt = jnp.dot(jnp.concatenate(tiles, axis=0).astype(BF16), _prefix_matrix(LANES), preferred_element_type=F32)
    seen, ranks = jnp.zeros((rows, LANES), F32), []
    for t in range(nt):
        blk = cnt[t * rows:(t + 1) * rows]
        ranks.append(blk[:, :LANES] + seen)
        seen = seen + blk[:, LANES:]
    take = eq & (jnp.concatenate(ranks, axis=1) <= need)
    mb = jnp.where(((key > thr_full) | take) & valid, 0.0, NEG).astype(F32)

    scale = HEAD_DIM ** -0.5
    row_b = lax.broadcasted_iota(jnp.int32, (GROUPS * rows, HEAD_DIM), 0) % rows // ST
    outs = [None] * N_KV_HEADS
    for bl in range(SB):
        for kh in range(N_KV_HEADS):
            qs = q_ref[kh * GROUPS:(kh + 1) * GROUPS].reshape(GROUPS * rows, HEAD_DIM).astype(BF16)
            kt = kbuf[slots[bl], kh].astype(BF16)
            vt = vbuf[slots[bl], kh].astype(BF16)
            s = lax.dot_general(qs, kt, (((1,), (1,)), ((), ())), preferred_element_type=F32) * scale
            s = (s.reshape(GROUPS, rows, ls) + tsamp_ref[kh * GROUPS:(kh + 1) * GROUPS] + mb[None]).reshape(GROUPS * rows, ls)
            m = jnp.max(s, axis=1, keepdims=True)
            e = jnp.exp(s - m)
            o = jnp.dot(e.astype(BF16), vt, preferred_element_type=F32) / jnp.sum(e, axis=1, keepdims=True)
            outs[kh] = o if bl == 0 else jnp.where(row_b == bl, o, outs[kh])

    for kh in range(N_KV_HEADS):
        for g in range(GROUPS):
            hd = kh * GROUPS + g
            out_ref[:, hd * HEAD_DIM:(hd + 1) * HEAD_DIM] = outs[kh][g * rows:(g + 1) * rows]


def _sample_attention(page_table, qi_s, wi_s, q_s, ki_new, k_new, v_new, tsamp, cache_idx, cache_k, cache_v, n_sel):
    R = wi_s.shape[0]
    n_b, n_pages = page_table.shape
    past = n_pages * PAGE_SIZE
    ls = past + PAGE_SIZE
    rows = SB * ST
    assert R == n_b * ST and n_b % SB == 0
    kvw = N_KV_HEADS * HEAD_DIM
    grid_spec = pltpu.PrefetchScalarGridSpec(
        num_scalar_prefetch=1,
        grid=(n_b // SB,),
        in_specs=[
            pl.BlockSpec((N_IDX_HEADS, rows, IDX_DIM), lambda p, pt: (0, p, 0)),
            pl.BlockSpec((rows, N_IDX_HEADS), lambda p, pt: (p, 0)),
            pl.BlockSpec((N_HEADS, rows, HEAD_DIM), lambda p, pt: (0, p, 0)),
            pl.BlockSpec((rows, IDX_DIM), lambda p, pt: (p, 0)),
            pl.BlockSpec((rows, kvw), lambda p, pt: (p, 0)),
            pl.BlockSpec((rows, kvw), lambda p, pt: (p, 0)),
            pl.BlockSpec((N_HEADS, SUBLANES, ls), lambda p, pt: (0, 0, 0)),
            pl.BlockSpec(memory_space=pl.ANY),
            pl.BlockSpec(memory_space=pl.ANY),
            pl.BlockSpec(memory_space=pl.ANY),
        ],
        out_specs=pl.BlockSpec((rows, N_HEADS * HEAD_DIM), lambda p, pt: (p, 0)),
        scratch_shapes=[
            pltpu.VMEM((2 * SB, ls, IDX_DIM), F32),
            pltpu.VMEM((2 * SB, N_KV_HEADS, ls, HEAD_DIM), F32),
            pltpu.VMEM((2 * SB, N_KV_HEADS, ls, HEAD_DIM), F32),
            pltpu.SemaphoreType.DMA((2 * SB,)),
        ],
    )
    return pl.pallas_call(
        functools.partial(_sattn_body, past=past, n_pages=n_pages, n_sel=n_sel,
                          idx_scale=(N_IDX_HEADS * IDX_DIM) ** -0.5),
        grid_spec=grid_spec,
        out_shape=jax.ShapeDtypeStruct((R, N_HEADS * HEAD_DIM), F32),
        compiler_params=_cparams("arbitrary"),
        name="sample_attention",
    )(page_table, qi_s, wi_s, q_s, ki_new, k_new, v_new, tsamp, cache_idx, cache_k, cache_v)


ROW_TILE_CAP = 1088
FF_TILE = 256


def _row_tile(m, cap=ROW_TILE_CAP):
    return max(t for t in range(16, cap + 1, 16) if m % t == 0)


def _pad_tokens(a, db, dt):
    a = a.reshape(db, dt, a.shape[-1])
    return jnp.pad(a, ((0, 0), (0, SUBLANES - dt), (0, 0))).reshape(db * SUBLANES, a.shape[-1])


def kernel(x_prompt, x_sample, cache_k, cache_v, cache_idx_k, state_conv, state_ssm, page_table, p_prompt, p_sample,
           rel_bias, norm_mix, w_in, conv_w, a_log, dt_bias, gdn_norm, w_attn_up, w_gdn_up, w_out, norm_ffn, w_gate_up,
           w_down, norm_ple, w_ple_gate, w_ple, norm_final):
    assert x_prompt.shape[0] == 1 and w_in.shape[0] == 1, "one prompt sequence, one layer"
    _, T, D = x_prompt.shape
    DB, DT, _ = x_sample.shape
    assert DT == ST and T % PQB == 0 and T % GDN_CHUNK == 0 and DT >= CONV_WIDTH - 1
    n_pool = cache_k.shape[1]
    past = page_table.shape[1] * PAGE_SIZE
    RS = DB * DT
    M = T + RS
    tm = _row_tile(M)
    kw, vw = GDN_HEADS * GDN_DK, GDN_HEADS * GDN_DV
    ch = 2 * kw + vw
    aw, kvw, iw = N_HEADS * HEAD_DIM, N_KV_HEADS * HEAD_DIM, N_IDX_HEADS * IDX_DIM

    w_t = jnp.swapaxes(w_in[0], 0, 1)
    w_dn = w_down[0].astype(BF16)
    w_pg = w_ple_gate[0].astype(BF16)
    w_pe = w_ple[0].astype(BF16)
    lay = ProjLayout(D)
    alog_vec = jnp.zeros((1, LANES), F32).at[0, GA_LANE:GA_LANE + GDN_HEADS].set(a_log[0])
    dtb_vec = jnp.zeros((1, LANES), F32).at[0, GA_LANE:GA_LANE + GDN_HEADS].set(dt_bias[0])

    x = jnp.concatenate([x_prompt[0], x_sample.reshape(RS, D)], axis=0)
    p_all = jnp.concatenate([p_prompt[0, 0], p_sample[0].reshape(RS, -1)], axis=0).astype(BF16)

    h0 = _rmsnorm_bf16(x, norm_mix[0], _row_tile(M, 512))
    qqi, kv_bf, pf = _proj(h0, w_t, tm)
    k_f, v_f = pf[:, lay.c_k:lay.c_k + kvw], pf[:, lay.c_v:lay.c_v + kvw]
    ki_f = pf[:, lay.c_ki:lay.c_ki + IDX_DIM]
    wi_f = pf[:, lay.c_wi:lay.c_wi + N_IDX_HEADS]
    qkv_s = pf[T:, lay.c_qkv:lay.c_qkv + ch]

    tprev, tdiag, tsamp = _bias_tables(rel_bias, PQB, past, past + PAGE_SIZE)
    attn_p = _prompt_attention(qqi, wi_f, ki_f.astype(BF16), kv_bf, tprev, tdiag, T, min(TOP_K_MAX, T // 4))
    qqi_s = qqi[:, T:].astype(F32)
    attn_s = _sample_attention(
        page_table, qqi_s[:N_IDX_HEADS], wi_f[T:], qqi_s[N_IDX_HEADS:], ki_f[T:], k_f[T:], v_f[T:], tsamp,
        cache_idx_k[0], cache_k[0], cache_v[0],
        min(TOP_K_MAX, (past + DT) // 4))
    attn = jnp.concatenate([attn_p, attn_s.astype(BF16)], axis=0)

    o_p, ssm_p = _gdn(pf, pf, pf, conv_w[0], alog_vec, dtb_vec, gdn_norm[0],
                      jnp.zeros((1, HALO, ch), F32), jnp.zeros((1, GDN_HEADS, GDN_DK, GDN_DV), F32),
                      n_seq=1, n_chunks=T // GDN_CHUNK, C=GDN_CHUNK, valid_rows=GDN_CHUNK, out_dtype=BF16,
                      col_qkv=lay.c_qkv, col_slab=lay.c_slab, col_gz=lay.c_gz)
    halo_s = jnp.pad(state_conv[0], ((0, 0), (HALO - (CONV_WIDTH - 1), 0), (0, 0)))
    o_s, ssm_s = _gdn(_pad_tokens(qkv_s, DB, DT), _pad_tokens(pf[T:, lay.c_slab:lay.c_slab + LANES], DB, DT),
                      _pad_tokens(pf[T:, lay.c_gz:lay.c_gz + vw], DB, DT),
                      conv_w[0], alog_vec, dtb_vec, gdn_norm[0], halo_s, state_ssm[0],
                      n_seq=DB, n_chunks=1, C=SUBLANES, valid_rows=DT, out_dtype=F32)
    o_s = o_s.reshape(DB, SUBLANES, vw)[:, :DT].reshape(RS, vw)
    o_all = jnp.concatenate([o_p, o_s.astype(BF16)], axis=0)

    merged = _merge(attn, o_all, pf, w_attn_up[0], w_gdn_up[0], tm, lay.c_gate_a, lay.c_gate_b)
    x1 = _resid_mm(x, merged, w_out[0], tm, 512)
    act = _ffn_up(x1, norm_ffn[0], w_gate_up[0], tm, FF_TILE)
    x2 = _resid_mm(x1, act, w_dn, tm, FF_TILE)
    tm_y = math.gcd(math.gcd(T, RS), 512)
    y_p = _ple_final(x2, norm_ple[0], w_pg, p_all, w_pe, norm_final, 0, T, tm_y)
    y_s = _ple_final(x2, norm_ple[0], w_pg, p_all, w_pe, norm_final, T, RS, tm_y)

    def heads(a, n):
        return a.reshape(a.shape[0], n, a.shape[1] // n)

    nc = CONV_WIDTH - 1
    return (
        y_p.reshape(1, T, D),
        y_s.reshape(DB, DT, D),
        heads(k_f[:T], N_KV_HEADS)[None, None],
        heads(v_f[:T], N_KV_HEADS)[None, None],
        ki_f[:T][None, None],
        pf[T - nc:T, lay.c_qkv:lay.c_qkv + ch][None, None],
        ssm_p.astype(state_ssm.dtype)[None],
        heads(k_f[T:], N_KV_HEADS).reshape(1, DB, DT, N_KV_HEADS, HEAD_DIM),
        heads(v_f[T:], N_KV_HEADS).reshape(1, DB, DT, N_KV_HEADS, HEAD_DIM),
        ki_f[T:].reshape(1, DB, DT, IDX_DIM),
        qkv_s.reshape(DB, DT, ch)[:, DT - nc:][None],
        ssm_s.astype(state_ssm.dtype)[None],
    )
```

```python
import functools
import math

import numpy as np
import jax
import jax.numpy as jnp
from jax import lax
from jax.experimental import pallas as pl
from jax.experimental.pallas import tpu as pltpu

F32 = jnp.float32
BF16 = jnp.bfloat16

N_HEADS = 16
N_KV_HEADS = 4
HEAD_DIM = 128
GROUPS = N_HEADS // N_KV_HEADS
N_IDX_HEADS = 32
IDX_DIM = 128
TOP_K_MAX = 256
PAGE_SIZE = 128
N_BUCKETS = 32
MAX_DISTANCE = 128
GDN_HEADS = 16
GDN_DK = 128
GDN_DV = 128
CONV_WIDTH = 4
GDN_CHUNK = 64
EPS = 1e-6
NEG = -1e30
LOG2E = math.log2(math.e)

LANES = 128
SUBLANES = 8
VMEM_LIMIT = 56 * 1024 * 1024


def _cparams(*sem):
    return pltpu.CompilerParams(dimension_semantics=sem, vmem_limit_bytes=VMEM_LIMIT)


def _rms_rows(x_ref, nw_ref, h_ref, rows):
    tm = x_ref.shape[0]
    rows = math.gcd(rows, tm)
    nw = nw_ref[...]

    def body(r, _):
        sl = pl.ds(pl.multiple_of(r * rows, rows), rows)
        x = x_ref[sl, :]
        ms = jnp.mean(x * x, axis=-1, keepdims=True)
        h_ref[sl, :] = (x * lax.rsqrt(ms + EPS) * nw).astype(h_ref.dtype)
        return 0

    lax.fori_loop(0, tm // rows, body, 0)


PROJ_TN = 512


def _rmsnorm_body(x_ref, nw_ref, h_ref):
    _rms_rows(x_ref, nw_ref, h_ref, 64)


def _rmsnorm_bf16(x, norm_w, tm):
    M, D = x.shape
    return pl.pallas_call(
        _rmsnorm_body,
        grid=(M // tm,),
        in_specs=[pl.BlockSpec((tm, D), lambda i: (i, 0)), pl.BlockSpec((1, D), lambda i: (0, 0))],
        out_specs=pl.BlockSpec((tm, D), lambda i: (i, 0)),
        out_shape=jax.ShapeDtypeStruct((M, D), BF16),
        compiler_params=_cparams("parallel"),
        name="rmsnorm",
    )(x, norm_w.reshape(1, D))


def _proj_body(h_ref, wt_ref, qqi_ref, kvb_ref, pf_ref, *, steps):
    j = pl.program_id(1)
    hpt = PROJ_TN // HEAD_DIM

    def within(name):
        lo, n = steps[name]
        return (j >= lo) & (j < lo + n)

    acc = lax.dot_general(h_ref[...], wt_ref[...].astype(BF16), (((1,), (1,)), ((), ())), preferred_element_type=F32)
    is_heads = within("q") | within("qi")

    @pl.when(is_heads)
    def _():
        for hh in range(hpt):
            qqi_ref[hh] = acc[:, hh * HEAD_DIM:(hh + 1) * HEAD_DIM].astype(qqi_ref.dtype)

    @pl.when(jnp.logical_not(is_heads))
    def _():
        pf_ref[...] = acc

    @pl.when(within("kv"))
    def _():
        kvb_ref[...] = acc.astype(kvb_ref.dtype)


class ProjLayout:
    def __init__(self, d_model):
        tn = PROJ_TN
        aw, kvw, iw = N_HEADS * HEAD_DIM, N_KV_HEADS * HEAD_DIM, N_IDX_HEADS * IDX_DIM
        ch = 2 * GDN_HEADS * GDN_DK + GDN_HEADS * GDN_DV
        gw = GDN_HEADS * GDN_DV + 2 * d_model
        o_ki = aw + 2 * kvw + iw
        o_qkv = o_ki + IDX_DIM + N_IDX_HEADS
        o_ga = o_qkv + ch
        o_gz = o_ga + 2 * GDN_HEADS
        assert GB_LANE == GA_LANE + GDN_HEADS
        groups = (("q", 0, aw), ("k", aw, kvw), ("v", aw + kvw, kvw), ("qi", aw + 2 * kvw, iw),
                  ("tile_a", o_ki, tn), ("tile_b", o_ga - LANES - GA_LANE, tn), ("qkv", o_qkv, ch), ("gates", o_gz, gw))
        self.steps, self.rows, lo = {}, {}, 0
        for name, start, width in groups:
            assert start % 16 == 0 and width % tn == 0
            self.steps[name] = (lo, width // tn)
            self.rows[name] = start
            lo += width // tn
        self.n_steps = lo
        self.ch, self.gw, self.kvw = ch, gw, kvw
        self.c_qkv, self.c_gz = 0, ch
        self.c_gate_a, self.c_gate_b = ch + GDN_HEADS * GDN_DV, ch + GDN_HEADS * GDN_DV + d_model
        self.c_k = ch + gw
        self.c_v = self.c_k + kvw
        self.c_ki = self.c_v + kvw
        self.c_wi = self.c_ki + IDX_DIM
        self.c_slab = self.c_ki + tn + LANES
        self.width = self.c_ki + 2 * tn


def _proj(h, w_t, tm):
    M, D = h.shape
    tn = PROJ_TN
    lay = ProjLayout(D)
    st = lay.steps
    assert st["v"][0] == st["k"][0] + st["k"][1] and st["tile_b"][0] == st["tile_a"][0] + 1
    assert st["gates"][0] == st["qkv"][0] + st["qkv"][1] and lay.c_gz == lay.ch and lay.kvw == tn

    def w_map(i, j):
        off = jnp.int32(0)
        for name, (lo, _) in st.items():
            off = jnp.where(j >= lo, lay.rows[name] + (j - lo) * tn, off)
        return (pl.multiple_of(off, 16), 0)

    hpt = tn // HEAD_DIM
    n_qi_blk, n_q_blk = N_IDX_HEADS // hpt, N_HEADS // hpt

    def heads_map(i, j):
        q_blk = n_qi_blk + jnp.clip(j - st["q"][0], 0, n_q_blk - 1)
        qi_blk = jnp.clip(j - st["qi"][0], 0, n_qi_blk - 1)
        return (jnp.where(j < st["qi"][0], q_blk, qi_blk), i, 0)

    def pf_map(i, j):
        blk = lay.c_k // tn + jnp.clip(j - st["k"][0], 0, 1)
        blk = jnp.where(j >= st["tile_a"][0], lay.c_ki // tn + j - st["tile_a"][0], blk)
        blk = jnp.where(j >= st["qkv"][0], j - st["qkv"][0], blk)
        return (i, blk)

    steps = dict(q=st["q"], qi=st["qi"], kv=(st["k"][0], 2))
    return pl.pallas_call(
        functools.partial(_proj_body, steps=steps),
        grid=(M // tm, lay.n_steps),
        in_specs=[
            pl.BlockSpec((tm, D), lambda i, j: (i, 0), pipeline_mode=pl.Buffered(1)),
            pl.BlockSpec((pl.Element(tn), pl.Element(D)), w_map),
        ],
        out_specs=(
            pl.BlockSpec((hpt, tm, HEAD_DIM), heads_map),
            pl.BlockSpec((tm, tn), lambda i, j: (i, jnp.clip(j - st["k"][0], 0, 1))),
            pl.BlockSpec((tm, tn), pf_map),
        ),
        out_shape=(
            jax.ShapeDtypeStruct((N_IDX_HEADS + N_HEADS, M, HEAD_DIM), BF16),
            jax.ShapeDtypeStruct((M, 2 * lay.kvw), BF16),
            jax.ShapeDtypeStruct((M, lay.width), F32),
        ),
        compiler_params=_cparams("parallel", "arbitrary"),
        name="in_proj",
    )(h, w_t)


def _merge_body(attn_ref, o_ref, ga_ref, gb_ref, wa_ref, wg_ref, out_ref):
    a = jnp.dot(attn_ref[...], wa_ref[...].astype(BF16), preferred_element_type=F32)
    b = jnp.dot(o_ref[...], wg_ref[...].astype(BF16), preferred_element_type=F32)
    out_ref[...] = (jax.nn.sigmoid(ga_ref[...]) * a + jax.nn.sigmoid(gb_ref[...]) * b).astype(out_ref.dtype)


def _merge(attn, o, gates, wa, wg, tm, col_a, col_b, tn):
    M, KA = attn.shape
    D = wa.shape[1]
    assert col_a % tn == 0 and col_b % tn == 0
    a_off, b_off = col_a // tn, col_b // tn
    return pl.pallas_call(
        _merge_body,
        grid=(M // tm, D // tn),
        in_specs=[
            pl.BlockSpec((tm, KA), lambda i, j: (i, 0), pipeline_mode=pl.Buffered(1)),
            pl.BlockSpec((tm, KA), lambda i, j: (i, 0), pipeline_mode=pl.Buffered(1)),
            pl.BlockSpec((tm, tn), lambda i, j: (i, a_off + j)),
            pl.BlockSpec((tm, tn), lambda i, j: (i, b_off + j)),
            pl.BlockSpec((KA, tn), lambda i, j: (0, j)),
            pl.BlockSpec((KA, tn), lambda i, j: (0, j)),
        ],
        out_specs=pl.BlockSpec((tm, tn), lambda i, j: (i, j)),
        out_shape=jax.ShapeDtypeStruct((M, D), BF16),
        compiler_params=_cparams("parallel", "arbitrary"),
        name="merge",
    )(attn, o, gates, gates, wa, wg)


def _resid_mm_body(x_ref, a_ref, w_ref, out_ref):
    out_ref[...] = x_ref[...] + jnp.dot(a_ref[...], w_ref[...].astype(BF16), preferred_element_type=F32)


def _resid_mm(x, a, w, tm, tn):
    M, N = x.shape
    Kd = a.shape[1]
    return pl.pallas_call(
        _resid_mm_body,
        grid=(M // tm, N // tn),
        in_specs=[
            pl.BlockSpec((tm, tn), lambda i, j: (i, j)),
            pl.BlockSpec((tm, Kd), lambda i, j: (i, 0), pipeline_mode=pl.Buffered(1)),
            pl.BlockSpec((Kd, tn), lambda i, j: (0, j)),
        ],
        out_specs=pl.BlockSpec((tm, tn), lambda i, j: (i, j)),
        out_shape=jax.ShapeDtypeStruct((M, N), F32),
        compiler_params=_cparams("parallel", "arbitrary"),
        name="resid_mm",
    )(x, a, w)


def _ffn_up_body(x_ref, nw_ref, wg_ref, wu_ref, out_ref, h_ref):
    @pl.when(pl.program_id(1) == 0)
    def _():
        _rms_rows(x_ref, nw_ref, h_ref, 64)

    h = h_ref[...]
    g = jnp.dot(h, wg_ref[...].astype(BF16), preferred_element_type=F32)
    u = jnp.dot(h, wu_ref[...].astype(BF16), preferred_element_type=F32)
    out_ref[...] = (jax.nn.silu(g) * u).astype(out_ref.dtype)


def _ffn_up(x, norm_w, w_gu, tm, tn):
    M, D = x.shape
    ffp = w_gu.shape[1] // 2
    nj = ffp // tn
    return pl.pallas_call(
        _ffn_up_body,
        grid=(M // tm, nj),
        in_specs=[
            pl.BlockSpec((tm, D), lambda i, j: (i, 0), pipeline_mode=pl.Buffered(1)),
            pl.BlockSpec((1, D), lambda i, j: (0, 0)),
            pl.BlockSpec((D, tn), lambda i, j: (0, j)),
            pl.BlockSpec((D, tn), lambda i, j: (0, nj + j)),
        ],
        out_specs=pl.BlockSpec((tm, tn), lambda i, j: (i, j)),
        out_shape=jax.ShapeDtypeStruct((M, ffp), BF16),
        scratch_shapes=[pltpu.VMEM((tm, D), BF16)],
        compiler_params=_cparams("parallel", "arbitrary"),
        name="ffn_up",
    )(x, norm_w.reshape(1, D), w_gu, w_gu)


def _ple_body(x_ref, nw_ref, wg_ref, p_ref, wp_ref, nf_ref, y_ref, h_ref, *, tn):
    j = pl.program_id(1)

    @pl.when(j == 0)
    def _():
        _rms_rows(x_ref, nw_ref, h_ref, 64)

    g = jnp.dot(h_ref[...], wg_ref[...].astype(BF16), preferred_element_type=F32)
    e = jnp.dot(p_ref[...], wp_ref[...].astype(BF16), preferred_element_type=F32)
    cols = pl.ds(pl.multiple_of(j * tn, tn), tn)
    y_ref[:, cols] = x_ref[:, cols] + jax.nn.sigmoid(g) * e

    @pl.when(j == pl.num_programs(1) - 1)
    def _():
        _rms_rows(y_ref, nf_ref, y_ref, 64)


def _ple_final(x, norm_w, w_gate, p, w_ple, norm_final, row0, n_rows, tm, tn):
    D = x.shape[1]
    P = p.shape[1]
    assert row0 % tm == 0 and n_rows % tm == 0
    r0 = row0 // tm
    return pl.pallas_call(
        functools.partial(_ple_body, tn=tn),
        grid=(n_rows // tm, D // tn),
        in_specs=[
            pl.BlockSpec((tm, D), lambda i, j: (r0 + i, 0), pipeline_mode=pl.Buffered(1)),
            pl.BlockSpec((1, D), lambda i, j: (0, 0)),
            pl.BlockSpec((D, tn), lambda i, j: (0, j)),
            pl.BlockSpec((tm, P), lambda i, j: (r0 + i, 0)),
            pl.BlockSpec((P, tn), lambda i, j: (0, j)),
            pl.BlockSpec((1, D), lambda i, j: (0, 0)),
        ],
        out_specs=pl.BlockSpec((tm, D), lambda i, j: (i, 0)),
        out_shape=jax.ShapeDtypeStruct((n_rows, D), F32),
        scratch_shapes=[pltpu.VMEM((tm, D), BF16)],
        compiler_params=_cparams("parallel", "arbitrary"),
        name="ple_final",
    )(x, norm_w.reshape(1, D), w_gate, p, w_ple, norm_final.reshape(1, D))


def _bucket_thresholds():
    d = np.arange(0, 4 * MAX_DISTANCE)
    max_exact = N_BUCKETS // 2
    large = max_exact + (np.log(np.maximum(d, 1) / max_exact) / math.log(MAX_DISTANCE / max_exact)
                         * (N_BUCKETS - max_exact)).astype(np.int32)
    b = np.where(d < max_exact, d, np.minimum(large, N_BUCKETS - 1))
    return [int(np.argmax(b >= k)) for k in range(N_BUCKETS)]


_BUCKET_THR = _bucket_thresholds()


def _bias_of_dist(rb_ref, h, d):
    v = jnp.full(d.shape, rb_ref[0, h], F32)
    for b in range(1, N_BUCKETS):
        v = jnp.where(d >= _BUCKET_THR[b], rb_ref[b, h], v)
    return v


def _bias_tables_body(rb_ref, tprev_ref, tdiag_ref, tsamp_ref, *, qb, past):
    h = pl.program_id(0)
    r = lax.broadcasted_iota(jnp.int32, (qb, qb), 0)
    c = lax.broadcasted_iota(jnp.int32, (qb, qb), 1)
    far = rb_ref[N_BUCKETS - 1, h]
    tprev_ref[0] = (_bias_of_dist(rb_ref, h, r + qb - c) - far) * LOG2E
    tdiag_ref[0] = (_bias_of_dist(rb_ref, h, r - c) - far) * LOG2E
    ls = tsamp_ref.shape[2]
    t = lax.broadcasted_iota(jnp.int32, (SUBLANES, ls), 0) % 4
    lane = lax.broadcasted_iota(jnp.int32, (SUBLANES, ls), 1)
    d = jnp.where(lane < past, past + t - lane, t - (lane - past) % 4)
    tsamp_ref[0] = _bias_of_dist(rb_ref, h, d)


def _bias_tables(rel_bias, qb, past, ls):
    return pl.pallas_call(
        functools.partial(_bias_tables_body, qb=qb, past=past),
        grid=(N_HEADS,),
        in_specs=[pl.BlockSpec(memory_space=pltpu.SMEM)],
        out_specs=(
            pl.BlockSpec((1, qb, qb), lambda h: (h, 0, 0)),
            pl.BlockSpec((1, qb, qb), lambda h: (h, 0, 0)),
            pl.BlockSpec((1, SUBLANES, ls), lambda h: (h, 0, 0)),
        ),
        out_shape=(
            jax.ShapeDtypeStruct((N_HEADS, qb, qb), F32),
            jax.ShapeDtypeStruct((N_HEADS, qb, qb), F32),
            jax.ShapeDtypeStruct((N_HEADS, SUBLANES, ls), F32),
        ),
        compiler_params=_cparams("arbitrary"),
        name="bias_tables",
    )(rel_bias)


INT_MIN = -2 ** 31


def _sort_key(score, valid):
    bits = lax.bitcast_convert_type(score, jnp.int32)
    bits = jnp.where(bits == jnp.int32(INT_MIN), 0, bits)
    key = jnp.where(bits < 0, bits ^ jnp.int32(0x7FFFFFFF), bits)
    return jnp.where(valid, key, jnp.int32(INT_MIN))


def _ties_needed(count_ge, thr, k):
    above = jnp.where(thr[:, :1] < jnp.int32(2 ** 31 - 1), count_ge(jnp.minimum(thr, jnp.int32(2 ** 31 - 2)) + 1), 0.0)
    return k - above


def _prefix_matrix(n):
    j = lax.broadcasted_iota(jnp.int32, (n, 2 * n), 0)
    l = lax.broadcasted_iota(jnp.int32, (n, 2 * n), 1)
    return jnp.where((j <= l) | (l >= n), 1.0, 0.0).astype(BF16)


def _kth_largest(count_ge, rows, k, bits_per_pass=1):
    assert 32 % bits_per_pass == 0
    n_try = 2 ** bits_per_pass - 1

    def step(b, t):
        unit = lax.shift_left(jnp.int32(1), jnp.asarray(32 - bits_per_pass * (b + 1), jnp.int32))
        best = t
        for m in range(1, n_try + 1):
            cand = t + unit * m
            best = jnp.where(count_ge(cand) >= k, cand, best)
        return best

    return lax.fori_loop(0, 32 // bits_per_pass, step, jnp.full((rows, LANES), INT_MIN, jnp.int32))


PQB = 256
PKC = 256
FAR_GROUP = 2

def _pattn_body(qi_ref, wi_ref, ki_ref, q_ref, k_ref, v_ref, tprev_ref, tdiag_ref, out_ref,
                key_ref, wb_ref, m_ref, l_ref, acc_ref, *, idx_scale, n_sel):
    i = pl.program_id(0)
    qb, kc = PQB, PKC
    hg = 8

    wi = wi_ref[...] * idx_scale
    for h in range(N_IDX_HEADS):
        wb_ref[h] = jnp.broadcast_to(wi[:, h:h + 1], (qb, LANES))

    def chunk(c):
        return pl.ds(pl.multiple_of(c * kc, kc), kc)

    row = lax.broadcasted_iota(jnp.int32, (qb, kc), 0)
    col = lax.broadcasted_iota(jnp.int32, (qb, kc), 1)

    def score_chunk(c, _):
        kic = ki_ref[chunk(c), :]
        acc = [jnp.zeros((qb, LANES), F32) for _ in range(kc // LANES)]
        for g in range(N_IDX_HEADS // hg):
            qg = qi_ref[g * hg:(g + 1) * hg].reshape(hg * qb, IDX_DIM)
            s = lax.dot_general(qg, kic, (((1,), (1,)), ((), ())), preferred_element_type=F32)
            for hh in range(hg):
                w = wb_ref[g * hg + hh]
                for half in range(kc // LANES):
                    sh = s[hh * qb:(hh + 1) * qb, half * LANES:(half + 1) * LANES]
                    acc[half] = acc[half] + jnp.maximum(sh, 0.0) * w
        score = jnp.concatenate(acc, axis=1)
        valid = (c * kc + col) <= (i * qb + row)
        key_ref[:, chunk(c)] = _sort_key(score, valid)
        return 0

    lax.fori_loop(0, i + 1, score_chunk, 0)

    def count_ge(cand):
        def body(c, cnt):
            kk = key_ref[:, chunk(c)]
            for half in range(kc // LANES):
                cnt = cnt + jnp.where(kk[:, half * LANES:(half + 1) * LANES] >= cand, 1.0, 0.0)
            return cnt

        cnt = lax.fori_loop(0, i + 1, body, jnp.zeros((qb, LANES), F32))
        return jnp.sum(cnt, axis=1, keepdims=True)

    thr = _kth_largest(count_ge, qb, float(n_sel))
    t2 = jnp.concatenate([thr] * (kc // LANES), axis=1)
    surplus = jnp.max(count_ge(thr)) > float(n_sel)

    def store_mask(c, sel):
        sel = sel & (key_ref[:, chunk(c)] > jnp.int32(INT_MIN))
        key_ref[:, chunk(c)] = lax.bitcast_convert_type(jnp.where(sel, 0.0, NEG).astype(F32), jnp.int32)

    @pl.when(jnp.logical_not(surplus))
    def _():
        def mask_chunk(c, _):
            store_mask(c, key_ref[:, chunk(c)] >= t2)
            return 0

        lax.fori_loop(0, i + 1, mask_chunk, 0)

    @pl.when(surplus)
    def _():
        need = _ties_needed(count_ge, thr, float(n_sel))
        prefix = _prefix_matrix(kc)

        def mask_chunk(c, seen):
            kk = key_ref[:, chunk(c)]
            eq = kk == t2
            cnt = jnp.dot(jnp.where(eq, 1.0, 0.0).astype(BF16), prefix, preferred_element_type=F32)
            rank = cnt[:, :kc] + jnp.concatenate([seen] * (kc // LANES), axis=1)
            store_mask(c, (kk > t2) | (eq & (rank <= need)))
            return seen + cnt[:, kc:kc + LANES]

        lax.fori_loop(0, i + 1, mask_chunk, jnp.zeros((qb, LANES), F32))

    c1 = HEAD_DIM ** -0.5 * LOG2E

    def kv_head(kh, _):
        qs = q_ref[pl.ds(kh * GROUPS, GROUPS)].reshape(GROUPS * qb, HEAD_DIM)
        lanes = pl.ds(pl.multiple_of(kh * HEAD_DIM, HEAD_DIM), HEAD_DIM)
        m_ref[...] = jnp.full(m_ref.shape, NEG, F32)
        l_ref[...] = jnp.zeros(l_ref.shape, F32)
        acc_ref[...] = jnp.zeros(acc_ref.shape, F32)

        def attend(items):
            ss = [lax.dot_general(qs, k_ref[chunk(c), lanes], (((1,), (1,)), ((), ())), preferred_element_type=F32)
                  for c, _ in items]
            m, l, acc = m_ref[...], l_ref[...], acc_ref[...]
            for (c, bias_of_group), s in zip(items, ss):
                vt = v_ref[chunk(c), lanes]
                vx = jnp.concatenate([vt, jnp.ones_like(vt)], axis=1)
                mb = lax.bitcast_convert_type(key_ref[:, chunk(c)], F32)
                ts = []
                for g in range(GROUPS):
                    t = s[g * qb:(g + 1) * qb] * c1 + mb
                    if bias_of_group is not None:
                        t = t + bias_of_group(g)
                    ts.append(t)
                t = jnp.concatenate(ts, axis=0)
                m_new = jnp.maximum(m, jnp.max(t, axis=1, keepdims=True))
                alpha = jnp.exp2(m - m_new)
                p = jnp.exp2(t - jnp.concatenate([m_new] * (kc // LANES), axis=1))
                pv = jnp.dot(p.astype(BF16), vx, preferred_element_type=F32)
                acc = alpha * acc + pv[:, :HEAD_DIM]
                l = alpha * l + pv[:, HEAD_DIM:]
                m = m_new
            m_ref[...], l_ref[...], acc_ref[...] = m, l, acc

        n_far = jnp.maximum(i - 1, 0)

        def far_group(cc, _):
            attend([(FAR_GROUP * cc + u, None) for u in range(FAR_GROUP)])
            return 0

        def far_single(c, _):
            attend([(c, None)])
            return 0

        n_grouped = n_far // FAR_GROUP * FAR_GROUP
        lax.fori_loop(0, n_far // FAR_GROUP, far_group, 0)
        lax.fori_loop(n_grouped, n_far, far_single, 0)

        @pl.when(i >= 1)
        def _():
            attend([(i - 1, lambda g: tprev_ref[kh * GROUPS + g]), (i, lambda g: tdiag_ref[kh * GROUPS + g])])

        @pl.when(i == 0)
        def _():
            attend([(i, lambda g: tdiag_ref[kh * GROUPS + g])])

        o = acc_ref[...] / l_ref[...]
        for g in range(GROUPS):
            out_ref[:, pl.ds(pl.multiple_of((kh * GROUPS + g) * HEAD_DIM, HEAD_DIM), HEAD_DIM)] = (
                o[g * qb:(g + 1) * qb].astype(out_ref.dtype))
        return 0

    lax.fori_loop(0, N_KV_HEADS, kv_head, 0)


def _prompt_attention(qqi, wi, ki_bf, kv_bf, tprev, tdiag, seq, n_sel):
    qb = PQB
    assert N_IDX_HEADS % N_HEADS == 0
    resident = dict(pipeline_mode=pl.Buffered(1))
    return pl.pallas_call(
        functools.partial(_pattn_body, idx_scale=(N_IDX_HEADS * IDX_DIM) ** -0.5, n_sel=n_sel),
        grid=(seq // qb,),
        in_specs=[
            pl.BlockSpec((N_IDX_HEADS, qb, IDX_DIM), lambda i: (0, i, 0)),
            pl.BlockSpec((qb, N_IDX_HEADS), lambda i: (i, 0)),
            pl.BlockSpec((seq, IDX_DIM), lambda i: (0, 0), **resident),
            pl.BlockSpec((N_HEADS, qb, HEAD_DIM), lambda i: (N_IDX_HEADS // N_HEADS, i, 0)),
            pl.BlockSpec((seq, N_KV_HEADS * HEAD_DIM), lambda i: (0, 0), **resident),
            pl.BlockSpec((seq, N_KV_HEADS * HEAD_DIM), lambda i: (0, 1), **resident),
            pl.BlockSpec((N_HEADS, qb, qb), lambda i: (0, 0, 0), **resident),
            pl.BlockSpec((N_HEADS, qb, qb), lambda i: (0, 0, 0), **resident),
        ],
        out_specs=pl.BlockSpec((qb, N_HEADS * HEAD_DIM), lambda i: (i, 0)),
        out_shape=jax.ShapeDtypeStruct((seq, N_HEADS * HEAD_DIM), BF16),
        scratch_shapes=[
            pltpu.VMEM((qb, seq), jnp.int32),
            pltpu.VMEM((N_IDX_HEADS, qb, LANES), F32),
            pltpu.VMEM((GROUPS * qb, LANES), F32),
            pltpu.VMEM((GROUPS * qb, LANES), F32),
            pltpu.VMEM((GROUPS * qb, HEAD_DIM), F32),
        ],
        compiler_params=_cparams("arbitrary"),
        name="prompt_attention",
    )(qqi, wi, ki_bf, qqi, kv_bf, kv_bf, tprev, tdiag)


GA_LANE = 32
GB_LANE = 48
HALO = SUBLANES
def _split2(a):
    hi = a.astype(BF16)
    return hi, (a - hi.astype(F32)).astype(BF16)


def _split3(a):
    hi = a.astype(BF16)
    r = a - hi.astype(F32)
    mid = r.astype(BF16)
    return hi, mid, (r - mid.astype(F32)).astype(BF16)


def _mm3(a2, b2):
    (ah, al), (bh, bl) = a2, b2
    d = lambda x, y: jnp.dot(x, y, preferred_element_type=F32)
    return d(ah, bh) + (d(ah, bl) + d(al, bh))


def _bmm(a, b):
    return jnp.dot(a.astype(BF16), b.astype(BF16), preferred_element_type=F32)


def _bmm_nt(a, b):
    return lax.dot_general(a.astype(BF16), b.astype(BF16), (((1,), (1,)), ((), ())), preferred_element_type=F32)


def _bmm_tn(a, b):
    return lax.dot_general(a.astype(BF16), b.astype(BF16), (((0,), (0,)), ((), ())), preferred_element_type=F32)


def _gdn_body(qkv_ref, slab_ref, gz_ref, cw_ref, alog_ref, dtb_ref, gn_ref, halo0_ref, s0_ref,
              o_ref, s_ref, xp_ref, *, C, valid_rows):
    c = pl.program_id(1)
    kw, vw = GDN_HEADS * GDN_DK, GDN_HEADS * GDN_DV

    @pl.when(c == 0)
    def _():
        xp_ref[0:HALO] = halo0_ref[0]
        s_ref[...] = s0_ref[...]

    xp_ref[HALO:HALO + C] = qkv_ref[...]

    ri = lax.broadcasted_iota(jnp.int32, (C, C), 0)
    ci = lax.broadcasted_iota(jnp.int32, (C, C), 1)
    causal = ri >= ci
    strict = ri > ci
    eye = jnp.where(ri == ci, 1.0, 0.0).astype(F32)
    ltri = jnp.where(causal, 1.0, 0.0).astype(F32)

    slab = slab_ref[...]
    live = lax.broadcasted_iota(jnp.int32, slab.shape, 0) < valid_rows
    g_all = jnp.where(live, -jnp.exp(alog_ref[...]) * jax.nn.softplus(slab + dtb_ref[...]), 0.0)
    beta_all = jnp.where(live, jax.nn.sigmoid(slab), 0.0)
    ltri_b = ltri.astype(BF16)
    gc_all = sum(jnp.dot(ltri_b, part, preferred_element_type=F32) for part in _split3(g_all))
    pad = jnp.zeros((LANES - C, LANES), F32)
    gc_t = jnp.concatenate([gc_all, pad], axis=0).T

    def conv(cols):
        y = cw_ref[0:1, cols] * xp_ref[HALO - 3:HALO - 3 + C, cols]
        for j in range(1, CONV_WIDTH):
            y = y + cw_ref[j:j + 1, cols] * xp_ref[HALO - 3 + j:HALO - 3 + j + C, cols]
        return jax.nn.silu(y)

    def l2n(x):
        return x * lax.rsqrt(jnp.sum(x * x, axis=-1, keepdims=True) + 1e-6)

    n_sq = max(1, (C - 1).bit_length() - 1)

    hs = range(GDN_HEADS)
    q = [l2n(conv(slice(h * GDN_DK, (h + 1) * GDN_DK))) * GDN_DK ** -0.5 for h in hs]
    k = [l2n(conv(slice(kw + h * GDN_DK, kw + (h + 1) * GDN_DK))) for h in hs]
    v = [conv(slice(2 * kw + h * GDN_DV, 2 * kw + (h + 1) * GDN_DV)) for h in hs]
    beta = [beta_all[:, GB_LANE + h:GB_LANE + h + 1] for h in hs]
    gc = [gc_all[:, GA_LANE + h:GA_LANE + h + 1] for h in hs]
    gc_row = [gc_t[GA_LANE + h:GA_LANE + h + 1, 0:C] for h in hs]
    gc_last = [gc_all[C - 1:C, GA_LANE + h:GA_LANE + h + 1] for h in hs]
    decay = [jnp.exp(jnp.where(causal, gc[h] - gc_row[h], NEG)) for h in hs]
    kb = [k[h] * beta[h] for h in hs]
    p = [-jnp.where(strict, _bmm_nt(kb[h], k[h]) * decay[h], 0.0) for h in hs]
    t_inv = [eye + p[h] for h in hs]
    p2 = [_split2(x) for x in p]
    for _ in range(n_sq):
        p = [_mm3(x2, x2) for x2 in p2]
        p2 = [_split2(x) for x in p]
        t_inv = [t + _mm3(_split2(t), x2) for t, x2 in zip(t_inv, p2)]
    egc = [jnp.exp(gc[h]) for h in hs]
    u = [_bmm(t_inv[h], v[h] * beta[h]) for h in hs]
    w = [_bmm(t_inv[h], kb[h] * egc[h]) for h in hs]
    a_in = [jnp.where(causal, _bmm_nt(q[h], k[h]) * decay[h], 0.0) for h in hs]
    qd = [q[h] * egc[h] for h in hs]
    kt = [k[h] * jnp.exp(gc_last[h] - gc[h]) for h in hs]
    s_old = [s_ref[0, h] for h in hs]
    v_new = [u[h] - _bmm(w[h], s_old[h]) for h in hs]
    o = [_bmm(qd[h], s_old[h]) + _bmm(a_in[h], v_new[h]) for h in hs]
    for h in hs:
        s_ref[0, h] = s_old[h] * jnp.exp(gc_last[h]) + _bmm_tn(kt[h], v_new[h])
    for h in hs:
        on = o[h] * lax.rsqrt(jnp.mean(o[h] * o[h], axis=-1, keepdims=True) + EPS) * gn_ref[...]
        gz = gz_ref[:, h * GDN_DV:(h + 1) * GDN_DV]
        o_ref[:, h * GDN_DV:(h + 1) * GDN_DV] = (on * jax.nn.silu(gz)).astype(o_ref.dtype)

    xp_ref[0:HALO] = xp_ref[C:C + HALO]


def _gdn(qkv, misc, gates, conv_w, alog_vec, dtb_vec, gdn_norm, halo0, s0, *, n_seq, n_chunks, C, valid_rows,
         out_dtype, col_qkv=0, col_slab=None, col_gz=0):
    ch = conv_w.shape[1]
    vw = GDN_HEADS * GDN_DV
    rows = n_seq * n_chunks * C
    col_slab = misc.shape[1] - LANES if col_slab is None else col_slab
    assert col_qkv % ch == 0 and col_slab % LANES == 0 and col_gz % vw == 0

    def rmap(s, c):
        return (s * n_chunks + c, 0)

    def at_col(blk):
        return lambda s, c: (s * n_chunks + c, blk)

    return pl.pallas_call(
        functools.partial(_gdn_body, C=C, valid_rows=valid_rows),
        grid=(n_seq, n_chunks),
        in_specs=[
            pl.BlockSpec((C, ch), at_col(col_qkv // ch)),
            pl.BlockSpec((C, LANES), at_col(col_slab // LANES)),
            pl.BlockSpec((C, vw), at_col(col_gz // vw)),
            pl.BlockSpec((CONV_WIDTH, ch), lambda s, c: (0, 0)),
            pl.BlockSpec((1, LANES), lambda s, c: (0, 0)),
            pl.BlockSpec((1, LANES), lambda s, c: (0, 0)),
            pl.BlockSpec((1, GDN_DV), lambda s, c: (0, 0)),
            pl.BlockSpec((1, HALO, ch), lambda s, c: (s, 0, 0)),
            pl.BlockSpec((1, GDN_HEADS, GDN_DK, GDN_DV), lambda s, c: (s, 0, 0, 0)),
        ],
        out_specs=(
            pl.BlockSpec((C, vw), rmap),
            pl.BlockSpec((1, GDN_HEADS, GDN_DK, GDN_DV), lambda s, c: (s, 0, 0, 0)),
        ),
        out_shape=(
            jax.ShapeDtypeStruct((rows, vw), out_dtype),
            jax.ShapeDtypeStruct((n_seq, GDN_HEADS, GDN_DK, GDN_DV), F32),
        ),
        scratch_shapes=[pltpu.VMEM((C + HALO, ch), F32)],
        compiler_params=_cparams("arbitrary", "arbitrary"),
        name=f"gdn_c{C}",
    )(qkv, misc, gates, conv_w, alog_vec, dtb_vec, gdn_norm.reshape(1, GDN_DV), halo0, s0)


SB = 2
ST = SUBLANES // SB
SCORE_KEYS = 512


def _sattn_body(pt_ref, qi_ref, wi_ref, q_ref, kin_ref, kn_ref, vn_ref, tsamp_ref, cidx_hbm, ck_hbm, cv_hbm, out_ref,
                kibuf, kbuf, vbuf, sem, *, past, n_pages, n_sel, idx_scale):
    p = pl.program_id(0)
    n_steps = pl.num_programs(0)
    ls = kibuf.shape[1]
    rows = SB * ST

    def slot_of(step, bl):
        return (step % 2) * SB + bl

    def copies(step, bl):
        b = step * SB + bl
        sl = slot_of(step, bl)
        out = []
        for j in range(n_pages):
            pg = pt_ref[b, j]
            dst = pl.ds(j * PAGE_SIZE, PAGE_SIZE)
            out.append(pltpu.make_async_copy(cidx_hbm.at[pg], kibuf.at[sl, dst], sem.at[sl]))
            for kh in range(N_KV_HEADS):
                out.append(pltpu.make_async_copy(ck_hbm.at[pg, :, kh], kbuf.at[sl, kh, dst], sem.at[sl]))
                out.append(pltpu.make_async_copy(cv_hbm.at[pg, :, kh], vbuf.at[sl, kh, dst], sem.at[sl]))
        return out

    @pl.when(p == 0)
    def _():
        for bl in range(SB):
            for cp in copies(0, bl):
                cp.start()
        for sl in range(2 * SB):
            kibuf[sl, past:ls] = jnp.zeros((ls - past, IDX_DIM), F32)
            kbuf[sl, :, past:ls] = jnp.zeros((N_KV_HEADS, ls - past, HEAD_DIM), F32)
            vbuf[sl, :, past:ls] = jnp.zeros((N_KV_HEADS, ls - past, HEAD_DIM), F32)

    @pl.when(p + 1 < n_steps)
    def _():
        for bl in range(SB):
            for cp in copies(p + 1, bl):
                cp.start()

    slots = [slot_of(p, bl) for bl in range(SB)]
    bt = lax.broadcasted_iota(jnp.int32, (rows, ls), 0)
    lane = lax.broadcasted_iota(jnp.int32, (rows, ls), 1)
    r_new = lane - past
    valid = (lane < past) | ((r_new < rows) & (r_new // ST == bt // ST) & (r_new % ST <= bt % ST))

    for sl in slots:
        kibuf[sl, past:past + rows] = kin_ref[...]
        for kh in range(N_KV_HEADS):
            kbuf[sl, kh, past:past + rows] = kn_ref[:, kh * HEAD_DIM:(kh + 1) * HEAD_DIM]
            vbuf[sl, kh, past:past + rows] = vn_ref[:, kh * HEAD_DIM:(kh + 1) * HEAD_DIM]

    wi = wi_ref[...] * idx_scale
    qi = qi_ref[...].reshape(N_IDX_HEADS * rows, IDX_DIM).astype(BF16)
    key_chunks = [(s, min(SCORE_KEYS, ls - s)) for s in range(0, ls, SCORE_KEYS)]

    def scores(bl):
        parts = []
        for s0, n in key_chunks:
            kic = kibuf[slots[bl], s0:s0 + n].astype(BF16)
            s = lax.dot_general(qi, kic, (((1,), (1,)), ((), ())), preferred_element_type=F32)
            acc = jnp.zeros((rows, n), F32)
            for h in range(N_IDX_HEADS):
                acc = acc + jnp.maximum(s[h * rows:(h + 1) * rows], 0.0) * wi[:, h:h + 1]
            parts.append(acc)
        return jnp.concatenate(parts, axis=1)

    sc = []
    for bl in range(SB):
        for cp in copies(p, bl):
            cp.wait()
        sc.append(scores(bl))
    score = sc[0]
    for bl in range(1, SB):
        score = jnp.where(bt // ST == bl, sc[bl], score)
    key = _sort_key(score, valid)

    def count_ge(cand):
        cnt = jnp.zeros((rows, LANES), F32)
        for t in range(ls // LANES):
            cnt = cnt + jnp.where(key[:, t * LANES:(t + 1) * LANES] >= cand, 1.0, 0.0)
        return jnp.sum(cnt, axis=1, keepdims=True)

    thr = _kth_largest(count_ge, rows, float(n_sel), bits_per_pass=2)
    need = _ties_needed(count_ge, thr, float(n_sel))
    nt = ls // LANES
    thr_full = jnp.concatenate([thr] * nt, axis=1)
    eq = key == thr_full
    tiles = [jnp.where(eq[:, t * LANES:(t + 1) * LANES], 1.0, 0.0) for t in range(nt)]
    tiles.append(jnp.zeros((rows, LANES), F32))
    cnt = jnp.dot(jnp.concatenate(tiles, axis=0).astype(BF16), _prefix_matrix(LANES), preferred_element_type=F32)
    seen, ranks = jnp.zeros((rows, LANES), F32), []
    for t in range(nt):
        blk = cnt[t * rows:(t + 1) * rows]
        ranks.append(blk[:, :LANES] + seen)
        seen = seen + blk[:, LANES:]
    take = eq & (jnp.concatenate(ranks, axis=1) <= need)
    mb = jnp.where(((key > thr_full) | take) & valid, 0.0, NEG).astype(F32)

    scale = HEAD_DIM ** -0.5
    row_b = lax.broadcasted_iota(jnp.int32, (GROUPS * rows, HEAD_DIM), 0) % rows // ST
    outs = [None] * N_KV_HEADS
    for bl in range(SB):
        for kh in range(N_KV_HEADS):
            qs = q_ref[kh * GROUPS:(kh + 1) * GROUPS].reshape(GROUPS * rows, HEAD_DIM).astype(BF16)
            kt = kbuf[slots[bl], kh].astype(BF16)
            vt = vbuf[slots[bl], kh].astype(BF16)
            s = lax.dot_general(qs, kt, (((1,), (1,)), ((), ())), preferred_element_type=F32) * scale
            s = (s.reshape(GROUPS, rows, ls) + tsamp_ref[kh * GROUPS:(kh + 1) * GROUPS] + mb[None]).reshape(GROUPS * rows, ls)
            m = jnp.max(s, axis=1, keepdims=True)
            e = jnp.exp(s - m)
            o = jnp.dot(e.astype(BF16), vt, preferred_element_type=F32) / jnp.sum(e, axis=1, keepdims=True)
            outs[kh] = o if bl == 0 else jnp.where(row_b == bl, o, outs[kh])

    for kh in range(N_KV_HEADS):
        for g in range(GROUPS):
            hd = kh * GROUPS + g
            out_ref[:, hd * HEAD_DIM:(hd + 1) * HEAD_DIM] = outs[kh][g * rows:(g + 1) * rows]


def _sample_attention(page_table, qi_s, wi_s, q_s, ki_new, k_new, v_new, tsamp, cache_idx, cache_k, cache_v, n_sel):
    R = wi_s.shape[0]
    n_b, n_pages = page_table.shape
    past = n_pages * PAGE_SIZE
    ls = past + PAGE_SIZE
    rows = SB * ST
    assert R == n_b * ST and n_b % SB == 0
    kvw = N_KV_HEADS * HEAD_DIM
    grid_spec = pltpu.PrefetchScalarGridSpec(
        num_scalar_prefetch=1,
        grid=(n_b // SB,),
        in_specs=[
            pl.BlockSpec((N_IDX_HEADS, rows, IDX_DIM), lambda p, pt: (0, p, 0)),
            pl.BlockSpec((rows, N_IDX_HEADS), lambda p, pt: (p, 0)),
            pl.BlockSpec((N_HEADS, rows, HEAD_DIM), lambda p, pt: (0, p, 0)),
            pl.BlockSpec((rows, IDX_DIM), lambda p, pt: (p, 0)),
            pl.BlockSpec((rows, kvw), lambda p, pt: (p, 0)),
            pl.BlockSpec((rows, kvw), lambda p, pt: (p, 0)),
            pl.BlockSpec((N_HEADS, SUBLANES, ls), lambda p, pt: (0, 0, 0)),
            pl.BlockSpec(memory_space=pl.ANY),
            pl.BlockSpec(memory_space=pl.ANY),
            pl.BlockSpec(memory_space=pl.ANY),
        ],
        out_specs=pl.BlockSpec((rows, N_HEADS * HEAD_DIM), lambda p, pt: (p, 0)),
        scratch_shapes=[
            pltpu.VMEM((2 * SB, ls, IDX_DIM), F32),
            pltpu.VMEM((2 * SB, N_KV_HEADS, ls, HEAD_DIM), F32),
            pltpu.VMEM((2 * SB, N_KV_HEADS, ls, HEAD_DIM), F32),
            pltpu.SemaphoreType.DMA((2 * SB,)),
        ],
    )
    return pl.pallas_call(
        functools.partial(_sattn_body, past=past, n_pages=n_pages, n_sel=n_sel,
                          idx_scale=(N_IDX_HEADS * IDX_DIM) ** -0.5),
        grid_spec=grid_spec,
        out_shape=jax.ShapeDtypeStruct((R, N_HEADS * HEAD_DIM), F32),
        compiler_params=_cparams("arbitrary"),
        name="sample_attention",
    )(page_table, qi_s, wi_s, q_s, ki_new, k_new, v_new, tsamp, cache_idx, cache_k, cache_v)


ROW_TILE_CAP = 1088
SMALL_ROW_TILE = 512
OUT_TILE = 512
FF_TILE = 256


def _row_tile(m, cap=ROW_TILE_CAP):
    return max(t for t in range(16, cap + 1, 16) if m % t == 0)


def _pad_tokens(a, db, dt):
    a = a.reshape(db, dt, a.shape[-1])
    return jnp.pad(a, ((0, 0), (0, SUBLANES - dt), (0, 0))).reshape(db * SUBLANES, a.shape[-1])


def kernel(x_prompt, x_sample, cache_k, cache_v, cache_idx_k, state_conv, state_ssm, page_table, p_prompt, p_sample,
           rel_bias, norm_mix, w_in, conv_w, a_log, dt_bias, gdn_norm, w_attn_up, w_gdn_up, w_out, norm_ffn, w_gate_up,
           w_down, norm_ple, w_ple_gate, w_ple, norm_final):
    assert x_prompt.shape[0] == 1 and w_in.shape[0] == 1, "one prompt sequence, one layer"
    _, T, D = x_prompt.shape
    DB, DT, _ = x_sample.shape
    assert DT == ST and T % PQB == 0 and T % GDN_CHUNK == 0 and DT >= CONV_WIDTH - 1
    past = page_table.shape[1] * PAGE_SIZE
    RS = DB * DT
    M = T + RS
    tm = _row_tile(M)
    kw, vw = GDN_HEADS * GDN_DK, GDN_HEADS * GDN_DV
    ch = 2 * kw + vw
    kvw = N_KV_HEADS * HEAD_DIM

    w_t = jnp.swapaxes(w_in[0], 0, 1)
    w_dn = w_down[0].astype(BF16)
    w_pg = w_ple_gate[0].astype(BF16)
    w_pe = w_ple[0].astype(BF16)
    lay = ProjLayout(D)
    alog_vec = jnp.zeros((1, LANES), F32).at[0, GA_LANE:GA_LANE + GDN_HEADS].set(a_log[0])
    dtb_vec = jnp.zeros((1, LANES), F32).at[0, GA_LANE:GA_LANE + GDN_HEADS].set(dt_bias[0])

    x = jnp.concatenate([x_prompt[0], x_sample.reshape(RS, D)], axis=0)
    p_all = jnp.concatenate([p_prompt[0, 0], p_sample[0].reshape(RS, -1)], axis=0).astype(BF16)

    h0 = _rmsnorm_bf16(x, norm_mix[0], _row_tile(M, SMALL_ROW_TILE))
    qqi, kv_bf, pf = _proj(h0, w_t, tm)
    k_f, v_f = pf[:, lay.c_k:lay.c_k + kvw], pf[:, lay.c_v:lay.c_v + kvw]
    ki_f = pf[:, lay.c_ki:lay.c_ki + IDX_DIM]
    wi_f = pf[:, lay.c_wi:lay.c_wi + N_IDX_HEADS]
    qkv_s = pf[T:, lay.c_qkv:lay.c_qkv + ch]

    tprev, tdiag, tsamp = _bias_tables(rel_bias, PQB, past, past + PAGE_SIZE)
    attn_p = _prompt_attention(qqi, wi_f, ki_f.astype(BF16), kv_bf, tprev, tdiag, T, min(TOP_K_MAX, T // 4))
    qqi_s = qqi[:, T:].astype(F32)
    attn_s = _sample_attention(
        page_table, qqi_s[:N_IDX_HEADS], wi_f[T:], qqi_s[N_IDX_HEADS:], ki_f[T:], k_f[T:], v_f[T:], tsamp,
        cache_idx_k[0], cache_k[0], cache_v[0],
        min(TOP_K_MAX, (past + DT) // 4))
    attn = jnp.concatenate([attn_p, attn_s.astype(BF16)], axis=0)

    o_p, ssm_p = _gdn(pf, pf, pf, conv_w[0], alog_vec, dtb_vec, gdn_norm[0],
                      jnp.zeros((1, HALO, ch), F32), jnp.zeros((1, GDN_HEADS, GDN_DK, GDN_DV), F32),
                      n_seq=1, n_chunks=T // GDN_CHUNK, C=GDN_CHUNK, valid_rows=GDN_CHUNK, out_dtype=BF16,
                      col_qkv=lay.c_qkv, col_slab=lay.c_slab, col_gz=lay.c_gz)
    halo_s = jnp.pad(state_conv[0], ((0, 0), (HALO - (CONV_WIDTH - 1), 0), (0, 0)))
    o_s, ssm_s = _gdn(_pad_tokens(qkv_s, DB, DT), _pad_tokens(pf[T:, lay.c_slab:lay.c_slab + LANES], DB, DT),
                      _pad_tokens(pf[T:, lay.c_gz:lay.c_gz + vw], DB, DT),
                      conv_w[0], alog_vec, dtb_vec, gdn_norm[0], halo_s, state_ssm[0],
                      n_seq=DB, n_chunks=1, C=SUBLANES, valid_rows=DT, out_dtype=F32)
    o_s = o_s.reshape(DB, SUBLANES, vw)[:, :DT].reshape(RS, vw)
    o_all = jnp.concatenate([o_p, o_s.astype(BF16)], axis=0)

    merged = _merge(attn, o_all, pf, w_attn_up[0], w_gdn_up[0], tm, lay.c_gate_a, lay.c_gate_b, OUT_TILE)
    x1 = _resid_mm(x, merged, w_out[0], tm, OUT_TILE)
    act = _ffn_up(x1, norm_ffn[0], w_gate_up[0], tm, FF_TILE)
    x2 = _resid_mm(x1, act, w_dn, tm, FF_TILE)
    tm_y = math.gcd(math.gcd(T, RS), SMALL_ROW_TILE)
    y_p = _ple_final(x2, norm_ple[0], w_pg, p_all, w_pe, norm_final, 0, T, tm_y, OUT_TILE)
    y_s = _ple_final(x2, norm_ple[0], w_pg, p_all, w_pe, norm_final, T, RS, tm_y, OUT_TILE)

    def heads(a, n):
        return a.reshape(a.shape[0], n, a.shape[1] // n)

    nc = CONV_WIDTH - 1
    return (
        y_p.reshape(1, T, D),
        y_s.reshape(DB, DT, D),
        heads(k_f[:T], N_KV_HEADS)[None, None],
        heads(v_f[:T], N_KV_HEADS)[None, None],
        ki_f[:T][None, None],
        pf[T - nc:T, lay.c_qkv:lay.c_qkv + ch][None, None],
        ssm_p.astype(state_ssm.dtype)[None],
        heads(k_f[T:], N_KV_HEADS).reshape(1, DB, DT, N_KV_HEADS, HEAD_DIM),
        heads(v_f[T:], N_KV_HEADS).reshape(1, DB, DT, N_KV_HEADS, HEAD_DIM),
        ki_f[T:].reshape(1, DB, DT, IDX_DIM),
        qkv_s.reshape(DB, DT, ch)[:, DT - nc:][None],
        ssm_s.astype(state_ssm.dtype)[None],
    )
```

```python
import functools
import math

import numpy as np
import jax
import jax.numpy as jnp
from jax import lax
from jax.experimental import pallas as pl
from jax.experimental.pallas import tpu as pltpu

F32 = jnp.float32
BF16 = jnp.bfloat16

N_HEADS = 16
N_KV_HEADS = 4
HEAD_DIM = 128
GROUPS = N_HEADS // N_KV_HEADS
N_IDX_HEADS = 32
IDX_DIM = 128
TOP_K_MAX = 256
PAGE_SIZE = 128
N_BUCKETS = 32
MAX_DISTANCE = 128
GDN_HEADS = 16
GDN_DK = 128
GDN_DV = 128
CONV_WIDTH = 4
GDN_CHUNK = 64
EPS = 1e-6
NEG = -1e30
LOG2E = math.log2(math.e)

LANES = 128
SUBLANES = 8
VMEM_LIMIT = 56 * 1024 * 1024


def _cparams(*sem):
    return pltpu.CompilerParams(dimension_semantics=sem, vmem_limit_bytes=VMEM_LIMIT)


def _rms_rows(x_ref, nw_ref, h_ref, rows):
    tm = x_ref.shape[0]
    rows = math.gcd(rows, tm)
    nw = nw_ref[...]

    def body(r, _):
        sl = pl.ds(pl.multiple_of(r * rows, rows), rows)
        x = x_ref[sl, :]
        ms = jnp.mean(x * x, axis=-1, keepdims=True)
        h_ref[sl, :] = (x * lax.rsqrt(ms + EPS) * nw).astype(h_ref.dtype)
        return 0

    lax.fori_loop(0, tm // rows, body, 0)


PROJ_TN = 512


def _rmsnorm_body(x_ref, nw_ref, h_ref):
    _rms_rows(x_ref, nw_ref, h_ref, 64)


def _rmsnorm_bf16(x, norm_w, tm):
    M, D = x.shape
    return pl.pallas_call(
        _rmsnorm_body,
        grid=(M // tm,),
        in_specs=[pl.BlockSpec((tm, D), lambda i: (i, 0)), pl.BlockSpec((1, D), lambda i: (0, 0))],
        out_specs=pl.BlockSpec((tm, D), lambda i: (i, 0)),
        out_shape=jax.ShapeDtypeStruct((M, D), BF16),
        compiler_params=_cparams("parallel"),
        name="rmsnorm",
    )(x, norm_w.reshape(1, D))


def _proj_body(h_ref, wt_ref, qqi_ref, kvb_ref, pf_ref, *, steps):
    j = pl.program_id(1)
    hpt = PROJ_TN // HEAD_DIM

    def within(name):
        lo, n = steps[name]
        return (j >= lo) & (j < lo + n)

    acc = lax.dot_general(h_ref[...], wt_ref[...].astype(BF16), (((1,), (1,)), ((), ())), preferred_element_type=F32)
    is_heads = within("q") | within("qi")

    @pl.when(is_heads)
    def _():
        for hh in range(hpt):
            qqi_ref[hh] = acc[:, hh * HEAD_DIM:(hh + 1) * HEAD_DIM].astype(qqi_ref.dtype)

    @pl.when(jnp.logical_not(is_heads))
    def _():
        pf_ref[...] = acc

    @pl.when(within("kv"))
    def _():
        kvb_ref[...] = acc.astype(kvb_ref.dtype)


class ProjLayout:
    def __init__(self, d_model):
        tn = PROJ_TN
        aw, kvw, iw = N_HEADS * HEAD_DIM, N_KV_HEADS * HEAD_DIM, N_IDX_HEADS * IDX_DIM
        ch = 2 * GDN_HEADS * GDN_DK + GDN_HEADS * GDN_DV
        gw = GDN_HEADS * GDN_DV + 2 * d_model
        o_ki = aw + 2 * kvw + iw
        o_qkv = o_ki + IDX_DIM + N_IDX_HEADS
        o_ga = o_qkv + ch
        o_gz = o_ga + 2 * GDN_HEADS
        assert GB_LANE == GA_LANE + GDN_HEADS
        groups = (("q", 0, aw), ("k", aw, kvw), ("v", aw + kvw, kvw), ("qi", aw + 2 * kvw, iw),
                  ("tile_a", o_ki, tn), ("tile_b", o_ga - LANES - GA_LANE, tn), ("qkv", o_qkv, ch), ("gates", o_gz, gw))
        self.steps, self.rows, lo = {}, {}, 0
        for name, start, width in groups:
            assert start % 16 == 0 and width % tn == 0
            self.steps[name] = (lo, width // tn)
            self.rows[name] = start
            lo += width // tn
        self.n_steps = lo
        self.ch, self.gw, self.kvw = ch, gw, kvw
        self.c_qkv, self.c_gz = 0, ch
        self.c_gate_a, self.c_gate_b = ch + GDN_HEADS * GDN_DV, ch + GDN_HEADS * GDN_DV + d_model
        self.c_k = ch + gw
        self.c_v = self.c_k + kvw
        self.c_ki = self.c_v + kvw
        self.c_wi = self.c_ki + IDX_DIM
        self.c_slab = self.c_ki + tn + LANES
        self.width = self.c_ki + 2 * tn


def _proj(h, w_t, tm):
    M, D = h.shape
    tn = PROJ_TN
    lay = ProjLayout(D)
    st = lay.steps
    assert st["v"][0] == st["k"][0] + st["k"][1] and st["tile_b"][0] == st["tile_a"][0] + 1
    assert st["gates"][0] == st["qkv"][0] + st["qkv"][1] and lay.c_gz == lay.ch and lay.kvw == tn

    def w_map(i, j):
        off = jnp.int32(0)
        for name, (lo, _) in st.items():
            off = jnp.where(j >= lo, lay.rows[name] + (j - lo) * tn, off)
        return (pl.multiple_of(off, 16), 0)

    hpt = tn // HEAD_DIM
    n_qi_blk, n_q_blk = N_IDX_HEADS // hpt, N_HEADS // hpt

    def heads_map(i, j):
        q_blk = n_qi_blk + jnp.clip(j - st["q"][0], 0, n_q_blk - 1)
        qi_blk = jnp.clip(j - st["qi"][0], 0, n_qi_blk - 1)
        return (jnp.where(j < st["qi"][0], q_blk, qi_blk), i, 0)

    def pf_map(i, j):
        blk = lay.c_k // tn + jnp.clip(j - st["k"][0], 0, 1)
        blk = jnp.where(j >= st["tile_a"][0], lay.c_ki // tn + j - st["tile_a"][0], blk)
        blk = jnp.where(j >= st["qkv"][0], j - st["qkv"][0], blk)
        return (i, blk)

    steps = dict(q=st["q"], qi=st["qi"], kv=(st["k"][0], 2))
    return pl.pallas_call(
        functools.partial(_proj_body, steps=steps),
        grid=(M // tm, lay.n_steps),
        in_specs=[
            pl.BlockSpec((tm, D), lambda i, j: (i, 0), pipeline_mode=pl.Buffered(1)),
            pl.BlockSpec((pl.Element(tn), pl.Element(D)), w_map),
        ],
        out_specs=(
            pl.BlockSpec((hpt, tm, HEAD_DIM), heads_map),
            pl.BlockSpec((tm, tn), lambda i, j: (i, jnp.clip(j - st["k"][0], 0, 1))),
            pl.BlockSpec((tm, tn), pf_map),
        ),
        out_shape=(
            jax.ShapeDtypeStruct((N_IDX_HEADS + N_HEADS, M, HEAD_DIM), BF16),
            jax.ShapeDtypeStruct((M, 2 * lay.kvw), BF16),
            jax.ShapeDtypeStruct((M, lay.width), F32),
        ),
        compiler_params=_cparams("parallel", "arbitrary"),
        name="in_proj",
    )(h, w_t)


def _merge_body(attn_ref, o_ref, ga_ref, gb_ref, wa_ref, wg_ref, out_ref):
    a = jnp.dot(attn_ref[...], wa_ref[...].astype(BF16), preferred_element_type=F32)
    b = jnp.dot(o_ref[...], wg_ref[...].astype(BF16), preferred_element_type=F32)
    out_ref[...] = (jax.nn.sigmoid(ga_ref[...]) * a + jax.nn.sigmoid(gb_ref[...]) * b).astype(out_ref.dtype)


def _merge(attn, o, gates, wa, wg, tm, col_a, col_b, tn):
    M, KA = attn.shape
    D = wa.shape[1]
    assert col_a % tn == 0 and col_b % tn == 0
    a_off, b_off = col_a // tn, col_b // tn
    return pl.pallas_call(
        _merge_body,
        grid=(M // tm, D // tn),
        in_specs=[
            pl.BlockSpec((tm, KA), lambda i, j: (i, 0), pipeline_mode=pl.Buffered(1)),
            pl.BlockSpec((tm, KA), lambda i, j: (i, 0), pipeline_mode=pl.Buffered(1)),
            pl.BlockSpec((tm, tn), lambda i, j: (i, a_off + j)),
            pl.BlockSpec((tm, tn), lambda i, j: (i, b_off + j)),
            pl.BlockSpec((KA, tn), lambda i, j: (0, j)),
            pl.BlockSpec((KA, tn), lambda i, j: (0, j)),
        ],
        out_specs=pl.BlockSpec((tm, tn), lambda i, j: (i, j)),
        out_shape=jax.ShapeDtypeStruct((M, D), BF16),
        compiler_params=_cparams("parallel", "arbitrary"),
        name="merge",
    )(attn, o, gates, gates, wa, wg)


def _resid_mm_body(x_ref, a_ref, w_ref, out_ref):
    out_ref[...] = x_ref[...] + jnp.dot(a_ref[...], w_ref[...].astype(BF16), preferred_element_type=F32)


def _resid_mm(x, a, w, tm, tn):
    M, N = x.shape
    Kd = a.shape[1]
    return pl.pallas_call(
        _resid_mm_body,
        grid=(M // tm, N // tn),
        in_specs=[
            pl.BlockSpec((tm, tn), lambda i, j: (i, j)),
            pl.BlockSpec((tm, Kd), lambda i, j: (i, 0), pipeline_mode=pl.Buffered(1)),
            pl.BlockSpec((Kd, tn), lambda i, j: (0, j)),
        ],
        out_specs=pl.BlockSpec((tm, tn), lambda i, j: (i, j)),
        out_shape=jax.ShapeDtypeStruct((M, N), F32),
        compiler_params=_cparams("parallel", "arbitrary"),
        name="resid_mm",
    )(x, a, w)


def _ffn_up_body(x_ref, nw_ref, wg_ref, wu_ref, out_ref, h_ref):
    @pl.when(pl.program_id(1) == 0)
    def _():
        _rms_rows(x_ref, nw_ref, h_ref, 64)

    h = h_ref[...]
    g = jnp.dot(h, wg_ref[...].astype(BF16), preferred_element_type=F32)
    u = jnp.dot(h, wu_ref[...].astype(BF16), preferred_element_type=F32)
    out_ref[...] = (jax.nn.silu(g) * u).astype(out_ref.dtype)


def _ffn_up(x, norm_w, w_gu, tm, tn):
    M, D = x.shape
    ffp = w_gu.shape[1] // 2
    nj = ffp // tn
    return pl.pallas_call(
        _ffn_up_body,
        grid=(M // tm, nj),
        in_specs=[
            pl.BlockSpec((tm, D), lambda i, j: (i, 0), pipeline_mode=pl.Buffered(1)),
            pl.BlockSpec((1, D), lambda i, j: (0, 0)),
            pl.BlockSpec((D, tn), lambda i, j: (0, j)),
            pl.BlockSpec((D, tn), lambda i, j: (0, nj + j)),
        ],
        out_specs=pl.BlockSpec((tm, tn), lambda i, j: (i, j)),
        out_shape=jax.ShapeDtypeStruct((M, ffp), BF16),
        scratch_shapes=[pltpu.VMEM((tm, D), BF16)],
        compiler_params=_cparams("parallel", "arbitrary"),
        name="ffn_up",
    )(x, norm_w.reshape(1, D), w_gu, w_gu)


def _ple_body(x_ref, nw_ref, wg_ref, p_ref, wp_ref, nf_ref, y_ref, h_ref, *, tn):
    j = pl.program_id(1)

    @pl.when(j == 0)
    def _():
        _rms_rows(x_ref, nw_ref, h_ref, 64)

    g = jnp.dot(h_ref[...], wg_ref[...].astype(BF16), preferred_element_type=F32)
    e = jnp.dot(p_ref[...], wp_ref[...].astype(BF16), preferred_element_type=F32)
    cols = pl.ds(pl.multiple_of(j * tn, tn), tn)
    y_ref[:, cols] = x_ref[:, cols] + jax.nn.sigmoid(g) * e

    @pl.when(j == pl.num_programs(1) - 1)
    def _():
        _rms_rows(y_ref, nf_ref, y_ref, 64)


def _ple_final(x, norm_w, w_gate, p, w_ple, norm_final, row0, n_rows, tm, tn):
    D = x.shape[1]
    P = p.shape[1]
    assert row0 % tm == 0 and n_rows % tm == 0
    r0 = row0 // tm
    return pl.pallas_call(
        functools.partial(_ple_body, tn=tn),
        grid=(n_rows // tm, D // tn),
        in_specs=[
            pl.BlockSpec((tm, D), lambda i, j: (r0 + i, 0), pipeline_mode=pl.Buffered(1)),
            pl.BlockSpec((1, D), lambda i, j: (0, 0)),
            pl.BlockSpec((D, tn), lambda i, j: (0, j)),
            pl.BlockSpec((tm, P), lambda i, j: (r0 + i, 0)),
            pl.BlockSpec((P, tn), lambda i, j: (0, j)),
            pl.BlockSpec((1, D), lambda i, j: (0, 0)),
        ],
        out_specs=pl.BlockSpec((tm, D), lambda i, j: (i, 0)),
        out_shape=jax.ShapeDtypeStruct((n_rows, D), F32),
        scratch_shapes=[pltpu.VMEM((tm, D), BF16)],
        compiler_params=_cparams("parallel", "arbitrary"),
        name="ple_final",
    )(x, norm_w.reshape(1, D), w_gate, p, w_ple, norm_final.reshape(1, D))


def _bucket_thresholds():
    d = np.arange(0, 4 * MAX_DISTANCE)
    max_exact = N_BUCKETS // 2
    large = max_exact + (np.log(np.maximum(d, 1) / max_exact) / math.log(MAX_DISTANCE / max_exact)
                         * (N_BUCKETS - max_exact)).astype(np.int32)
    b = np.where(d < max_exact, d, np.minimum(large, N_BUCKETS - 1))
    return [int(np.argmax(b >= k)) for k in range(N_BUCKETS)]


_BUCKET_THR = _bucket_thresholds()


def _bias_of_dist(rb_ref, h, d):
    v = jnp.full(d.shape, rb_ref[0, h], F32)
    for b in range(1, N_BUCKETS):
        v = jnp.where(d >= _BUCKET_THR[b], rb_ref[b, h], v)
    return v


def _bias_tables_body(rb_ref, tprev_ref, tdiag_ref, tsamp_ref, *, qb, past):
    h = pl.program_id(0)
    r = lax.broadcasted_iota(jnp.int32, (qb, qb), 0)
    c = lax.broadcasted_iota(jnp.int32, (qb, qb), 1)
    far = rb_ref[N_BUCKETS - 1, h]
    tprev_ref[0] = (_bias_of_dist(rb_ref, h, r + qb - c) - far) * LOG2E
    tdiag_ref[0] = (_bias_of_dist(rb_ref, h, r - c) - far) * LOG2E
    ls = tsamp_ref.shape[2]
    t = lax.broadcasted_iota(jnp.int32, (SUBLANES, ls), 0) % 4
    lane = lax.broadcasted_iota(jnp.int32, (SUBLANES, ls), 1)
    d = jnp.where(lane < past, past + t - lane, t - (lane - past) % 4)
    tsamp_ref[0] = _bias_of_dist(rb_ref, h, d)


def _bias_tables(rel_bias, qb, past, ls):
    return pl.pallas_call(
        functools.partial(_bias_tables_body, qb=qb, past=past),
        grid=(N_HEADS,),
        in_specs=[pl.BlockSpec(memory_space=pltpu.SMEM)],
        out_specs=(
            pl.BlockSpec((1, qb, qb), lambda h: (h, 0, 0)),
            pl.BlockSpec((1, qb, qb), lambda h: (h, 0, 0)),
            pl.BlockSpec((1, SUBLANES, ls), lambda h: (h, 0, 0)),
        ),
        out_shape=(
            jax.ShapeDtypeStruct((N_HEADS, qb, qb), F32),
            jax.ShapeDtypeStruct((N_HEADS, qb, qb), F32),
            jax.ShapeDtypeStruct((N_HEADS, SUBLANES, ls), F32),
        ),
        compiler_params=_cparams("arbitrary"),
        name="bias_tables",
    )(rel_bias)


INT_MIN = -2 ** 31


def _sort_key(score, valid):
    bits = lax.bitcast_convert_type(score, jnp.int32)
    bits = jnp.where(bits == jnp.int32(INT_MIN), 0, bits)
    key = jnp.where(bits < 0, bits ^ jnp.int32(0x7FFFFFFF), bits)
    return jnp.where(valid, key, jnp.int32(INT_MIN))


def _ties_needed(count_ge, thr, k):
    above = jnp.where(thr[:, :1] < jnp.int32(2 ** 31 - 1), count_ge(jnp.minimum(thr, jnp.int32(2 ** 31 - 2)) + 1), 0.0)
    return k - above


def _prefix_matrix(n):
    j = lax.broadcasted_iota(jnp.int32, (n, 2 * n), 0)
    l = lax.broadcasted_iota(jnp.int32, (n, 2 * n), 1)
    return jnp.where((j <= l) | (l >= n), 1.0, 0.0).astype(BF16)


def _kth_largest(count_ge, rows, k, bits_per_pass=1):
    assert 32 % bits_per_pass == 0
    n_try = 2 ** bits_per_pass - 1

    def step(b, t):
        unit = lax.shift_left(jnp.int32(1), jnp.asarray(32 - bits_per_pass * (b + 1), jnp.int32))
        best = t
        for m in range(1, n_try + 1):
            cand = t + unit * m
            best = jnp.where(count_ge(cand) >= k, cand, best)
        return best

    return lax.fori_loop(0, 32 // bits_per_pass, step, jnp.full((rows, LANES), INT_MIN, jnp.int32))


PQB = 256
PKC = 256
FAR_GROUP = 4

def _pattn_body(qi_ref, wi_ref, ki_ref, q_ref, k_ref, v_ref, tprev_ref, tdiag_ref, out_ref,
                key_ref, wb_ref, m_ref, l_ref, acc_ref, *, idx_scale, n_sel):
    i = pl.program_id(0)
    qb, kc = PQB, PKC
    hg = 8

    wi = wi_ref[...] * idx_scale
    for h in range(N_IDX_HEADS):
        wb_ref[h] = jnp.broadcast_to(wi[:, h:h + 1], (qb, LANES))

    def chunk(c):
        return pl.ds(pl.multiple_of(c * kc, kc), kc)

    row = lax.broadcasted_iota(jnp.int32, (qb, kc), 0)
    col = lax.broadcasted_iota(jnp.int32, (qb, kc), 1)

    def score_chunk(c, _):
        kic = ki_ref[chunk(c), :]
        acc = [jnp.zeros((qb, LANES), F32) for _ in range(kc // LANES)]
        for g in range(N_IDX_HEADS // hg):
            qg = qi_ref[g * hg:(g + 1) * hg].reshape(hg * qb, IDX_DIM)
            s = lax.dot_general(qg, kic, (((1,), (1,)), ((), ())), preferred_element_type=F32)
            for hh in range(hg):
                w = wb_ref[g * hg + hh]
                for half in range(kc // LANES):
                    sh = s[hh * qb:(hh + 1) * qb, half * LANES:(half + 1) * LANES]
                    acc[half] = acc[half] + jnp.maximum(sh, 0.0) * w
        score = jnp.concatenate(acc, axis=1)
        valid = (c * kc + col) <= (i * qb + row)
        key_ref[:, chunk(c)] = _sort_key(score, valid)
        return 0

    lax.fori_loop(0, i + 1, score_chunk, 0)

    def count_ge(cand):
        def body(c, cnt):
            kk = key_ref[:, chunk(c)]
            for half in range(kc // LANES):
                cnt = cnt + jnp.where(kk[:, half * LANES:(half + 1) * LANES] >= cand, 1.0, 0.0)
            return cnt

        cnt = lax.fori_loop(0, i + 1, body, jnp.zeros((qb, LANES), F32))
        return jnp.sum(cnt, axis=1, keepdims=True)

    thr = _kth_largest(count_ge, qb, float(n_sel))
    t2 = jnp.concatenate([thr] * (kc // LANES), axis=1)
    surplus = jnp.max(count_ge(thr)) > float(n_sel)

    def store_mask(c, sel):
        sel = sel & (key_ref[:, chunk(c)] > jnp.int32(INT_MIN))
        key_ref[:, chunk(c)] = lax.bitcast_convert_type(jnp.where(sel, 0.0, NEG).astype(F32), jnp.int32)

    @pl.when(jnp.logical_not(surplus))
    def _():
        def mask_chunk(c, _):
            store_mask(c, key_ref[:, chunk(c)] >= t2)
            return 0

        lax.fori_loop(0, i + 1, mask_chunk, 0)

    @pl.when(surplus)
    def _():
        need = _ties_needed(count_ge, thr, float(n_sel))
        prefix = _prefix_matrix(kc)

        def mask_chunk(c, seen):
            kk = key_ref[:, chunk(c)]
            eq = kk == t2
            cnt = jnp.dot(jnp.where(eq, 1.0, 0.0).astype(BF16), prefix, preferred_element_type=F32)
            rank = cnt[:, :kc] + jnp.concatenate([seen] * (kc // LANES), axis=1)
            store_mask(c, (kk > t2) | (eq & (rank <= need)))
            return seen + cnt[:, kc:kc + LANES]

        lax.fori_loop(0, i + 1, mask_chunk, jnp.zeros((qb, LANES), F32))

    c1 = HEAD_DIM ** -0.5 * LOG2E

    def kv_head(kh, _):
        qs = q_ref[pl.ds(kh * GROUPS, GROUPS)].reshape(GROUPS * qb, HEAD_DIM)
        lanes = pl.ds(pl.multiple_of(kh * HEAD_DIM, HEAD_DIM), HEAD_DIM)
        m_ref[...] = jnp.full(m_ref.shape, NEG, F32)
        l_ref[...] = jnp.zeros(l_ref.shape, F32)
        acc_ref[...] = jnp.zeros(acc_ref.shape, F32)

        def attend(items):
            ss = [lax.dot_general(qs, k_ref[chunk(c), lanes], (((1,), (1,)), ((), ())), preferred_element_type=F32)
                  for c, _ in items]
            m, l, acc = m_ref[...], l_ref[...], acc_ref[...]
            for (c, bias_of_group), s in zip(items, ss):
                vt = v_ref[chunk(c), lanes]
                vx = jnp.concatenate([vt, jnp.ones_like(vt)], axis=1)
                mb = lax.bitcast_convert_type(key_ref[:, chunk(c)], F32)
                ts = []
                for g in range(GROUPS):
                    t = s[g * qb:(g + 1) * qb] * c1 + mb
                    if bias_of_group is not None:
                        t = t + bias_of_group(g)
                    ts.append(t)
                t = jnp.concatenate(ts, axis=0)
                m_new = jnp.maximum(m, jnp.max(t, axis=1, keepdims=True))
                alpha = jnp.exp2(m - m_new)
                p = jnp.exp2(t - jnp.concatenate([m_new] * (kc // LANES), axis=1))
                pv = jnp.dot(p.astype(BF16), vx, preferred_element_type=F32)
                acc = alpha * acc + pv[:, :HEAD_DIM]
                l = alpha * l + pv[:, HEAD_DIM:]
                m = m_new
            m_ref[...], l_ref[...], acc_ref[...] = m, l, acc

        n_far = jnp.maximum(i - 1, 0)

        def far_group(cc, _):
            attend([(FAR_GROUP * cc + u, None) for u in range(FAR_GROUP)])
            return 0

        n_grouped = n_far // FAR_GROUP * FAR_GROUP
        lax.fori_loop(0, n_far // FAR_GROUP, far_group, 0)
        size, done = FAR_GROUP // 2, n_grouped
        while size >= 1:
            @pl.when(n_far - done >= size)
            def _(done=done, size=size):
                attend([(done + u, None) for u in range(size)])

            done = done + jnp.where(n_far - done >= size, size, 0)
            size //= 2

        @pl.when(i >= 1)
        def _():
            attend([(i - 1, lambda g: tprev_ref[kh * GROUPS + g]), (i, lambda g: tdiag_ref[kh * GROUPS + g])])

        @pl.when(i == 0)
        def _():
            attend([(i, lambda g: tdiag_ref[kh * GROUPS + g])])

        o = acc_ref[...] / l_ref[...]
        for g in range(GROUPS):
            out_ref[:, pl.ds(pl.multiple_of((kh * GROUPS + g) * HEAD_DIM, HEAD_DIM), HEAD_DIM)] = (
                o[g * qb:(g + 1) * qb].astype(out_ref.dtype))
        return 0

    lax.fori_loop(0, N_KV_HEADS, kv_head, 0)


def _prompt_attention(qqi, wi, ki_bf, kv_bf, tprev, tdiag, seq, n_sel):
    qb = PQB
    assert N_IDX_HEADS % N_HEADS == 0
    resident = dict(pipeline_mode=pl.Buffered(1))
    return pl.pallas_call(
        functools.partial(_pattn_body, idx_scale=(N_IDX_HEADS * IDX_DIM) ** -0.5, n_sel=n_sel),
        grid=(seq // qb,),
        in_specs=[
            pl.BlockSpec((N_IDX_HEADS, qb, IDX_DIM), lambda i: (0, i, 0)),
            pl.BlockSpec((qb, N_IDX_HEADS), lambda i: (i, 0)),
            pl.BlockSpec((seq, IDX_DIM), lambda i: (0, 0), **resident),
            pl.BlockSpec((N_HEADS, qb, HEAD_DIM), lambda i: (N_IDX_HEADS // N_HEADS, i, 0)),
            pl.BlockSpec((seq, N_KV_HEADS * HEAD_DIM), lambda i: (0, 0), **resident),
            pl.BlockSpec((seq, N_KV_HEADS * HEAD_DIM), lambda i: (0, 1), **resident),
            pl.BlockSpec((N_HEADS, qb, qb), lambda i: (0, 0, 0), **resident),
            pl.BlockSpec((N_HEADS, qb, qb), lambda i: (0, 0, 0), **resident),
        ],
        out_specs=pl.BlockSpec((qb, N_HEADS * HEAD_DIM), lambda i: (i, 0)),
        out_shape=jax.ShapeDtypeStruct((seq, N_HEADS * HEAD_DIM), BF16),
        scratch_shapes=[
            pltpu.VMEM((qb, seq), jnp.int32),
            pltpu.VMEM((N_IDX_HEADS, qb, LANES), F32),
            pltpu.VMEM((GROUPS * qb, LANES), F32),
            pltpu.VMEM((GROUPS * qb, LANES), F32),
            pltpu.VMEM((GROUPS * qb, HEAD_DIM), F32),
        ],
        compiler_params=_cparams("arbitrary"),
        name="prompt_attention",
    )(qqi, wi, ki_bf, qqi, kv_bf, kv_bf, tprev, tdiag)


GA_LANE = 32
GB_LANE = 48
HALO = SUBLANES
def _split2(a):
    hi = a.astype(BF16)
    return hi, (a - hi.astype(F32)).astype(BF16)


def _split3(a):
    hi = a.astype(BF16)
    r = a - hi.astype(F32)
    mid = r.astype(BF16)
    return hi, mid, (r - mid.astype(F32)).astype(BF16)


def _mm3(a2, b2):
    (ah, al), (bh, bl) = a2, b2
    d = lambda x, y: jnp.dot(x, y, preferred_element_type=F32)
    return d(ah, bh) + (d(ah, bl) + d(al, bh))


def _bmm(a, b):
    return jnp.dot(a.astype(BF16), b.astype(BF16), preferred_element_type=F32)


def _bmm_nt(a, b):
    return lax.dot_general(a.astype(BF16), b.astype(BF16), (((1,), (1,)), ((), ())), preferred_element_type=F32)


def _bmm_tn(a, b):
    return lax.dot_general(a.astype(BF16), b.astype(BF16), (((0,), (0,)), ((), ())), preferred_element_type=F32)


def _gdn_body(qkv_ref, slab_ref, gz_ref, cw_ref, alog_ref, dtb_ref, gn_ref, halo0_ref, s0_ref,
              o_ref, s_ref, xp_ref, *, C, valid_rows):
    c = pl.program_id(1)
    kw, vw = GDN_HEADS * GDN_DK, GDN_HEADS * GDN_DV

    @pl.when(c == 0)
    def _():
        xp_ref[0:HALO] = halo0_ref[0]
        s_ref[...] = s0_ref[...]

    xp_ref[HALO:HALO + C] = qkv_ref[...]

    ri = lax.broadcasted_iota(jnp.int32, (C, C), 0)
    ci = lax.broadcasted_iota(jnp.int32, (C, C), 1)
    causal = ri >= ci
    strict = ri > ci
    eye = jnp.where(ri == ci, 1.0, 0.0).astype(F32)
    ltri = jnp.where(causal, 1.0, 0.0).astype(F32)

    slab = slab_ref[...]
    live = lax.broadcasted_iota(jnp.int32, slab.shape, 0) < valid_rows
    g_all = jnp.where(live, -jnp.exp(alog_ref[...]) * jax.nn.softplus(slab + dtb_ref[...]), 0.0)
    beta_all = jnp.where(live, jax.nn.sigmoid(slab), 0.0)
    ltri_b = ltri.astype(BF16)
    gc_all = sum(jnp.dot(ltri_b, part, preferred_element_type=F32) for part in _split3(g_all))
    pad = jnp.zeros((LANES - C, LANES), F32)
    gc_t = jnp.concatenate([gc_all, pad], axis=0).T

    def conv(cols):
        y = cw_ref[0:1, cols] * xp_ref[HALO - 3:HALO - 3 + C, cols]
        for j in range(1, CONV_WIDTH):
            y = y + cw_ref[j:j + 1, cols] * xp_ref[HALO - 3 + j:HALO - 3 + j + C, cols]
        return jax.nn.silu(y)

    def l2n(x):
        return x * lax.rsqrt(jnp.sum(x * x, axis=-1, keepdims=True) + 1e-6)

    n_sq = max(1, (C - 1).bit_length() - 1)

    hs = range(GDN_HEADS)
    q = [l2n(conv(slice(h * GDN_DK, (h + 1) * GDN_DK))) * GDN_DK ** -0.5 for h in hs]
    k = [l2n(conv(slice(kw + h * GDN_DK, kw + (h + 1) * GDN_DK))) for h in hs]
    v = [conv(slice(2 * kw + h * GDN_DV, 2 * kw + (h + 1) * GDN_DV)) for h in hs]
    beta = [beta_all[:, GB_LANE + h:GB_LANE + h + 1] for h in hs]
    gc = [gc_all[:, GA_LANE + h:GA_LANE + h + 1] for h in hs]
    gc_row = [gc_t[GA_LANE + h:GA_LANE + h + 1, 0:C] for h in hs]
    gc_last = [gc_all[C - 1:C, GA_LANE + h:GA_LANE + h + 1] for h in hs]
    decay = [jnp.exp(jnp.where(causal, gc[h] - gc_row[h], NEG)) for h in hs]
    kb = [k[h] * beta[h] for h in hs]
    p = [-jnp.where(strict, _bmm_nt(kb[h], k[h]) * decay[h], 0.0) for h in hs]
    t_inv = [eye + p[h] for h in hs]
    p2 = [_split2(x) for x in p]
    for _ in range(n_sq):
        p = [_mm3(x2, x2) for x2 in p2]
        p2 = [_split2(x) for x in p]
        t_inv = [t + _mm3(_split2(t), x2) for t, x2 in zip(t_inv, p2)]
    egc = [jnp.exp(gc[h]) for h in hs]
    u = [_bmm(t_inv[h], v[h] * beta[h]) for h in hs]
    w = [_bmm(t_inv[h], kb[h] * egc[h]) for h in hs]
    a_in = [jnp.where(causal, _bmm_nt(q[h], k[h]) * decay[h], 0.0) for h in hs]
    qd = [q[h] * egc[h] for h in hs]
    kt = [k[h] * jnp.exp(gc_last[h] - gc[h]) for h in hs]
    s_old = [s_ref[0, h] for h in hs]
    v_new = [u[h] - _bmm(w[h], s_old[h]) for h in hs]
    o = [_bmm(qd[h], s_old[h]) + _bmm(a_in[h], v_new[h]) for h in hs]
    for h in hs:
        s_ref[0, h] = s_old[h] * jnp.exp(gc_last[h]) + _bmm_tn(kt[h], v_new[h])
    for h in hs:
        on = o[h] * lax.rsqrt(jnp.mean(o[h] * o[h], axis=-1, keepdims=True) + EPS) * gn_ref[...]
        gz = gz_ref[:, h * GDN_DV:(h + 1) * GDN_DV]
        o_ref[:, h * GDN_DV:(h + 1) * GDN_DV] = (on * jax.nn.silu(gz)).astype(o_ref.dtype)

    xp_ref[0:HALO] = xp_ref[C:C + HALO]


def _gdn(qkv, misc, gates, conv_w, alog_vec, dtb_vec, gdn_norm, halo0, s0, *, n_seq, n_chunks, C, valid_rows,
         out_dtype, col_qkv=0, col_slab=None, col_gz=0):
    ch = conv_w.shape[1]
    vw = GDN_HEADS * GDN_DV
    rows = n_seq * n_chunks * C
    col_slab = misc.shape[1] - LANES if col_slab is None else col_slab
    assert col_qkv % ch == 0 and col_slab % LANES == 0 and col_gz % vw == 0

    def rmap(s, c):
        return (s * n_chunks + c, 0)

    def at_col(blk):
        return lambda s, c: (s * n_chunks + c, blk)

    return pl.pallas_call(
        functools.partial(_gdn_body, C=C, valid_rows=valid_rows),
        grid=(n_seq, n_chunks),
        in_specs=[
            pl.BlockSpec((C, ch), at_col(col_qkv // ch)),
            pl.BlockSpec((C, LANES), at_col(col_slab // LANES)),
            pl.BlockSpec((C, vw), at_col(col_gz // vw)),
            pl.BlockSpec((CONV_WIDTH, ch), lambda s, c: (0, 0)),
            pl.BlockSpec((1, LANES), lambda s, c: (0, 0)),
            pl.BlockSpec((1, LANES), lambda s, c: (0, 0)),
            pl.BlockSpec((1, GDN_DV), lambda s, c: (0, 0)),
            pl.BlockSpec((1, HALO, ch), lambda s, c: (s, 0, 0)),
            pl.BlockSpec((1, GDN_HEADS, GDN_DK, GDN_DV), lambda s, c: (s, 0, 0, 0)),
        ],
        out_specs=(
            pl.BlockSpec((C, vw), rmap),
            pl.BlockSpec((1, GDN_HEADS, GDN_DK, GDN_DV), lambda s, c: (s, 0, 0, 0)),
        ),
        out_shape=(
            jax.ShapeDtypeStruct((rows, vw), out_dtype),
            jax.ShapeDtypeStruct((n_seq, GDN_HEADS, GDN_DK, GDN_DV), F32),
        ),
        scratch_shapes=[pltpu.VMEM((C + HALO, ch), F32)],
        compiler_params=_cparams("arbitrary", "arbitrary"),
        name=f"gdn_c{C}",
    )(qkv, misc, gates, conv_w, alog_vec, dtb_vec, gdn_norm.reshape(1, GDN_DV), halo0, s0)


SB = 2
ST = SUBLANES // SB
SCORE_KEYS = 512


def _sattn_body(pt_ref, qi_ref, wi_ref, q_ref, kin_ref, kn_ref, vn_ref, tsamp_ref, cidx_hbm, ck_hbm, cv_hbm, out_ref,
                kibuf, kbuf, vbuf, sem, *, past, n_pages, n_sel, idx_scale):
    p = pl.program_id(0)
    n_steps = pl.num_programs(0)
    ls = kibuf.shape[1]
    rows = SB * ST

    def slot_of(step, bl):
        return (step % 2) * SB + bl

    def copies(step, bl):
        b = step * SB + bl
        sl = slot_of(step, bl)
        out = []
        for j in range(n_pages):
            pg = pt_ref[b, j]
            dst = pl.ds(j * PAGE_SIZE, PAGE_SIZE)
            out.append(pltpu.make_async_copy(cidx_hbm.at[pg], kibuf.at[sl, dst], sem.at[sl]))
            for kh in range(N_KV_HEADS):
                out.append(pltpu.make_async_copy(ck_hbm.at[pg, :, kh], kbuf.at[sl, kh, dst], sem.at[sl]))
                out.append(pltpu.make_async_copy(cv_hbm.at[pg, :, kh], vbuf.at[sl, kh, dst], sem.at[sl]))
        return out

    @pl.when(p == 0)
    def _():
        for bl in range(SB):
            for cp in copies(0, bl):
                cp.start()
        for sl in range(2 * SB):
            kibuf[sl, past:ls] = jnp.zeros((ls - past, IDX_DIM), F32)
            kbuf[sl, :, past:ls] = jnp.zeros((N_KV_HEADS, ls - past, HEAD_DIM), F32)
            vbuf[sl, :, past:ls] = jnp.zeros((N_KV_HEADS, ls - past, HEAD_DIM), F32)

    @pl.when(p + 1 < n_steps)
    def _():
        for bl in range(SB):
            for cp in copies(p + 1, bl):
                cp.start()

    slots = [slot_of(p, bl) for bl in range(SB)]
    bt = lax.broadcasted_iota(jnp.int32, (rows, ls), 0)
    lane = lax.broadcasted_iota(jnp.int32, (rows, ls), 1)
    r_new = lane - past
    valid = (lane < past) | ((r_new < rows) & (r_new // ST == bt // ST) & (r_new % ST <= bt % ST))

    for sl in slots:
        kibuf[sl, past:past + rows] = kin_ref[...]
        for kh in range(N_KV_HEADS):
            kbuf[sl, kh, past:past + rows] = kn_ref[:, kh * HEAD_DIM:(kh + 1) * HEAD_DIM]
            vbuf[sl, kh, past:past + rows] = vn_ref[:, kh * HEAD_DIM:(kh + 1) * HEAD_DIM]

    wi = wi_ref[...] * idx_scale
    qi = qi_ref[...].reshape(N_IDX_HEADS * rows, IDX_DIM).astype(BF16)
    key_chunks = [(s, min(SCORE_KEYS, ls - s)) for s in range(0, ls, SCORE_KEYS)]

    def scores(bl):
        parts = []
        for s0, n in key_chunks:
            kic = kibuf[slots[bl], s0:s0 + n].astype(BF16)
            s = lax.dot_general(qi, kic, (((1,), (1,)), ((), ())), preferred_element_type=F32)
            acc = jnp.zeros((rows, n), F32)
            for h in range(N_IDX_HEADS):
                acc = acc + jnp.maximum(s[h * rows:(h + 1) * rows], 0.0) * wi[:, h:h + 1]
            parts.append(acc)
        return jnp.concatenate(parts, axis=1)

    sc = []
    for bl in range(SB):
        for cp in copies(p, bl):
            cp.wait()
        sc.append(scores(bl))
    score = sc[0]
    for bl in range(1, SB):
        score = jnp.where(bt // ST == bl, sc[bl], score)
    key = _sort_key(score, valid)

    def count_ge(cand):
        cnt = jnp.zeros((rows, LANES), F32)
        for t in range(ls // LANES):
            cnt = cnt + jnp.where(key[:, t * LANES:(t + 1) * LANES] >= cand, 1.0, 0.0)
        return jnp.sum(cnt, axis=1, keepdims=True)

    thr = _kth_largest(count_ge, rows, float(n_sel), bits_per_pass=2)
    need = _ties_needed(count_ge, thr, float(n_sel))
    nt = ls // LANES
    thr_full = jnp.concatenate([thr] * nt, axis=1)
    eq = key == thr_full
    tiles = [jnp.where(eq[:, t * LANES:(t + 1) * LANES], 1.0, 0.0) for t in range(nt)]
    tiles.append(jnp.zeros((rows, LANES), F32))
    cnt = jnp.dot(jnp.concatenate(tiles, axis=0).astype(BF16), _prefix_matrix(LANES), preferred_element_type=F32)
    seen, ranks = jnp.zeros((rows, LANES), F32), []
    for t in range(nt):
        blk = cnt[t * rows:(t + 1) * rows]
        ranks.append(blk[:, :LANES] + seen)
        seen = seen + blk[:, LANES:]
    take = eq & (jnp.concatenate(ranks, axis=1) <= need)
    mb = jnp.where(((key > thr_full) | take) & valid, 0.0, NEG).astype(F32)

    scale = HEAD_DIM ** -0.5
    row_b = lax.broadcasted_iota(jnp.int32, (GROUPS * rows, HEAD_DIM), 0) % rows // ST
    outs = [None] * N_KV_HEADS
    for bl in range(SB):
        for kh in range(N_KV_HEADS):
            qs = q_ref[kh * GROUPS:(kh + 1) * GROUPS].reshape(GROUPS * rows, HEAD_DIM).astype(BF16)
            kt = kbuf[slots[bl], kh].astype(BF16)
            vt = vbuf[slots[bl], kh].astype(BF16)
            s = lax.dot_general(qs, kt, (((1,), (1,)), ((), ())), preferred_element_type=F32) * scale
            s = (s.reshape(GROUPS, rows, ls) + tsamp_ref[kh * GROUPS:(kh + 1) * GROUPS] + mb[None]).reshape(GROUPS * rows, ls)
            m = jnp.max(s, axis=1, keepdims=True)
            e = jnp.exp(s - m)
            o = jnp.dot(e.astype(BF16), vt, preferred_element_type=F32) / jnp.sum(e, axis=1, keepdims=True)
            outs[kh] = o if bl == 0 else jnp.where(row_b == bl, o, outs[kh])

    for kh in range(N_KV_HEADS):
        for g in range(GROUPS):
            hd = kh * GROUPS + g
            out_ref[:, hd * HEAD_DIM:(hd + 1) * HEAD_DIM] = outs[kh][g * rows:(g + 1) * rows]


def _sample_attention(page_table, qi_s, wi_s, q_s, ki_new, k_new, v_new, tsamp, cache_idx, cache_k, cache_v, n_sel):
    R = wi_s.shape[0]
    n_b, n_pages = page_table.shape
    past = n_pages * PAGE_SIZE
    ls = past + PAGE_SIZE
    rows = SB * ST
    assert R == n_b * ST and n_b % SB == 0
    kvw = N_KV_HEADS * HEAD_DIM
    grid_spec = pltpu.PrefetchScalarGridSpec(
        num_scalar_prefetch=1,
        grid=(n_b // SB,),
        in_specs=[
            pl.BlockSpec((N_IDX_HEADS, rows, IDX_DIM), lambda p, pt: (0, p, 0)),
            pl.BlockSpec((rows, N_IDX_HEADS), lambda p, pt: (p, 0)),
            pl.BlockSpec((N_HEADS, rows, HEAD_DIM), lambda p, pt: (0, p, 0)),
            pl.BlockSpec((rows, IDX_DIM), lambda p, pt: (p, 0)),
            pl.BlockSpec((rows, kvw), lambda p, pt: (p, 0)),
            pl.BlockSpec((rows, kvw), lambda p, pt: (p, 0)),
            pl.BlockSpec((N_HEADS, SUBLANES, ls), lambda p, pt: (0, 0, 0)),
            pl.BlockSpec(memory_space=pl.ANY),
            pl.BlockSpec(memory_space=pl.ANY),
            pl.BlockSpec(memory_space=pl.ANY),
        ],
        out_specs=pl.BlockSpec((rows, N_HEADS * HEAD_DIM), lambda p, pt: (p, 0)),
        scratch_shapes=[
            pltpu.VMEM((2 * SB, ls, IDX_DIM), F32),
            pltpu.VMEM((2 * SB, N_KV_HEADS, ls, HEAD_DIM), F32),
            pltpu.VMEM((2 * SB, N_KV_HEADS, ls, HEAD_DIM), F32),
            pltpu.SemaphoreType.DMA((2 * SB,)),
        ],
    )
    return pl.pallas_call(
        functools.partial(_sattn_body, past=past, n_pages=n_pages, n_sel=n_sel,
                          idx_scale=(N_IDX_HEADS * IDX_DIM) ** -0.5),
        grid_spec=grid_spec,
        out_shape=jax.ShapeDtypeStruct((R, N_HEADS * HEAD_DIM), F32),
        compiler_params=_cparams("arbitrary"),
        name="sample_attention",
    )(page_table, qi_s, wi_s, q_s, ki_new, k_new, v_new, tsamp, cache_idx, cache_k, cache_v)


ROW_TILE_CAP = 1088
SMALL_ROW_TILE = 512
OUT_TILE = 512
FF_TILE = 256


def _row_tile(m, cap=ROW_TILE_CAP):
    return max(t for t in range(16, cap + 1, 16) if m % t == 0)


def _pad_tokens(a, db, dt):
    a = a.reshape(db, dt, a.shape[-1])
    return jnp.pad(a, ((0, 0), (0, SUBLANES - dt), (0, 0))).reshape(db * SUBLANES, a.shape[-1])


def kernel(x_prompt, x_sample, cache_k, cache_v, cache_idx_k, state_conv, state_ssm, page_table, p_prompt, p_sample,
           rel_bias, norm_mix, w_in, conv_w, a_log, dt_bias, gdn_norm, w_attn_up, w_gdn_up, w_out, norm_ffn, w_gate_up,
           w_down, norm_ple, w_ple_gate, w_ple, norm_final):
    assert x_prompt.shape[0] == 1 and w_in.shape[0] == 1, "one prompt sequence, one layer"
    _, T, D = x_prompt.shape
    DB, DT, _ = x_sample.shape
    assert DT == ST and T % PQB == 0 and T % GDN_CHUNK == 0 and DT >= CONV_WIDTH - 1
    past = page_table.shape[1] * PAGE_SIZE
    RS = DB * DT
    M = T + RS
    tm = _row_tile(M)
    kw, vw = GDN_HEADS * GDN_DK, GDN_HEADS * GDN_DV
    ch = 2 * kw + vw
    kvw = N_KV_HEADS * HEAD_DIM

    w_t = jnp.swapaxes(w_in[0], 0, 1)
    w_dn = w_down[0].astype(BF16)
    w_pg = w_ple_gate[0].astype(BF16)
    w_pe = w_ple[0].astype(BF16)
    lay = ProjLayout(D)
    alog_vec = jnp.zeros((1, LANES), F32).at[0, GA_LANE:GA_LANE + GDN_HEADS].set(a_log[0])
    dtb_vec = jnp.zeros((1, LANES), F32).at[0, GA_LANE:GA_LANE + GDN_HEADS].set(dt_bias[0])

    x = jnp.concatenate([x_prompt[0], x_sample.reshape(RS, D)], axis=0)
    p_all = jnp.concatenate([p_prompt[0, 0], p_sample[0].reshape(RS, -1)], axis=0).astype(BF16)

    h0 = _rmsnorm_bf16(x, norm_mix[0], _row_tile(M, SMALL_ROW_TILE))
    qqi, kv_bf, pf = _proj(h0, w_t, tm)
    k_f, v_f = pf[:, lay.c_k:lay.c_k + kvw], pf[:, lay.c_v:lay.c_v + kvw]
    ki_f = pf[:, lay.c_ki:lay.c_ki + IDX_DIM]
    wi_f = pf[:, lay.c_wi:lay.c_wi + N_IDX_HEADS]
    qkv_s = pf[T:, lay.c_qkv:lay.c_qkv + ch]

    tprev, tdiag, tsamp = _bias_tables(rel_bias, PQB, past, past + PAGE_SIZE)
    attn_p = _prompt_attention(qqi, wi_f, ki_f.astype(BF16), kv_bf, tprev, tdiag, T, min(TOP_K_MAX, T // 4))
    qqi_s = qqi[:, T:].astype(F32)
    attn_s = _sample_attention(
        page_table, qqi_s[:N_IDX_HEADS], wi_f[T:], qqi_s[N_IDX_HEADS:], ki_f[T:], k_f[T:], v_f[T:], tsamp,
        cache_idx_k[0], cache_k[0], cache_v[0],
        min(TOP_K_MAX, (past + DT) // 4))
    attn = jnp.concatenate([attn_p, attn_s.astype(BF16)], axis=0)

    o_p, ssm_p = _gdn(pf, pf, pf, conv_w[0], alog_vec, dtb_vec, gdn_norm[0],
                      jnp.zeros((1, HALO, ch), F32), jnp.zeros((1, GDN_HEADS, GDN_DK, GDN_DV), F32),
                      n_seq=1, n_chunks=T // GDN_CHUNK, C=GDN_CHUNK, valid_rows=GDN_CHUNK, out_dtype=BF16,
                      col_qkv=lay.c_qkv, col_slab=lay.c_slab, col_gz=lay.c_gz)
    halo_s = jnp.pad(state_conv[0], ((0, 0), (HALO - (CONV_WIDTH - 1), 0), (0, 0)))
    o_s, ssm_s = _gdn(_pad_tokens(qkv_s, DB, DT), _pad_tokens(pf[T:, lay.c_slab:lay.c_slab + LANES], DB, DT),
                      _pad_tokens(pf[T:, lay.c_gz:lay.c_gz + vw], DB, DT),
                      conv_w[0], alog_vec, dtb_vec, gdn_norm[0], halo_s, state_ssm[0],
                      n_seq=DB, n_chunks=1, C=SUBLANES, valid_rows=DT, out_dtype=F32)
    o_s = o_s.reshape(DB, SUBLANES, vw)[:, :DT].reshape(RS, vw)
    o_all = jnp.concatenate([o_p, o_s.astype(BF16)], axis=0)

    merged = _merge(attn, o_all, pf, w_attn_up[0], w_gdn_up[0], tm, lay.c_gate_a, lay.c_gate_b, OUT_TILE)
    x1 = _resid_mm(x, merged, w_out[0], tm, OUT_TILE)
    act = _ffn_up(x1, norm_ffn[0], w_gate_up[0], tm, FF_TILE)
    x2 = _resid_mm(x1, act, w_dn, tm, FF_TILE)
    tm_y = math.gcd(math.gcd(T, RS), SMALL_ROW_TILE)
    y_p = _ple_final(x2, norm_ple[0], w_pg, p_all, w_pe, norm_final, 0, T, tm_y, OUT_TILE)
    y_s = _ple_final(x2, norm_ple[0], w_pg, p_all, w_pe, norm_final, T, RS, tm_y, OUT_TILE)

    def heads(a, n):
        return a.reshape(a.shape[0], n, a.shape[1] // n)

    nc = CONV_WIDTH - 1
    return (
        y_p.reshape(1, T, D),
        y_s.reshape(DB, DT, D),
        heads(k_f[:T], N_KV_HEADS)[None, None],
        heads(v_f[:T], N_KV_HEADS)[None, None],
        ki_f[:T][None, None],
        pf[T - nc:T, lay.c_qkv:lay.c_qkv + ch][None, None],
        ssm_p.astype(state_ssm.dtype)[None],
        heads(k_f[T:], N_KV_HEADS).reshape(1, DB, DT, N_KV_HEADS, HEAD_DIM),
        heads(v_f[T:], N_KV_HEADS).reshape(1, DB, DT, N_KV_HEADS, HEAD_DIM),
        ki_f[T:].reshape(1, DB, DT, IDX_DIM),
        qkv_s.reshape(DB, DT, ch)[:, DT - nc:][None],
        ssm_s.astype(state_ssm.dtype)[None],
    )
```
